```python
import jax, jax.numpy as jnp
from jax import lax
import numpy as np

D_MODEL = 1024
BATCH = 8
SEQ = 2048
DEPTH = 1

ATTN_HEADS = 8
ATTN_HEAD_DIM = 64
ATTN_WIDTH = ATTN_HEADS * ATTN_HEAD_DIM
KV_LATENT = 128
IDX_HEADS = 8
IDX_HEAD_DIM = 64
TOPK_MAX = 256
Q_BLOCK = 128
POOL_WINDOWS = (2, 4, 8, 16)
POOL_GROUP = 128
POOL_WIDTH = len(POOL_WINDOWS) * POOL_GROUP
N_EXPERTS = 32
TOP_K_EXPERTS = 4
D_FF = 1024
SWIGLU_LIMIT = 7.0
SWIGLU_ALPHA = 1.702
MOE_BLOCK = 128
DEEPNORM_ALPHA = (2.0 * DEPTH) ** 0.25
DEEPNORM_BETA = (8.0 * DEPTH) ** -0.25
LN_EPS = 1e-5
N_MOD = 6
IN_SIZES = (ATTN_WIDTH, KV_LATENT, IDX_HEADS * IDX_HEAD_DIM, IDX_HEAD_DIM, IDX_HEADS, POOL_WIDTH, 2 * D_MODEL)
IN_COLS = sum(IN_SIZES)

kernel_name = 'hybrid_dsa_pool_moe_deepnorm_block'


def layer_norm(x, g, b):
    xf = x.astype(jnp.float32)
    mu = jnp.mean(xf, axis=-1, keepdims=True)
    var = jnp.mean(jnp.square(xf - mu), axis=-1, keepdims=True)
    y = (xf - mu) * lax.rsqrt(var + LN_EPS)
    return (y * g.astype(jnp.float32) + b.astype(jnp.float32)).astype(x.dtype)


def rms_norm(x, g):
    xf = x.astype(jnp.float32)
    y = xf * lax.rsqrt(jnp.mean(jnp.square(xf), axis=-1, keepdims=True) + LN_EPS)
    return (y * g.astype(jnp.float32)).astype(x.dtype)


def split_cols(a):
    offs, acc = [], 0
    for s in IN_SIZES[:-1]:
        acc += s
        offs.append(acc)
    return jnp.split(a, offs, axis=-1)


def dsa_attention(q_lat, c_kv, q_idx, k_idx, w_idx):
    B, S = c_kv.shape[0], c_kv.shape[1]
    topk = min(TOPK_MAX, S // 4)
    nb = S // Q_BLOCK
    slopes = 2.0 ** (-8.0 * jnp.arange(1, ATTN_HEADS + 1, dtype=jnp.float32) / ATTN_HEADS)
    key_pos = jnp.arange(S)

    def to_blocks(a):
        return jnp.moveaxis(a.reshape((B, nb, Q_BLOCK) + a.shape[2:]), 1, 0)

    def one_block(args):
        ql, qi, wi, start = args
        qpos = start + jnp.arange(Q_BLOCK)
        causal = key_pos[None, :] <= qpos[:, None]
        dots = jnp.einsum('bqhd,bsd->bqhs', qi, k_idx).astype(jnp.float32)
        score = jnp.einsum('bqh,bqhs->bqs', wi.astype(jnp.float32), jax.nn.relu(dots))
        score = jnp.where(causal[None], score, -jnp.inf)
        _, sel = lax.top_k(score, topk)
        c_sel = jax.vmap(lambda c, i: c[i])(c_kv, sel)
        logits = jnp.einsum('bqhr,bqkr->bqhk', ql, c_sel).astype(jnp.float32)
        dist = (qpos[None, :, None] - sel).astype(jnp.float32)
        logits = logits - slopes[None, None, :, None] * dist[:, :, None, :]
        valid = sel <= qpos[None, :, None]
        logits = jnp.where(valid[:, :, None, :], logits, -jnp.inf)
        p = jax.nn.softmax(logits, axis=-1).astype(c_sel.dtype)
        return jnp.einsum('bqhk,bqkr->bqhr', p, c_sel)

    starts = jnp.arange(nb) * Q_BLOCK
    out = lax.map(one_block, (to_blocks(q_lat), to_blocks(q_idx), to_blocks(w_idx), starts))
    return jnp.moveaxis(out, 0, 1).reshape(B, S, ATTN_HEADS, KV_LATENT)


def multiscale_pool(u):
    B, S, C = u.shape
    uf = u.astype(jnp.float32)
    cs = jnp.concatenate([jnp.zeros((B, 1, C), jnp.float32), jnp.cumsum(uf, axis=1)], axis=1)
    upper = jnp.arange(1, S + 1)
    outs = []
    for g, w in enumerate(POOL_WINDOWS):
        lo, hi = g * POOL_GROUP, (g + 1) * POOL_GROUP
        lower = jnp.maximum(upper - w, 0)
        csg = cs[..., lo:hi]
        count = (upper - lower).astype(jnp.float32)[None, :, None]
        outs.append((csg[:, upper] - csg[:, lower]) / count - uf[..., lo:hi])
    return jnp.concatenate(outs, axis=-1).astype(u.dtype)


def moe(h, w_router, b_router, w_gate_up, b_gate_up, w_down, b_down):
    B, S, D = h.shape
    T = B * S
    hf = h.reshape(T, D)
    logits = (hf @ w_router).astype(jnp.float32) + b_router.astype(jnp.float32)
    top_logit, top_e = lax.top_k(logits, TOP_K_EXPERTS)
    gate = jax.nn.softmax(top_logit, axis=-1)
    A = T * TOP_K_EXPERTS
    flat_e = top_e.reshape(A)
    order = jnp.argsort(flat_e)
    sorted_e = flat_e[order]
    tok = order // TOP_K_EXPERTS
    sizes = jnp.bincount(flat_e, length=N_EXPERTS)
    starts = jnp.cumsum(sizes) - sizes
    padded = (sizes + MOE_BLOCK - 1) // MOE_BLOCK * MOE_BLOCK
    pad_end = jnp.cumsum(padded)
    pad_start = pad_end - padded
    dest = pad_start[sorted_e] + jnp.arange(A) - starts[sorted_e]
    n_blocks = -(-A // MOE_BLOCK) + N_EXPERTS
    x_pad = jnp.zeros((n_blocks * MOE_BLOCK, D), h.dtype).at[dest].set(hf[tok])
    block_e = jnp.minimum(jnp.searchsorted(pad_end, jnp.arange(n_blocks) * MOE_BLOCK, side='right'), N_EXPERTS - 1)

    def expert_block(args):
        xb, e = args
        gu = xb @ w_gate_up[e] + b_gate_up[e]
        x_glu, x_lin = jnp.split(gu, 2, axis=-1)
        x_glu = jnp.minimum(x_glu, SWIGLU_LIMIT)
        x_lin = jnp.clip(x_lin, -SWIGLU_LIMIT, SWIGLU_LIMIT)
        act = x_glu * jax.nn.sigmoid(SWIGLU_ALPHA * x_glu) * (x_lin + 1.0)
        return act @ w_down[e] + b_down[e]

    y_pad = lax.map(expert_block, (x_pad.reshape(n_blocks, MOE_BLOCK, D), block_e)).reshape(-1, D)
    y = y_pad[dest] * gate.reshape(A)[order][:, None].astype(h.dtype)
    return jax.ops.segment_sum(y, tok, num_segments=T).reshape(B, S, D)


def setup_inputs(seed: int = 0) -> dict:
    key = jax.random.key(seed)
    ks = jax.random.split(key, 24)
    f32 = jnp.float32
    L, D, E, F, R = DEPTH, D_MODEL, N_EXPERTS, D_FF, KV_LATENT
    nrm = lambda k, shape, s: jax.random.normal(k, shape, f32) * s
    return {
        'x': nrm(ks[0], (BATCH, SEQ, D), 1.0),
        'c': nrm(ks[1], (BATCH, D), 1.0),
        'w_ada': nrm(ks[2], (L, D, N_MOD * D), 0.5 * D ** -0.5),
        'b_ada': nrm(ks[3], (L, N_MOD * D), 0.01),
        'w_in': nrm(ks[4], (L, D, IN_COLS), D ** -0.5),
        'kv_norm_g': 1.0 + nrm(ks[5], (L, R), 0.02),
        'w_uk': nrm(ks[6], (L, R, ATTN_HEADS, ATTN_HEAD_DIM), R ** -0.5),
        'w_uv': nrm(ks[7], (L, R, ATTN_HEADS, ATTN_HEAD_DIM), R ** -0.5),
        'w_pool_group': nrm(ks[8], (L, len(POOL_WINDOWS), POOL_GROUP, POOL_GROUP), POOL_GROUP ** -0.5),
        'pool_scale': 1.0 + nrm(ks[9], (L, POOL_WIDTH), 0.02),
        'w_branch_attn': nrm(ks[10], (L, ATTN_WIDTH, D), ATTN_WIDTH ** -0.5),
        'w_branch_pool': nrm(ks[11], (L, POOL_WIDTH, D), POOL_WIDTH ** -0.5),
        'w_out': nrm(ks[12], (L, D, D), D ** -0.5 * DEEPNORM_BETA),
        'ln1_g': 1.0 + nrm(ks[13], (L, D), 0.02),
        'ln1_b': nrm(ks[14], (L, D), 0.02),
        'w_router': nrm(ks[15], (L, D, E), D ** -0.5),
        'b_router': nrm(ks[16], (L, E), 0.01),
        'w_gate_up': nrm(ks[17], (L, E, D, 2 * F), D ** -0.5),
        'b_gate_up': nrm(ks[18], (L, E, 2 * F), 0.01),
        'w_down': nrm(ks[19], (L, E, F, D), F ** -0.5 * DEEPNORM_BETA),
        'b_down': nrm(ks[20], (L, E, D), 0.01),
        'ln2_g': 1.0 + nrm(ks[21], (L, D), 0.02),
        'ln2_b': nrm(ks[22], (L, D), 0.02),
    }


def reference(x, c, w_ada, b_ada, w_in, kv_norm_g, w_uk, w_uv, w_pool_group, pool_scale,
              w_branch_attn, w_branch_pool, w_out, ln1_g, ln1_b, w_router, b_router,
              w_gate_up, b_gate_up, w_down, b_down, ln2_g, ln2_b):
    B, S, D = x.shape
    cond = jax.nn.silu(c)
    for l in range(DEPTH):
        mod = (cond @ w_ada[l] + b_ada[l]).reshape(B, N_MOD, 1, D)
        shift1, scale1, gate1, shift2, scale2, gate2 = (mod[:, 0], mod[:, 1], mod[:, 2], mod[:, 3], mod[:, 4], mod[:, 5])

        h = x * (1.0 + scale1) + shift1
        q, ckv, qi, ki, wi, u, g = split_cols(h @ w_in[l])
        q = q.reshape(B, S, ATTN_HEADS, ATTN_HEAD_DIM)
        q_lat = jnp.einsum('bshd,rhd->bshr', q, w_uk[l]) * (ATTN_HEAD_DIM ** -0.5)
        ckv = rms_norm(ckv, kv_norm_g[l])
        qi = qi.reshape(B, S, IDX_HEADS, IDX_HEAD_DIM) * (IDX_HEAD_DIM ** -0.5)
        wi = wi * (IDX_HEADS ** -0.5)
        o_lat = dsa_attention(q_lat, ckv, qi, ki, wi)
        y_attn = jnp.einsum('bshr,rhd->bshd', o_lat, w_uv[l]).reshape(B, S, ATTN_WIDTH)
        pooled = multiscale_pool(u).reshape(B, S, len(POOL_WINDOWS), POOL_GROUP)
        y_pool = jnp.einsum('bsgc,gcd->bsgd', pooled, w_pool_group[l]).reshape(B, S, POOL_WIDTH) * pool_scale[l]
        g_attn, g_pool = g[..., :D], g[..., D:]
        mix = jax.nn.sigmoid(g_attn) * (y_attn @ w_branch_attn[l]) + jax.nn.sigmoid(g_pool) * (y_pool @ w_branch_pool[l])
        x = layer_norm(DEEPNORM_ALPHA * x + gate1 * (mix @ w_out[l]), ln1_g[l], ln1_b[l])

        h2 = x * (1.0 + scale2) + shift2
        y_moe = moe(h2, w_router[l], b_router[l], w_gate_up[l], b_gate_up[l], w_down[l], b_down[l])
        x = layer_norm(DEEPNORM_ALPHA * x + gate2 * y_moe, ln2_g[l], ln2_b[l])
    return x
```

```python
import functools
import math

import jax
import jax.numpy as jnp
from jax import lax
from jax.experimental import pallas as pl
from jax.experimental.pallas import tpu as pltpu

F32 = jnp.float32
BF16 = jnp.bfloat16
I32 = jnp.int32
HIGHEST = lax.Precision.HIGHEST

LN_EPS = 1e-5
TOPK_MAX = 256
TOP_K_EXPERTS = 4
POOL_WINDOWS = (2, 4, 8, 16)
SWIGLU_LIMIT = 7.0
SWIGLU_ALPHA = 1.702
N_MOD = 6
INT_MIN = -(2 ** 31)
NEG_BIG = -1e30
LANES = 128
POOL_HALO = 16

NT_DIMS = (((1,), (1,)), ((), ()))


def _layer_norm(z, g, b):
    mu = jnp.mean(z, axis=-1, keepdims=True)
    zc = z - mu
    var = jnp.mean(zc * zc, axis=-1, keepdims=True)
    return zc * lax.rsqrt(var + LN_EPS) * g + b


def _ada_kernel(c_ref, w_ref, b_ref, o_ref):
    c = c_ref[...]
    cond = c * jax.nn.sigmoid(c)
    o_ref[...] = jnp.dot(cond, w_ref[...], precision=HIGHEST, preferred_element_type=F32) + b_ref[...]


def _ada(c, w_ada, b_ada):
    B, D = c.shape
    N = w_ada.shape[1]
    tn = D
    return pl.pallas_call(
        _ada_kernel,
        grid=(N // tn,),
        in_specs=[pl.BlockSpec((B, D), lambda j: (0, 0)),
                  pl.BlockSpec((D, tn), lambda j: (0, j)),
                  pl.BlockSpec((1, tn), lambda j: (0, j))],
        out_specs=pl.BlockSpec((B, tn), lambda j: (0, j)),
        out_shape=jax.ShapeDtypeStruct((B, N), F32),
    )(c, w_ada, b_ada.reshape(1, N))


def _inproj_kernel(x_ref, mod_ref, wq, wckv, wqi, wki, wwi, wu, wg, kvg_ref,
                   q_o, ckv_o, qi_o, ki_o, wi_o, u_o, g_o, *, qi_scale, wi_scale):
    x = x_ref[0]
    mod = mod_ref[0]
    h = (x * (1.0 + mod[1:2]) + mod[0:1]).astype(BF16)

    def proj(w):
        return jnp.dot(h, w[...], preferred_element_type=F32)

    q_o[0] = proj(wq).astype(BF16)
    ckv = proj(wckv)
    ckv = ckv * lax.rsqrt(jnp.mean(ckv * ckv, axis=-1, keepdims=True) + LN_EPS) * kvg_ref[...]
    ckv_o[0] = ckv.astype(BF16)
    qi_o[0] = (proj(wqi) * qi_scale).astype(BF16)
    ki_o[0] = proj(wki).astype(BF16)
    wi_o[0] = proj(wwi) * wi_scale
    u_o[0] = proj(wu)
    g_o[0] = proj(wg).astype(BF16)


def _inproj(x, mod, ws, kv_norm_g, *, qi_scale, wi_scale, tm):
    B, S, D = x.shape
    wq, wckv, wqi, wki, wwi, wu, wg = ws
    outs = [(wq.shape[1], BF16), (wckv.shape[1], BF16), (wqi.shape[1], BF16), (wki.shape[1], BF16),
            (wwi.shape[1], F32), (wu.shape[1], F32), (wg.shape[1], BF16)]
    row = lambda n: pl.BlockSpec((1, tm, n), lambda b, i: (b, i, 0))
    full = lambda a: pl.BlockSpec(a.shape, lambda b, i: (0,) * a.ndim)
    return pl.pallas_call(
        functools.partial(_inproj_kernel, qi_scale=qi_scale, wi_scale=wi_scale),
        grid=(B, S // tm),
        in_specs=[row(D), pl.BlockSpec((1, N_MOD, D), lambda b, i: (b, 0, 0))]
                 + [full(w) for w in ws] + [full(kv_norm_g)],
        out_specs=[row(n) for n, _ in outs],
        out_shape=[jax.ShapeDtypeStruct((B, S, n), dt) for n, dt in outs],
        compiler_params=pltpu.CompilerParams(
            dimension_semantics=("parallel", "parallel"), vmem_limit_bytes=48 * 2 ** 20),
    )(x, mod, *ws, kv_norm_g)


def _attn_kernel(qi_ref, wi_ref, ki_ref, q_ref, ckv_ref, wuk_ref, wuv_ref, y_ref, key_scr, bias_scr,
                 *, tq, S, H, dh, Hi, di, topk, q_scale, slopes, chunk):
    j = pl.program_id(1)
    rowpos = j * tq + lax.broadcasted_iota(I32, (tq, 1), 0)

    wi = wi_ref[0]
    qi = qi_ref[0]
    for c in range(S // chunk):
        kic = ki_ref[0, c * chunk:(c + 1) * chunk, :]
        acc = jnp.zeros((tq, chunk), F32)
        for h in range(Hi):
            d = lax.dot_general(qi[:, h * di:(h + 1) * di], kic, NT_DIMS, preferred_element_type=F32)
            acc = acc + wi[:, h:h + 1] * jnp.maximum(d, 0.0)
        bits = pltpu.bitcast(acc, I32)
        key = jnp.where(bits >= 0, bits, bits ^ 0x7FFFFFFF)
        col = c * chunk + lax.broadcasted_iota(I32, (tq, chunk), 1)
        key_scr[:, c * chunk:(c + 1) * chunk] = jnp.where(col <= rowpos, key, INT_MIN)

    def thr_step(i, t):
        cand = t + lax.shift_left(jnp.int32(1), 31 - i)
        cnt = jnp.sum((key_scr[...] >= cand).astype(I32), axis=1, keepdims=True)
        return jnp.where(cnt >= topk, cand, t)

    thr = lax.fori_loop(0, 32, thr_step, jnp.full((tq, 1), INT_MIN, I32))

    key = key_scr[...]
    col = lax.broadcasted_iota(I32, (tq, S), 1)
    gt = key > thr
    eq = key == thr
    n_gt = jnp.sum(gt.astype(I32), axis=1, keepdims=True)
    n_eq = jnp.sum(eq.astype(I32), axis=1, keepdims=True)
    need = topk - n_gt

    def tie_search():
        def step(i, jm):
            cand = jm + lax.shift_right_logical(jnp.int32(2 ** (nbits - 1)), i)
            f = jnp.sum((eq & (col < cand)).astype(I32), axis=1, keepdims=True)
            return jnp.where(f < need, cand, jm)
        return lax.fori_loop(0, nbits, step, jnp.zeros((tq, 1), I32))

    nbits = max(1, (S - 1).bit_length())
    any_tie = jnp.max(n_eq - need) > 0
    jmax = lax.cond(any_tie, tie_search, lambda: jnp.full((tq, 1), S, I32))
    sel = (gt | (eq & (col <= jmax))) & (col <= rowpos)
    bias_scr[...] = jnp.where(sel, 0.0, NEG_BIG)

    dist = (rowpos - col).astype(F32)
    ckv = ckv_ref[0]
    q = q_ref[0]
    for h in range(H):
        ql = jnp.dot(q[:, h * dh:(h + 1) * dh], wuk_ref[h], preferred_element_type=F32) * q_scale
        lg = lax.dot_general(ql.astype(BF16), ckv, NT_DIMS, preferred_element_type=F32)
        lg = lg - slopes[h] * dist + bias_scr[...]
        m = jnp.max(lg, axis=1, keepdims=True)
        p = jnp.exp(lg - m)
        l = jnp.sum(p, axis=1, keepdims=True)
        o = jnp.dot(p.astype(BF16), ckv, preferred_element_type=F32) / l
        yh = jnp.dot(o.astype(BF16), wuv_ref[h], preferred_element_type=F32)
        y_ref[0, :, h * dh:(h + 1) * dh] = yh.astype(BF16)


def _attention(qi, wi, ki, q, ckv, wuk, wuv, *, tq, topk):
    B, S, _ = q.shape
    H, dh, R = wuk.shape
    di = ki.shape[2]
    Hi = qi.shape[2] // di
    slopes = tuple(2.0 ** (-8.0 * (h + 1) / H) for h in range(H))
    kern = functools.partial(_attn_kernel, tq=tq, S=S, H=H, dh=dh, Hi=Hi, di=di, topk=topk,
                             q_scale=dh ** -0.5, slopes=slopes, chunk=min(S, 512))
    row = lambda n: pl.BlockSpec((1, tq, n), lambda b, j: (b, j, 0))
    seq = lambda n: pl.BlockSpec((1, S, n), lambda b, j: (b, 0, 0))
    full = lambda a: pl.BlockSpec(a.shape, lambda b, j: (0,) * a.ndim)
    return pl.pallas_call(
        kern,
        grid=(B, S // tq),
        in_specs=[row(qi.shape[2]), row(wi.shape[2]), seq(di), row(q.shape[2]), seq(R), full(wuk), full(wuv)],
        out_specs=row(H * dh),
        out_shape=jax.ShapeDtypeStruct((B, S, H * dh), BF16),
        scratch_shapes=[pltpu.VMEM((tq, S), I32), pltpu.VMEM((tq, S), F32)],
        compiler_params=pltpu.CompilerParams(
            dimension_semantics=("parallel", "parallel"), vmem_limit_bytes=48 * 2 ** 20),
    )(qi, wi, ki, q, ckv, wuk, wuv)


def _merge_kernel(x_ref, ya_ref, u_ref, halo_ref, g_ref, mod_ref, wpg_ref, psc_ref, wba_ref, wbp_ref,
                  wout_ref, ln1g_ref, ln1b_ref, wr_ref, br_ref,
                  x1_o, h2_o, te_o, tg_o, ext_scr, *, tm, D, G, alpha, E, K):
    i = pl.program_id(1)

    @pl.when(i == 0)
    def _():
        ext_scr[0:POOL_HALO, :] = jnp.zeros((POOL_HALO, ext_scr.shape[1]), F32)

    @pl.when(i > 0)
    def _():
        ext_scr[0:POOL_HALO, :] = halo_ref[0]

    ext_scr[POOL_HALO:POOL_HALO + tm, :] = u_ref[0]

    pos = i * tm + lax.broadcasted_iota(I32, (tm, 1), 0)
    ys = []
    for g, w in enumerate(POOL_WINDOWS):
        cols = slice(g * G, (g + 1) * G)
        cur = ext_scr[POOL_HALO:POOL_HALO + tm, cols]
        acc = cur
        for back in range(1, w):
            acc = acc + ext_scr[POOL_HALO - back:POOL_HALO - back + tm, cols]
        cnt = jnp.minimum(pos + 1, w).astype(F32)
        pooled = acc / cnt - cur
        ys.append(jnp.dot(pooled.astype(BF16), wpg_ref[g], preferred_element_type=F32))
    y_pool = jnp.concatenate(ys, axis=1) * psc_ref[...]

    gates = g_ref[0]
    a = jnp.dot(ya_ref[0], wba_ref[...], preferred_element_type=F32)
    p = jnp.dot(y_pool.astype(BF16), wbp_ref[...], preferred_element_type=F32)
    mix = jax.nn.sigmoid(gates[:, :D].astype(F32)) * a + jax.nn.sigmoid(gates[:, D:].astype(F32)) * p
    o = jnp.dot(mix.astype(BF16), wout_ref[...], preferred_element_type=F32)
    mod = mod_ref[0]
    x1 = _layer_norm(alpha * x_ref[0] + mod[2:3] * o, ln1g_ref[...], ln1b_ref[...])
    x1_o[0] = x1
    h2 = x1 * (1.0 + mod[4:5]) + mod[3:4]
    h2_o[0] = h2

    logits = jnp.dot(h2, wr_ref[...], precision=HIGHEST, preferred_element_type=F32) + br_ref[...]
    lane = lax.broadcasted_iota(I32, logits.shape, 1)
    logits = jnp.where(lane < E, logits, -jnp.inf)
    te = jnp.zeros(logits.shape, I32)
    vals = []
    for k in range(K):
        m = jnp.max(logits, axis=1, keepdims=True)
        idx = jnp.min(jnp.where(logits == m, lane, LANES), axis=1, keepdims=True)
        te = jnp.where(lane == k, idx, te)
        vals.append(m)
        logits = jnp.where(lane == idx, -jnp.inf, logits)
    ex = [jnp.exp(v - vals[0]) for v in vals]
    den = ex[0]
    for e in ex[1:]:
        den = den + e
    tg = jnp.zeros(logits.shape, F32)
    for k in range(K):
        tg = jnp.where(lane == k, ex[k] / den, tg)
    te_o[0] = te
    tg_o[0] = tg


def _merge(x, ya, u, g, mod, wpg, psc, wba, wbp, wout, ln1g, ln1b, wr, br, *, tm, alpha, E, K):
    B, S, D = x.shape
    PW = u.shape[2]
    G = wpg.shape[1]
    kern = functools.partial(_merge_kernel, tm=tm, D=D, G=G, alpha=alpha, E=E, K=K)
    row = lambda n: pl.BlockSpec((1, tm, n), lambda b, i: (b, i, 0))
    full = lambda a: pl.BlockSpec(a.shape, lambda b, i: (0,) * a.ndim)
    hb = tm // POOL_HALO
    halo = pl.BlockSpec((1, POOL_HALO, PW), lambda b, i: (b, jnp.maximum(i * hb - 1, 0), 0))
    return pl.pallas_call(
        kern,
        grid=(B, S // tm),
        in_specs=[row(D), row(ya.shape[2]), row(PW), halo, row(g.shape[2]),
                  pl.BlockSpec((1, N_MOD, D), lambda b, i: (b, 0, 0)),
                  full(wpg), full(psc), full(wba), full(wbp), full(wout), full(ln1g), full(ln1b),
                  full(wr), full(br)],
        out_specs=[row(D), row(D), row(LANES), row(LANES)],
        out_shape=[jax.ShapeDtypeStruct((B, S, D), F32), jax.ShapeDtypeStruct((B, S, D), F32),
                   jax.ShapeDtypeStruct((B, S, LANES), I32), jax.ShapeDtypeStruct((B, S, LANES), F32)],
        scratch_shapes=[pltpu.VMEM((POOL_HALO + tm, PW), F32)],
        compiler_params=pltpu.CompilerParams(
            dimension_semantics=("parallel", "arbitrary"), vmem_limit_bytes=48 * 2 ** 20),
    )(x, ya, u, u, g, mod, wpg, psc, wba, wbp, wout, ln1g, ln1b, wr, br)


def _expert_kernel(be_ref, nu_ref, tok_ref, h2_hbm, wgu_ref, bgu_ref, wd_ref, bd_ref, y_ref, xbuf, sem,
                   *, tmE, F):
    i = pl.program_id(0)

    @pl.when(i < nu_ref[0])
    def _():
        def issue(r, carry):
            t = tok_ref[0, 0, r]
            pltpu.make_async_copy(h2_hbm.at[pl.ds(t, 1), :], xbuf.at[pl.ds(r, 1), :], sem).start()
            return carry

        lax.fori_loop(0, tmE, issue, 0)

        def drain(r, carry):
            pltpu.make_async_copy(h2_hbm.at[pl.ds(0, 1), :], xbuf.at[pl.ds(r, 1), :], sem).wait()
            return carry

        lax.fori_loop(0, tmE, drain, 0)

        x = xbuf[...].astype(BF16)
        gu = jnp.dot(x, wgu_ref[0], preferred_element_type=F32) + bgu_ref[0]
        glu = jnp.minimum(gu[:, :F], SWIGLU_LIMIT)
        lin = jnp.clip(gu[:, F:], -SWIGLU_LIMIT, SWIGLU_LIMIT)
        act = glu * jax.nn.sigmoid(SWIGLU_ALPHA * glu) * (lin + 1.0)
        y_ref[...] = jnp.dot(act.astype(BF16), wd_ref[0], preferred_element_type=F32) + bd_ref[0]

    @pl.when(i >= nu_ref[0])
    def _():
        y_ref[...] = jnp.zeros(y_ref.shape, F32)


def _experts(block_e, n_used, row_tok, h2, wgu, bgu, wd, bd, *, tmE):
    T, D = h2.shape
    E, _, F2 = wgu.shape
    F = F2 // 2
    n_blocks = block_e.shape[0]
    grid_spec = pltpu.PrefetchScalarGridSpec(
        num_scalar_prefetch=2,
        grid=(n_blocks,),
        in_specs=[pl.BlockSpec((1, 1, tmE), lambda i, be, nu: (i, 0, 0), memory_space=pltpu.SMEM),
                  pl.BlockSpec(memory_space=pl.ANY),
                  pl.BlockSpec((1, D, F2), lambda i, be, nu: (be[i], 0, 0)),
                  pl.BlockSpec((1, 1, F2), lambda i, be, nu: (be[i], 0, 0)),
                  pl.BlockSpec((1, F, D), lambda i, be, nu: (be[i], 0, 0)),
                  pl.BlockSpec((1, 1, D), lambda i, be, nu: (be[i], 0, 0))],
        out_specs=pl.BlockSpec((tmE, D), lambda i, be, nu: (i, 0)),
        scratch_shapes=[pltpu.VMEM((tmE, D), F32), pltpu.SemaphoreType.DMA(())],
    )
    return pl.pallas_call(
        functools.partial(_expert_kernel, tmE=tmE, F=F),
        grid_spec=grid_spec,
        out_shape=jax.ShapeDtypeStruct((n_blocks * tmE, D), F32),
        compiler_params=pltpu.CompilerParams(
            dimension_semantics=("arbitrary",), vmem_limit_bytes=56 * 2 ** 20),
    )(block_e, n_used, row_tok.reshape(n_blocks, 1, tmE), h2, wgu, bgu, wd, bd)


def _combine_kernel(pos_ref, y_hbm, x1_ref, tg_ref, mod_ref, g_ref, b_ref, o_ref, buf, sem, *, tmC, K, alpha):
    for k in range(K):
        def issue(r, carry):
            p = pos_ref[0, 0, k * tmC + r]
            pltpu.make_async_copy(y_hbm.at[pl.ds(p, 1), :], buf.at[k, pl.ds(r, 1), :], sem).start()
            return carry

        lax.fori_loop(0, tmC, issue, 0)

    for k in range(K):
        def drain(r, carry):
            pltpu.make_async_copy(y_hbm.at[pl.ds(0, 1), :], buf.at[k, pl.ds(r, 1), :], sem).wait()
            return carry

        lax.fori_loop(0, tmC, drain, 0)

    tg = tg_ref[...]
    y = tg[:, 0:1] * buf[0]
    for k in range(1, K):
        y = y + tg[:, k:k + 1] * buf[k]
    mod = mod_ref[0]
    o_ref[...] = _layer_norm(alpha * x1_ref[...] + mod[5:6] * y, g_ref[...], b_ref[...])


def _combine(pos, y_sorted, x1, tg, mod, ln2g, ln2b, *, tmC, K, alpha, S):
    T, D = x1.shape
    nt = T // tmC
    per_seq = S // tmC
    return pl.pallas_call(
        functools.partial(_combine_kernel, tmC=tmC, K=K, alpha=alpha),
        grid=(nt,),
        in_specs=[pl.BlockSpec((1, 1, K * tmC), lambda i: (i, 0, 0), memory_space=pltpu.SMEM),
                  pl.BlockSpec(memory_space=pl.ANY),
                  pl.BlockSpec((tmC, D), lambda i: (i, 0)),
                  pl.BlockSpec((tmC, LANES), lambda i: (i, 0)),
                  pl.BlockSpec((1, N_MOD, D), lambda i: (i // per_seq, 0, 0)),
                  pl.BlockSpec((1, D), lambda i: (0, 0)),
                  pl.BlockSpec((1, D), lambda i: (0, 0))],
        out_specs=pl.BlockSpec((tmC, D), lambda i: (i, 0)),
        out_shape=jax.ShapeDtypeStruct((T, D), F32),
        scratch_shapes=[pltpu.VMEM((K, tmC, D), F32), pltpu.SemaphoreType.DMA(())],
        compiler_params=pltpu.CompilerParams(
            dimension_semantics=("arbitrary",), vmem_limit_bytes=48 * 2 ** 20),
    )(pos, y_sorted, x1, tg, mod, ln2g, ln2b)


def _routing_tables(top_e, E, tmE, tmC):
    T, K = top_e.shape
    A = T * K
    flat_e = top_e.reshape(A)
    order = jnp.argsort(flat_e, stable=True)
    sorted_e = flat_e[order]
    tok = (order // K).astype(I32)
    sizes = jnp.bincount(flat_e, length=E)
    starts = jnp.cumsum(sizes) - sizes
    padded = (sizes + tmE - 1) // tmE * tmE
    pad_end = jnp.cumsum(padded)
    pad_start = pad_end - padded
    dest = (pad_start[sorted_e] + jnp.arange(A) - starts[sorted_e]).astype(I32)
    n_blocks = -(-A // tmE) + E
    row_tok = jnp.zeros((n_blocks * tmE,), I32).at[dest].set(tok)
    block_e = jnp.minimum(jnp.searchsorted(pad_end, jnp.arange(n_blocks) * tmE, side='right'), E - 1).astype(I32)
    n_used = (pad_end[-1] // tmE).astype(I32).reshape(1)
    pos_flat = jnp.zeros((A,), I32).at[order].set(dest)
    pos = pos_flat.reshape(T // tmC, tmC, K).transpose(0, 2, 1).reshape(T // tmC, 1, K * tmC)
    return row_tok, block_e, n_used, pos


def _layer(x, mod, w_in, kv_norm_g, w_uk, w_uv, w_pool_group, pool_scale, w_branch_attn, w_branch_pool,
           w_out, ln1_g, ln1_b, w_router, b_router, w_gate_up, b_gate_up, w_down, b_down, ln2_g, ln2_b,
           *, alpha):
    B, S, D = x.shape
    R, H, dh = w_uk.shape
    AW = H * dh
    PW = pool_scale.shape[0]
    E = w_router.shape[1]
    K = TOP_K_EXPERTS
    n_idx = w_in.shape[1] - (AW + R + PW + 2 * D)
    di = dh
    Hi = (n_idx - di) // (di + 1)
    sizes = (AW, R, Hi * di, di, Hi, PW, 2 * D)
    offs = [0]
    for s in sizes:
        offs.append(offs[-1] + s)
    ws = [w_in[:, offs[k]:offs[k + 1]].astype(BF16) for k in range(len(sizes))]
    ws[4] = jnp.pad(ws[4], ((0, 0), (0, LANES - Hi)))

    q, ckv, qi, ki, wi, u, g = _inproj(x, mod, ws, kv_norm_g.reshape(1, R),
                                       qi_scale=di ** -0.5, wi_scale=Hi ** -0.5, tm=min(S, 512))

    topk = min(TOPK_MAX, S // 4)
    wuk = w_uk.transpose(1, 2, 0).astype(BF16)
    wuv = w_uv.transpose(1, 0, 2).astype(BF16)
    y_attn = _attention(qi, wi, ki, q, ckv, wuk, wuv, tq=min(S, 128), topk=topk)

    wr = jnp.pad(w_router, ((0, 0), (0, LANES - E)))
    br = jnp.pad(b_router, (0, LANES - E)).reshape(1, LANES)
    x1, h2, te, tg = _merge(x, y_attn, u, g, mod, w_pool_group.astype(BF16), pool_scale.reshape(1, PW),
                            w_branch_attn.astype(BF16), w_branch_pool.astype(BF16), w_out.astype(BF16),
                            ln1_g.reshape(1, D), ln1_b.reshape(1, D), wr, br,
                            tm=min(S, 256), alpha=alpha, E=E, K=K)

    T = B * S
    tmE = 256
    tmC = min(S, 256)
    row_tok, block_e, n_used, pos = _routing_tables(te.reshape(T, LANES)[:, :K], E, tmE, tmC)
    y_sorted = _experts(block_e, n_used, row_tok, h2.reshape(T, D),
                        w_gate_up.astype(BF16), b_gate_up.reshape(E, 1, -1),
                        w_down.astype(BF16), b_down.reshape(E, 1, D), tmE=tmE)
    out = _combine(pos, y_sorted, x1.reshape(T, D), tg.reshape(T, LANES), mod,
                   ln2_g.reshape(1, D), ln2_b.reshape(1, D), tmC=tmC, K=K, alpha=alpha, S=S)
    return out.reshape(B, S, D)


def kernel(x, c, w_ada, b_ada, w_in, kv_norm_g, w_uk, w_uv, w_pool_group, pool_scale, w_branch_attn,
           w_branch_pool, w_out, ln1_g, ln1_b, w_router, b_router, w_gate_up, b_gate_up, w_down, b_down,
           ln2_g, ln2_b):
    B, S, D = x.shape
    depth = w_ada.shape[0]
    alpha = (2.0 * depth) ** 0.25
    for l in range(depth):
        mod = _ada(c, w_ada[l], b_ada[l]).reshape(B, N_MOD, D)
        x = _layer(x, mod, w_in[l], kv_norm_g[l], w_uk[l], w_uv[l], w_pool_group[l], pool_scale[l],
                   w_branch_attn[l], w_branch_pool[l], w_out[l], ln1_g[l], ln1_b[l], w_router[l],
                   b_router[l], w_gate_up[l], b_gate_up[l], w_down[l], b_down[l], ln2_g[l], ln2_b[l],
                   alpha=alpha)
    return x
```

```python
import functools

import jax
import jax.numpy as jnp
from jax import lax
from jax.experimental import pallas as pl
from jax.experimental.pallas import tpu as pltpu

F32 = jnp.float32
BF16 = jnp.bfloat16
I32 = jnp.int32
HIGHEST = lax.Precision.HIGHEST

LN_EPS = 1e-5
TOPK_MAX = 256
TOP_K_EXPERTS = 4
POOL_WINDOWS = (2, 4, 8, 16)
SWIGLU_LIMIT = 7.0
SWIGLU_ALPHA = 1.702
N_MOD = 6
INT_MIN = -(2 ** 31)
NEG_BIG = -1e30
LANES = 128
POOL_HALO = 16


def _layer_norm(z, g, b):
    mu = jnp.mean(z, axis=-1, keepdims=True)
    zc = z - mu
    var = jnp.mean(zc * zc, axis=-1, keepdims=True)
    return zc * lax.rsqrt(var + LN_EPS) * g + b


def _ada_kernel(c_ref, w_ref, b_ref, o_ref):
    c = c_ref[...]
    cond = c * jax.nn.sigmoid(c)
    o_ref[...] = jnp.dot(cond, w_ref[...], precision=HIGHEST, preferred_element_type=F32) + b_ref[...]


def _ada(c, w_ada, b_ada):
    B, D = c.shape
    N = w_ada.shape[1]
    tn = D
    return pl.pallas_call(
        _ada_kernel,
        grid=(N // tn,),
        in_specs=[pl.BlockSpec((B, D), lambda j: (0, 0)),
                  pl.BlockSpec((D, tn), lambda j: (0, j)),
                  pl.BlockSpec((1, tn), lambda j: (0, j))],
        out_specs=pl.BlockSpec((B, tn), lambda j: (0, j)),
        out_shape=jax.ShapeDtypeStruct((B, N), F32),
    )(c, w_ada, b_ada.reshape(1, N))


def _inproj_kernel(x_ref, mod_ref, wq, wckv, wqi, wki, wwi, wu, wg, kvg_ref, wukt_ref,
                   qlt_o, ckva_o, ckvt_o, qit_o, ki_o, wit_o, u_o, g_o,
                   *, tm, H, dh, Hi, q_scale, qi_scale, wi_scale):
    i = pl.program_id(1)
    x = x_ref[0]
    mod = mod_ref[0]
    h = (x * (1.0 + mod[1:2]) + mod[0:1]).astype(BF16)

    def proj(w):
        return jnp.dot(h, w[...], preferred_element_type=F32)

    qt = proj(wq).T.astype(BF16)
    for hh in range(H):
        qlt = jnp.dot(wukt_ref[hh], qt[hh * dh:(hh + 1) * dh, :], preferred_element_type=F32) * q_scale
        qlt_o[0, hh] = qlt.astype(BF16)

    ckv = proj(wckv)
    ckv = ckv * lax.rsqrt(jnp.mean(ckv * ckv, axis=-1, keepdims=True) + LN_EPS) * kvg_ref[...]
    ckvt_o[0] = ckv.T.astype(BF16)
    R = ckv.shape[1]
    pos = i * tm + lax.broadcasted_iota(I32, (tm, R), 0)
    lane = lax.broadcasted_iota(I32, (tm, R), 1)
    pos_hi = ((pos >> 8) << 8).astype(F32)
    pos_lo = (pos & 255).astype(F32)
    extra = jnp.where((lane == 0) | (lane == 2), pos_hi, jnp.where((lane == 1) | (lane == 3), pos_lo, 0.0))
    ckva_o[0, :, 0:R] = ckv.astype(BF16)
    ckva_o[0, :, R:2 * R] = extra.astype(BF16)

    qit_o[0] = (proj(wqi) * qi_scale).T.astype(BF16)
    ki_o[0] = proj(wki).astype(BF16)
    wit_o[0] = (proj(wwi) * wi_scale).T[0:Hi, :]
    u_o[0] = proj(wu)
    g_o[0] = proj(wg).astype(BF16)


def _inproj(x, mod, ws, kv_norm_g, wukt, *, q_scale, qi_scale, wi_scale, Hi, tm):
    B, S, D = x.shape
    wq, wckv, wqi, wki, wwi, wu, wg = ws
    H, R, dh = wukt.shape
    row = lambda n: pl.BlockSpec((1, tm, n), lambda b, i: (b, i, 0))
    col = lambda n: pl.BlockSpec((1, n, tm), lambda b, i: (b, 0, i))
    full = lambda a: pl.BlockSpec(a.shape, lambda b, i: (0,) * a.ndim)
    out_specs = [pl.BlockSpec((1, H, R, tm), lambda b, i: (b, 0, 0, i)), row(2 * R), col(R), col(wqi.shape[1]),
                 row(wki.shape[1]), col(Hi), row(wu.shape[1]), row(wg.shape[1])]
    out_shape = [jax.ShapeDtypeStruct((B, H, R, S), BF16), jax.ShapeDtypeStruct((B, S, 2 * R), BF16),
                 jax.ShapeDtypeStruct((B, R, S), BF16), jax.ShapeDtypeStruct((B, wqi.shape[1], S), BF16),
                 jax.ShapeDtypeStruct((B, S, wki.shape[1]), BF16), jax.ShapeDtypeStruct((B, Hi, S), F32),
                 jax.ShapeDtypeStruct((B, S, wu.shape[1]), F32), jax.ShapeDtypeStruct((B, S, wg.shape[1]), BF16)]
    return pl.pallas_call(
        functools.partial(_inproj_kernel, tm=tm, H=H, dh=dh, Hi=Hi, q_scale=q_scale, qi_scale=qi_scale,
                          wi_scale=wi_scale),
        grid=(B, S // tm),
        in_specs=[row(D), pl.BlockSpec((1, N_MOD, D), lambda b, i: (b, 0, 0))]
                 + [full(w) for w in ws] + [full(kv_norm_g), full(wukt)],
        out_specs=out_specs,
        out_shape=out_shape,
        compiler_params=pltpu.CompilerParams(
            dimension_semantics=("parallel", "parallel"), vmem_limit_bytes=48 * 2 ** 20),
    )(x, mod, *ws, kv_norm_g, wukt)


def _key_to_f32(t):
    return pltpu.bitcast(jnp.where(t >= 0, t, t ^ 0x7FFFFFFF), F32)


def _attn_body(qit_ref, wit_ref, ki_ref, qlt_ref, aug_ref, ckva_ref, ckvt_ref, wuvt_ref, yt_ref,
               score_scr, bias_scr, *, W, tq, S, H, dh, Hi, di, topk, cw):
    j = pl.program_id(1)
    qpos = j * tq + lax.broadcasted_iota(I32, (1, tq), 1)
    score = score_scr.at[0:W, :]
    bias = bias_scr.at[0:W, :]

    wit = wit_ref[0]
    for c in range(W // cw):
        kic = ki_ref[0, c * cw:(c + 1) * cw, :]
        acc = jnp.zeros((cw, tq), F32)
        for h in range(Hi):
            d = jnp.dot(kic, qit_ref[0, h * di:(h + 1) * di, :], preferred_element_type=F32)
            acc = acc + wit[h:h + 1, :] * jnp.maximum(d, 0.0)
        kpos_c = c * cw + lax.broadcasted_iota(I32, (cw, tq), 0)
        score_scr[c * cw:(c + 1) * cw, :] = jnp.where(kpos_c <= qpos, acc, -jnp.inf)

    def count(hit):
        return jnp.sum(jnp.where(hit, 1.0, 0.0), axis=0, keepdims=True)

    def thr_step(i, t):
        cand = t + lax.shift_left(jnp.int32(1), 31 - i)
        cnt = count(score[...] >= _key_to_f32(cand))
        return jnp.where(cnt >= float(topk), cand, t)

    t = lax.fori_loop(0, 32, thr_step, jnp.full((1, tq), INT_MIN, I32))
    few = t == INT_MIN
    thr_f = _key_to_f32(t)
    up_f = _key_to_f32(t + 1)
    s = score[...]
    n_gt = count(s >= up_f)
    n_ge = count(s >= thr_f)
    need = float(topk) - n_gt

    kpos = lax.broadcasted_iota(I32, (W, tq), 0)
    nbits = max(1, (S - 1).bit_length())

    def tie_search():
        def step(i, jm):
            cand = jm + lax.shift_right_logical(jnp.int32(2 ** (nbits - 1)), i)
            sc = score[...]
            f = count((sc >= thr_f) & jnp.logical_not(sc >= up_f) & (kpos < cand))
            return jnp.where(f < need, cand, jm)
        return lax.fori_loop(0, nbits, step, jnp.zeros((1, tq), I32))

    any_tie = jnp.max(jnp.where(few, 0.0, n_ge - n_gt - need)) > 0.0
    jmax = lax.cond(any_tie, tie_search, lambda: jnp.full((1, tq), S, I32))
    sel = (few | (s >= up_f) | ((s >= thr_f) & (kpos <= jmax))) & (kpos <= qpos)
    bias_scr[0:W, :] = jnp.where(sel, 0.0, NEG_BIG)

    def head(h, carry):
        qa = jnp.concatenate([qlt_ref[0, h], aug_ref[h]], axis=0)
        lg = jnp.dot(ckva_ref[0, 0:W, :], qa, preferred_element_type=F32) + bias[...]
        m = jnp.max(lg, axis=0, keepdims=True)
        p = jnp.exp(lg - m)
        l = jnp.sum(p, axis=0, keepdims=True)
        ot = jnp.dot(ckvt_ref[0, :, 0:W], p.astype(BF16), preferred_element_type=F32) / l
        yt = jnp.dot(wuvt_ref[h], ot.astype(BF16), preferred_element_type=F32)
        yt_ref[0, pl.ds(pl.multiple_of(h * dh, dh), dh), :] = yt.astype(BF16)
        return carry

    lax.fori_loop(0, H, head, 0)


def _attn_kernel(*refs, tq, cw, S, **kw):
    j = pl.program_id(1)
    nc = ((j + 1) * tq + cw - 1) // cw
    for w in range(1, S // cw + 1):
        @pl.when(nc == w)
        def _():
            _attn_body(*refs, W=w * cw, tq=tq, S=S, cw=cw, **kw)


def _attention(qit, wit, ki, qlt, ckva, ckvt, wuvt, *, tq, topk):
    B, H, R, S = qlt.shape
    dh = wuvt.shape[1]
    di = ki.shape[2]
    Hi = wit.shape[1]
    slopes = jnp.asarray([2.0 ** (-8.0 * (h + 1) / H) for h in range(H)], F32)
    s_hi = slopes.astype(BF16)
    s_lo = (slopes - s_hi.astype(F32)).astype(BF16)
    aug = jnp.zeros((H, R, tq), BF16)
    aug = aug.at[:, 0:2, :].set(s_hi[:, None, None]).at[:, 2:4, :].set(s_lo[:, None, None])
    kern = functools.partial(_attn_kernel, tq=tq, S=S, H=H, dh=dh, Hi=Hi, di=di, topk=topk, cw=min(S, 512))
    col = lambda n: pl.BlockSpec((1, n, tq), lambda b, j: (b, 0, j))
    full = lambda a: pl.BlockSpec(a.shape, lambda b, j: (0,) * a.ndim)
    return pl.pallas_call(
        kern,
        grid=(B, S // tq),
        in_specs=[col(qit.shape[1]), col(Hi), pl.BlockSpec((1, S, di), lambda b, j: (b, 0, 0)),
                  pl.BlockSpec((1, H, R, tq), lambda b, j: (b, 0, 0, j)), full(aug),
                  pl.BlockSpec((1, S, 2 * R), lambda b, j: (b, 0, 0)),
                  pl.BlockSpec((1, R, S), lambda b, j: (b, 0, 0)), full(wuvt)],
        out_specs=col(H * dh),
        out_shape=jax.ShapeDtypeStruct((B, H * dh, S), BF16),
        scratch_shapes=[pltpu.VMEM((S, tq), F32), pltpu.VMEM((S, tq), F32)],
        compiler_params=pltpu.CompilerParams(
            dimension_semantics=("parallel", "parallel"), vmem_limit_bytes=48 * 2 ** 20),
    )(qit, wit, ki, qlt, aug, ckva, ckvt, wuvt)


def _merge_kernel(x_ref, yat_ref, u_ref, halo_ref, g_ref, mod_ref, wpg_ref, psc_ref, wba_ref, wbp_ref,
                  wout_ref, ln1g_ref, ln1b_ref, wr_ref, br_ref,
                  x1_o, h2_o, te_o, tg_o, ext_scr, *, tm, D, G, alpha, E, K):
    i = pl.program_id(1)

    @pl.when(i == 0)
    def _():
        ext_scr[0:POOL_HALO, :] = jnp.zeros((POOL_HALO, ext_scr.shape[1]), F32)

    @pl.when(i > 0)
    def _():
        ext_scr[0:POOL_HALO, :] = halo_ref[0]

    ext_scr[POOL_HALO:POOL_HALO + tm, :] = u_ref[0]

    pos = i * tm + lax.broadcasted_iota(I32, (tm, 1), 0)
    ys = []
    for g, w in enumerate(POOL_WINDOWS):
        cols = slice(g * G, (g + 1) * G)
        cur = ext_scr[POOL_HALO:POOL_HALO + tm, cols]
        acc = cur
        for back in range(1, w):
            acc = acc + ext_scr[POOL_HALO - back:POOL_HALO - back + tm, cols]
        cnt = jnp.minimum(pos + 1, w).astype(F32)
        pooled = acc / cnt - cur
        ys.append(jnp.dot(pooled.astype(BF16), wpg_ref[g], preferred_element_type=F32))
    y_pool = jnp.concatenate(ys, axis=1) * psc_ref[...]

    gates = g_ref[0]
    ya = yat_ref[0].astype(F32).T.astype(BF16)
    a = jnp.dot(ya, wba_ref[...], preferred_element_type=F32)
    p = jnp.dot(y_pool.astype(BF16), wbp_ref[...], preferred_element_type=F32)
    mix = jax.nn.sigmoid(gates[:, :D].astype(F32)) * a + jax.nn.sigmoid(gates[:, D:].astype(F32)) * p
    o = jnp.dot(mix.astype(BF16), wout_ref[...], preferred_element_type=F32)
    mod = mod_ref[0]
    x1 = _layer_norm(alpha * x_ref[0] + mod[2:3] * o, ln1g_ref[...], ln1b_ref[...])
    x1_o[0] = x1
    h2 = x1 * (1.0 + mod[4:5]) + mod[3:4]
    h2_o[0] = h2

    logits = jnp.dot(h2, wr_ref[...], precision=HIGHEST, preferred_element_type=F32) + br_ref[...]
    lane = lax.broadcasted_iota(I32, logits.shape, 1)
    logits = jnp.where(lane < E, logits, -jnp.inf)
    te = jnp.zeros(logits.shape, I32)
    vals = []
    for k in range(K):
        m = jnp.max(logits, axis=1, keepdims=True)
        idx = jnp.min(jnp.where(logits == m, lane, LANES), axis=1, keepdims=True)
        te = jnp.where(lane == k, idx, te)
        vals.append(m)
        logits = jnp.where(lane == idx, -jnp.inf, logits)
    ex = [jnp.exp(v - vals[0]) for v in vals]
    den = ex[0]
    for e in ex[1:]:
        den = den + e
    tg = jnp.zeros(logits.shape, F32)
    for k in range(K):
        tg = jnp.where(lane == k, ex[k] / den, tg)
    te_o[0] = te
    tg_o[0] = tg


def _merge(x, yat, u, g, mod, wpg, psc, wba, wbp, wout, ln1g, ln1b, wr, br, *, tm, alpha, E, K):
    B, S, D = x.shape
    PW = u.shape[2]
    G = wpg.shape[1]
    kern = functools.partial(_merge_kernel, tm=tm, D=D, G=G, alpha=alpha, E=E, K=K)
    row = lambda n: pl.BlockSpec((1, tm, n), lambda b, i: (b, i, 0))
    full = lambda a: pl.BlockSpec(a.shape, lambda b, i: (0,) * a.ndim)
    hb = tm // POOL_HALO
    halo = pl.BlockSpec((1, POOL_HALO, PW), lambda b, i: (b, jnp.maximum(i * hb - 1, 0), 0))
    return pl.pallas_call(
        kern,
        grid=(B, S // tm),
        in_specs=[row(D), pl.BlockSpec((1, yat.shape[1], tm), lambda b, i: (b, 0, i)), row(PW), halo,
                  row(g.shape[2]), pl.BlockSpec((1, N_MOD, D), lambda b, i: (b, 0, 0)),
                  full(wpg), full(psc), full(wba), full(wbp), full(wout), full(ln1g), full(ln1b),
                  full(wr), full(br)],
        out_specs=[row(D), row(D), row(LANES), row(LANES)],
        out_shape=[jax.ShapeDtypeStruct((B, S, D), F32), jax.ShapeDtypeStruct((B, S, D), F32),
                   jax.ShapeDtypeStruct((B, S, LANES), I32), jax.ShapeDtypeStruct((B, S, LANES), F32)],
        scratch_shapes=[pltpu.VMEM((POOL_HALO + tm, PW), F32)],
        compiler_params=pltpu.CompilerParams(
            dimension_semantics=("parallel", "arbitrary"), vmem_limit_bytes=48 * 2 ** 20),
    )(x, yat, u, u, g, mod, wpg, psc, wba, wbp, wout, ln1g, ln1b, wr, br)


def _route_kernel(te_ref, rank_o, cnt_o, base_scr, *, tR, K):
    i = pl.program_id(0)

    @pl.when(i == 0)
    def _():
        base_scr[...] = jnp.zeros(base_scr.shape, F32)

    te = te_ref[...]
    lane = lax.broadcasted_iota(I32, te.shape, 1)
    hits = [lane == te[:, k:k + 1] for k in range(K)]
    onehot = hits[0].astype(F32)
    for k in range(1, K):
        onehot = onehot + hits[k].astype(F32)
    earlier = (lax.broadcasted_iota(I32, (tR, tR), 0) > lax.broadcasted_iota(I32, (tR, tR), 1)).astype(BF16)
    before = jnp.dot(earlier, onehot.astype(BF16), preferred_element_type=F32) + base_scr[...]
    rank = jnp.zeros(te.shape, I32)
    for k in range(K):
        rk = jnp.sum(jnp.where(hits[k], before, 0.0), axis=1, keepdims=True)
        rank = jnp.where(lane == k, rk.astype(I32), rank)
    rank_o[...] = rank
    base_scr[...] = base_scr[...] + jnp.sum(onehot, axis=0, keepdims=True)
    cnt_o[...] = base_scr[...].astype(I32)


def _route(te, *, tR, K):
    T = te.shape[0]
    return pl.pallas_call(
        functools.partial(_route_kernel, tR=tR, K=K),
        grid=(T // tR,),
        in_specs=[pl.BlockSpec((tR, LANES), lambda i: (i, 0))],
        out_specs=[pl.BlockSpec((tR, LANES), lambda i: (i, 0)), pl.BlockSpec((1, LANES), lambda i: (0, 0))],
        out_shape=[jax.ShapeDtypeStruct((T, LANES), I32), jax.ShapeDtypeStruct((1, LANES), I32)],
        scratch_shapes=[pltpu.VMEM((1, LANES), F32)],
        compiler_params=pltpu.CompilerParams(dimension_semantics=("arbitrary",)),
    )(te)


def _dispatch_kernel(pos_ref, h2_ref, xs_in, xs_out, sem, *, tD, K):
    del xs_in

    def issue(r, carry):
        for k in range(K):
            p = pos_ref[0, 0, k * tD + r]
            pltpu.make_async_copy(h2_ref.at[pl.ds(r, 1), :], xs_out.at[pl.ds(p, 1), :], sem).start()
        return carry

    lax.fori_loop(0, tD, issue, 0, unroll=8)
    for k in range(K):
        pltpu.make_async_copy(h2_ref, xs_out.at[pl.ds(0, tD), :], sem).wait()


def _dispatch(pos, h2, n_rows, *, tD, K):
    T, D = h2.shape
    xs0 = jnp.zeros((n_rows, D), h2.dtype)
    return pl.pallas_call(
        functools.partial(_dispatch_kernel, tD=tD, K=K),
        grid=(T // tD,),
        in_specs=[pl.BlockSpec((1, 1, K * tD), lambda i: (i, 0, 0), memory_space=pltpu.SMEM),
                  pl.BlockSpec((tD, D), lambda i: (i, 0)),
                  pl.BlockSpec(memory_space=pl.ANY)],
        out_specs=pl.BlockSpec(memory_space=pl.ANY),
        out_shape=jax.ShapeDtypeStruct((n_rows, D), h2.dtype),
        scratch_shapes=[pltpu.SemaphoreType.DMA(())],
        input_output_aliases={2: 0},
        compiler_params=pltpu.CompilerParams(
            dimension_semantics=("arbitrary",), has_side_effects=True),
    )(pos, h2, xs0)


def _expert_kernel(be_ref, nu_ref, x_ref, wgu_ref, bgu_ref, wd_ref, bd_ref, y_ref, wgu_bf, wd_bf, *, F):
    i = pl.program_id(0)
    new_expert = (i == 0) | (be_ref[i] != be_ref[jnp.maximum(i - 1, 0)])

    @pl.when(new_expert)
    def _():
        wgu_bf[...] = wgu_ref[0].astype(BF16)
        wd_bf[...] = wd_ref[0].astype(BF16)

    @pl.when(i < nu_ref[0])
    def _():
        x = x_ref[...].astype(BF16)
        gu = jnp.dot(x, wgu_bf[...], preferred_element_type=F32) + bgu_ref[0]
        glu = jnp.minimum(gu[:, :F], SWIGLU_LIMIT)
        lin = jnp.clip(gu[:, F:], -SWIGLU_LIMIT, SWIGLU_LIMIT)
        act = glu * jax.nn.sigmoid(SWIGLU_ALPHA * glu) * (lin + 1.0)
        y_ref[...] = jnp.dot(act.astype(BF16), wd_bf[...], preferred_element_type=F32) + bd_ref[0]

    @pl.when(i >= nu_ref[0])
    def _():
        y_ref[...] = jnp.zeros(y_ref.shape, F32)


def _experts(block_e, n_used, xs, wgu, bgu, wd, bd, *, tmE):
    n_rows, D = xs.shape
    E, _, F2 = wgu.shape
    F = F2 // 2
    n_blocks = n_rows // tmE
    grid_spec = pltpu.PrefetchScalarGridSpec(
        num_scalar_prefetch=2,
        grid=(n_blocks,),
        in_specs=[pl.BlockSpec((tmE, D), lambda i, be, nu: (i, 0)),
                  pl.BlockSpec((1, D, F2), lambda i, be, nu: (be[i], 0, 0)),
                  pl.BlockSpec((1, 1, F2), lambda i, be, nu: (be[i], 0, 0)),
                  pl.BlockSpec((1, F, D), lambda i, be, nu: (be[i], 0, 0)),
                  pl.BlockSpec((1, 1, D), lambda i, be, nu: (be[i], 0, 0))],
        out_specs=pl.BlockSpec((tmE, D), lambda i, be, nu: (i, 0)),
        scratch_shapes=[pltpu.VMEM((D, F2), BF16), pltpu.VMEM((F, D), BF16)],
    )
    return pl.pallas_call(
        functools.partial(_expert_kernel, F=F),
        grid_spec=grid_spec,
        out_shape=jax.ShapeDtypeStruct((n_rows, D), F32),
        compiler_params=pltpu.CompilerParams(
            dimension_semantics=("arbitrary",), vmem_limit_bytes=56 * 2 ** 20),
    )(block_e, n_used, xs, wgu, bgu, wd, bd)


def _combine_kernel(pos_ref, nxt_ref, y_hbm, x1_ref, tg_ref, mod_ref, g_ref, b_ref, o_ref, buf, sem,
                    *, tmC, K, alpha, nt):
    i = pl.program_id(0)
    slot = lax.rem(i, 2)

    def fetch(p_ref, s):
        def issue(r, carry):
            for k in range(K):
                p = p_ref[0, 0, k * tmC + r]
                pltpu.make_async_copy(y_hbm.at[pl.ds(p, 1), :], buf.at[s, k, pl.ds(r, 1), :], sem.at[s]).start()
            return carry

        lax.fori_loop(0, tmC, issue, 0, unroll=8)

    @pl.when(i == 0)
    def _():
        fetch(pos_ref, slot)

    @pl.when(i + 1 < nt)
    def _():
        fetch(nxt_ref, 1 - slot)

    for k in range(K):
        pltpu.make_async_copy(y_hbm.at[pl.ds(0, tmC), :], buf.at[slot, k], sem.at[slot]).wait()

    tg = tg_ref[...]
    y = tg[:, 0:1] * buf[slot, 0]
    for k in range(1, K):
        y = y + tg[:, k:k + 1] * buf[slot, k]
    mod = mod_ref[0]
    o_ref[...] = _layer_norm(alpha * x1_ref[...] + mod[5:6] * y, g_ref[...], b_ref[...])


def _combine(pos, y_sorted, x1, tg, mod, ln2g, ln2b, *, tmC, K, alpha, S):
    T, D = x1.shape
    nt = T // tmC
    per_seq = S // tmC
    return pl.pallas_call(
        functools.partial(_combine_kernel, tmC=tmC, K=K, alpha=alpha, nt=nt),
        grid=(nt,),
        in_specs=[pl.BlockSpec((1, 1, K * tmC), lambda i: (i, 0, 0), memory_space=pltpu.SMEM),
                  pl.BlockSpec((1, 1, K * tmC), lambda i: (jnp.minimum(i + 1, nt - 1), 0, 0),
                               memory_space=pltpu.SMEM),
                  pl.BlockSpec(memory_space=pl.ANY),
                  pl.BlockSpec((tmC, D), lambda i: (i, 0)),
                  pl.BlockSpec((tmC, LANES), lambda i: (i, 0)),
                  pl.BlockSpec((1, N_MOD, D), lambda i: (i // per_seq, 0, 0)),
                  pl.BlockSpec((1, D), lambda i: (0, 0)),
                  pl.BlockSpec((1, D), lambda i: (0, 0))],
        out_specs=pl.BlockSpec((tmC, D), lambda i: (i, 0)),
        out_shape=jax.ShapeDtypeStruct((T, D), F32),
        scratch_shapes=[pltpu.VMEM((2, K, tmC, D), F32), pltpu.SemaphoreType.DMA((2,))],
        compiler_params=pltpu.CompilerParams(
            dimension_semantics=("arbitrary",), vmem_limit_bytes=48 * 2 ** 20),
    )(pos, pos, y_sorted, x1, tg, mod, ln2g, ln2b)


def _routing_tables(top_e, rank, counts, E, tmE):
    T, K = top_e.shape
    padded = (counts + tmE - 1) // tmE * tmE
    pad_end = jnp.cumsum(padded)
    pad_start = pad_end - padded
    n_blocks = -(-(T * K) // tmE) + E
    block_start = jnp.arange(n_blocks, dtype=I32) * tmE
    block_e = jnp.minimum(jnp.sum((pad_end[None, :] <= block_start[:, None]).astype(I32), axis=1), E - 1)
    n_used = (pad_end[-1] // tmE).astype(I32).reshape(1)
    pos = (jnp.take(pad_start, top_e) + rank).astype(I32)
    return block_e.astype(I32), n_used, pos, n_blocks


def _tile_major(pos, tile):
    T, K = pos.shape
    return pos.reshape(T // tile, tile, K).transpose(0, 2, 1).reshape(T // tile, 1, K * tile)


def _layer(x, mod, w_in, kv_norm_g, w_uk, w_uv, w_pool_group, pool_scale, w_branch_attn, w_branch_pool,
           w_out, ln1_g, ln1_b, w_router, b_router, w_gate_up, b_gate_up, w_down, b_down, ln2_g, ln2_b,
           *, alpha):
    B, S, D = x.shape
    R, H, dh = w_uk.shape
    AW = H * dh
    PW = pool_scale.shape[0]
    E = w_router.shape[1]
    K = TOP_K_EXPERTS
    n_idx = w_in.shape[1] - (AW + R + PW + 2 * D)
    di = dh
    Hi = (n_idx - di) // (di + 1)
    sizes = (AW, R, Hi * di, di, Hi, PW, 2 * D)
    offs = [0]
    for s in sizes:
        offs.append(offs[-1] + s)
    ws = [w_in[:, offs[k]:offs[k + 1]].astype(BF16) for k in range(len(sizes))]
    ws[4] = jnp.pad(ws[4], ((0, 0), (0, LANES - Hi)))

    wukt = w_uk.transpose(1, 0, 2).astype(BF16)
    wuvt = w_uv.transpose(1, 2, 0).astype(BF16)
    qlt, ckva, ckvt, qit, ki, wit, u, g = _inproj(
        x, mod, ws, kv_norm_g.reshape(1, R), wukt, q_scale=dh ** -0.5, qi_scale=di ** -0.5,
        wi_scale=Hi ** -0.5, Hi=Hi, tm=min(S, 512))

    topk = min(TOPK_MAX, S // 4)
    y_attn_t = _attention(qit, wit, ki, qlt, ckva, ckvt, wuvt, tq=min(S, 256), topk=topk)

    wr = jnp.pad(w_router, ((0, 0), (0, LANES - E)))
    br = jnp.pad(b_router, (0, LANES - E)).reshape(1, LANES)
    x1, h2, te, tg = _merge(x, y_attn_t, u, g, mod, w_pool_group.astype(BF16), pool_scale.reshape(1, PW),
                            w_branch_attn.astype(BF16), w_branch_pool.astype(BF16), w_out.astype(BF16),
                            ln1_g.reshape(1, D), ln1_b.reshape(1, D), wr, br,
                            tm=min(S, 256), alpha=alpha, E=E, K=K)

    T = B * S
    tmE = 256
    tmC = min(S, 256)
    tD = min(T, 1024)
    te = te.reshape(T, LANES)
    rank, counts = _route(te, tR=min(T, 512), K=K)
    block_e, n_used, pos, n_blocks = _routing_tables(te[:, :K], rank[:, :K], counts[0, :E], E, tmE)
    xs = _dispatch(_tile_major(pos, tD), h2.reshape(T, D), n_blocks * tmE, tD=tD, K=K)
    y_sorted = _experts(block_e, n_used, xs, w_gate_up, b_gate_up.reshape(E, 1, -1),
                        w_down, b_down.reshape(E, 1, D), tmE=tmE)
    out = _combine(_tile_major(pos, tmC), y_sorted, x1.reshape(T, D), tg.reshape(T, LANES), mod,
                   ln2_g.reshape(1, D), ln2_b.reshape(1, D), tmC=tmC, K=K, alpha=alpha, S=S)
    return out.reshape(B, S, D)


def kernel(x, c, w_ada, b_ada, w_in, kv_norm_g, w_uk, w_uv, w_pool_group, pool_scale, w_branch_attn,
           w_branch_pool, w_out, ln1_g, ln1_b, w_router, b_router, w_gate_up, b_gate_up, w_down, b_down,
           ln2_g, ln2_b):
    B, S, D = x.shape
    depth = w_ada.shape[0]
    alpha = (2.0 * depth) ** 0.25
    for l in range(depth):
        mod = _ada(c, w_ada[l], b_ada[l]).reshape(B, N_MOD, D)
        x = _layer(x, mod, w_in[l], kv_norm_g[l], w_uk[l], w_uv[l], w_pool_group[l], pool_scale[l],
                   w_branch_attn[l], w_branch_pool[l], w_out[l], ln1_g[l], ln1_b[l], w_router[l],
                   b_router[l], w_gate_up[l], b_gate_up[l], w_down[l], b_down[l], ln2_g[l], ln2_b[l],
                   alpha=alpha)
    return x
```

```python
import functools

import jax
import jax.numpy as jnp
from jax import lax
from jax.experimental import pallas as pl
from jax.experimental.pallas import tpu as pltpu

F32 = jnp.float32
BF16 = jnp.bfloat16
I32 = jnp.int32
HIGHEST = lax.Precision.HIGHEST

LN_EPS = 1e-5
TOPK_MAX = 256
TOP_K_EXPERTS = 4
POOL_WINDOWS = (2, 4, 8, 16)
SWIGLU_LIMIT = 7.0
SWIGLU_ALPHA = 1.702
N_MOD = 6
INT_MIN = -(2 ** 31)
NEG_BIG = -1e30
LANES = 128
POOL_HALO = 16
ONES_ROWS = 16
LOG2E = 1.4426950408889634
ATTN_KEY_CHUNK = 256
ATTN_HEAD_UNROLL = 4


def _layer_norm(z, g, b):
    mu = jnp.mean(z, axis=-1, keepdims=True)
    zc = z - mu
    var = jnp.mean(zc * zc, axis=-1, keepdims=True)
    return zc * lax.rsqrt(var + LN_EPS) * g + b


def _ada_kernel(c_ref, w_ref, b_ref, o_ref):
    c = c_ref[...]
    cond = c * jax.nn.sigmoid(c)
    o_ref[...] = jnp.dot(cond, w_ref[...], precision=HIGHEST, preferred_element_type=F32) + b_ref[...]


def _ada(c, w_ada, b_ada):
    B, D = c.shape
    N = w_ada.shape[1]
    tn = D
    return pl.pallas_call(
        _ada_kernel,
        grid=(N // tn,),
        in_specs=[pl.BlockSpec((B, D), lambda j: (0, 0)),
                  pl.BlockSpec((D, tn), lambda j: (0, j)),
                  pl.BlockSpec((1, tn), lambda j: (0, j))],
        out_specs=pl.BlockSpec((B, tn), lambda j: (0, j)),
        out_shape=jax.ShapeDtypeStruct((B, N), F32),
    )(c, w_ada, b_ada.reshape(1, N))


def _inproj_kernel(x_ref, mod_ref, wq, wckv, wqi, wki, wwi, wu, wg, kvg_ref, wukt_ref,
                   qlt_o, ckva_o, ckvt_o, qit_o, ki_o, wit_o, u_o, g_o,
                   *, tm, H, dh, Hi, q_scale, qi_scale, wi_scale):
    i = pl.program_id(1)
    x = x_ref[0]
    mod = mod_ref[0]
    h = (x * (1.0 + mod[1:2]) + mod[0:1]).astype(BF16)

    def proj(w):
        return jnp.dot(h, w[...], preferred_element_type=F32)

    qt = proj(wq).T.astype(BF16)
    for hh in range(H):
        qlt = jnp.dot(wukt_ref[hh], qt[hh * dh:(hh + 1) * dh, :], preferred_element_type=F32) * q_scale
        qlt_o[0, hh] = qlt.astype(BF16)

    ckv = proj(wckv)
    ckv = ckv * lax.rsqrt(jnp.mean(ckv * ckv, axis=-1, keepdims=True) + LN_EPS) * kvg_ref[...]
    ckvt_o[0, 0:ckv.shape[1], :] = ckv.T.astype(BF16)
    ckvt_o[0, ckv.shape[1]:, :] = jnp.ones((ONES_ROWS, tm), BF16)
    R = ckv.shape[1]
    pos = i * tm + lax.broadcasted_iota(I32, (tm, R), 0)
    lane = lax.broadcasted_iota(I32, (tm, R), 1)
    pos_hi = ((pos >> 8) << 8).astype(F32)
    pos_lo = (pos & 255).astype(F32)
    extra = jnp.where((lane == 0) | (lane == 2), pos_hi, jnp.where((lane == 1) | (lane == 3), pos_lo, 0.0))
    ckva_o[0, :, 0:R] = ckv.astype(BF16)
    ckva_o[0, :, R:2 * R] = extra.astype(BF16)

    qit_o[0] = (proj(wqi) * qi_scale).T.astype(BF16)
    ki_o[0] = proj(wki).astype(BF16)
    wit_o[0] = (proj(wwi) * wi_scale).T[0:Hi, :]
    u_o[0] = proj(wu)
    g_o[0] = proj(wg).astype(BF16)


def _inproj(x, mod, ws, kv_norm_g, wukt, *, q_scale, qi_scale, wi_scale, Hi, tm):
    B, S, D = x.shape
    wq, wckv, wqi, wki, wwi, wu, wg = ws
    H, R, dh = wukt.shape
    row = lambda n: pl.BlockSpec((1, tm, n), lambda b, i: (b, i, 0))
    col = lambda n: pl.BlockSpec((1, n, tm), lambda b, i: (b, 0, i))
    full = lambda a: pl.BlockSpec(a.shape, lambda b, i: (0,) * a.ndim)
    out_specs = [pl.BlockSpec((1, H, R, tm), lambda b, i: (b, 0, 0, i)), row(2 * R), col(R + ONES_ROWS),
                 col(wqi.shape[1]),
                 row(wki.shape[1]), col(Hi), row(wu.shape[1]), row(wg.shape[1])]
    out_shape = [jax.ShapeDtypeStruct((B, H, R, S), BF16), jax.ShapeDtypeStruct((B, S, 2 * R), BF16),
                 jax.ShapeDtypeStruct((B, R + ONES_ROWS, S), BF16),
                 jax.ShapeDtypeStruct((B, wqi.shape[1], S), BF16),
                 jax.ShapeDtypeStruct((B, S, wki.shape[1]), BF16), jax.ShapeDtypeStruct((B, Hi, S), F32),
                 jax.ShapeDtypeStruct((B, S, wu.shape[1]), F32), jax.ShapeDtypeStruct((B, S, wg.shape[1]), BF16)]
    return pl.pallas_call(
        functools.partial(_inproj_kernel, tm=tm, H=H, dh=dh, Hi=Hi, q_scale=q_scale, qi_scale=qi_scale,
                          wi_scale=wi_scale),
        grid=(B, S // tm),
        in_specs=[row(D), pl.BlockSpec((1, N_MOD, D), lambda b, i: (b, 0, 0))]
                 + [full(w) for w in ws] + [full(kv_norm_g), full(wukt)],
        out_specs=out_specs,
        out_shape=out_shape,
        compiler_params=pltpu.CompilerParams(
            dimension_semantics=("parallel", "parallel"), vmem_limit_bytes=48 * 2 ** 20),
    )(x, mod, *ws, kv_norm_g, wukt)


def _key_to_f32(t):
    return pltpu.bitcast(jnp.where(t >= 0, t, t ^ 0x7FFFFFFF), F32)


def _attn_body(qit_ref, wit_ref, ki_ref, qlt_ref, aug_ref, ckva_ref, ckvt_ref, wuvt_ref, yt_ref,
               score_scr, bias_scr, lg_scr, o_scr, *, W, tq, S, H, dh, Hi, di, topk, cw):
    j = pl.program_id(1)
    qpos = j * tq + lax.broadcasted_iota(I32, (1, tq), 1)
    score = score_scr.at[0:W, :]

    wit = wit_ref[0]
    for c in range(W // cw):
        kic = ki_ref[0, c * cw:(c + 1) * cw, :]
        acc = jnp.zeros((cw, tq), F32)
        for h in range(Hi):
            d = jnp.dot(kic, qit_ref[0, h * di:(h + 1) * di, :], preferred_element_type=F32)
            acc = acc + wit[h:h + 1, :] * jnp.maximum(d, 0.0)
        kpos_c = c * cw + lax.broadcasted_iota(I32, (cw, tq), 0)
        score_scr[c * cw:(c + 1) * cw, :] = jnp.where(kpos_c <= qpos, acc, -jnp.inf)

    def count(hit):
        return jnp.sum(jnp.where(hit, 1.0, 0.0), axis=0, keepdims=True)

    def thr_step(i, t):
        cand = t + lax.shift_left(jnp.int32(1), 31 - i)
        cnt = count(score[...] >= _key_to_f32(cand))
        return jnp.where(cnt >= float(topk), cand, t)

    t = lax.fori_loop(0, 32, thr_step, jnp.full((1, tq), INT_MIN, I32))
    few = t == INT_MIN
    thr_f = _key_to_f32(t)
    up_f = _key_to_f32(t + 1)
    s = score[...]
    n_gt = count(s >= up_f)
    n_ge = count(s >= thr_f)
    need = float(topk) - n_gt

    kpos = lax.broadcasted_iota(I32, (W, tq), 0)
    nbits = max(1, (S - 1).bit_length())

    def tie_search():
        def step(i, jm):
            cand = jm + lax.shift_right_logical(jnp.int32(2 ** (nbits - 1)), i)
            sc = score[...]
            f = count((sc >= thr_f) & jnp.logical_not(sc >= up_f) & (kpos < cand))
            return jnp.where(f < need, cand, jm)
        return lax.fori_loop(0, nbits, step, jnp.zeros((1, tq), I32))

    any_tie = jnp.max(jnp.where(few, 0.0, n_ge - n_gt - need)) > 0.0
    jmax = lax.cond(any_tie, tie_search, lambda: jnp.full((1, tq), S, I32))
    sel = (few | (s >= up_f) | ((s >= thr_f) & (kpos <= jmax))) & (kpos <= qpos)
    bias_scr[0:W, :] = jnp.where(sel, 0.0, NEG_BIG)

    R = qlt_ref.shape[2]
    ck = min(ATTN_KEY_CHUNK, W)

    def head(h, carry):
        qa = jnp.concatenate([qlt_ref[0, h], aug_ref[h]], axis=0)
        m = None
        for c in range(W // ck):
            keys = slice(c * ck, (c + 1) * ck)
            lg = jnp.dot(ckva_ref[0, keys, :], qa, preferred_element_type=F32) + bias_scr[keys, :]
            lg_scr[keys, :] = lg
            mc = jnp.max(lg, axis=0, keepdims=True)
            m = mc if m is None else jnp.maximum(m, mc)
        acc = jnp.zeros((R + ONES_ROWS, tq), F32)
        for c in range(W // ck):
            keys = slice(c * ck, (c + 1) * ck)
            p = jnp.exp2(lg_scr[keys, :] - m)
            acc = acc + jnp.dot(ckvt_ref[0, :, keys], p.astype(BF16), preferred_element_type=F32)
        o_scr[h] = (acc[0:R] / acc[R:R + 1]).astype(BF16)
        return carry

    lax.fori_loop(0, H, head, 0, unroll=ATTN_HEAD_UNROLL)
    for h in range(H):
        yt = jnp.dot(wuvt_ref[h], o_scr[h], preferred_element_type=F32)
        yt_ref[0, h * dh:(h + 1) * dh, :] = yt.astype(BF16)


def _attn_kernel(*refs, tq, cw, S, **kw):
    j = pl.program_id(1)
    nc = ((j + 1) * tq + cw - 1) // cw
    for w in range(1, S // cw + 1):
        @pl.when(nc == w)
        def _():
            _attn_body(*refs, W=w * cw, tq=tq, S=S, cw=cw, **kw)


def _attention(qit, wit, ki, qlt, ckva, ckvt, wuvt, *, tq, topk):
    B, H, R, S = qlt.shape
    dh = wuvt.shape[1]
    di = ki.shape[2]
    Hi = wit.shape[1]
    slopes = jnp.asarray([LOG2E * 2.0 ** (-8.0 * (h + 1) / H) for h in range(H)], F32)
    s_hi = slopes.astype(BF16)
    s_lo = (slopes - s_hi.astype(F32)).astype(BF16)
    aug = jnp.zeros((H, R, tq), BF16)
    aug = aug.at[:, 0:2, :].set(s_hi[:, None, None]).at[:, 2:4, :].set(s_lo[:, None, None])
    kern = functools.partial(_attn_kernel, tq=tq, S=S, H=H, dh=dh, Hi=Hi, di=di, topk=topk, cw=min(S, 512))
    col = lambda n: pl.BlockSpec((1, n, tq), lambda b, j: (b, 0, j))
    full = lambda a: pl.BlockSpec(a.shape, lambda b, j: (0,) * a.ndim)
    return pl.pallas_call(
        kern,
        grid=(B, S // tq),
        in_specs=[col(qit.shape[1]), col(Hi), pl.BlockSpec((1, S, di), lambda b, j: (b, 0, 0)),
                  pl.BlockSpec((1, H, R, tq), lambda b, j: (b, 0, 0, j)), full(aug),
                  pl.BlockSpec((1, S, 2 * R), lambda b, j: (b, 0, 0)),
                  pl.BlockSpec((1, R + ONES_ROWS, S), lambda b, j: (b, 0, 0)), full(wuvt)],
        out_specs=col(H * dh),
        out_shape=jax.ShapeDtypeStruct((B, H * dh, S), BF16),
        scratch_shapes=[pltpu.VMEM((S, tq), F32), pltpu.VMEM((S, tq), F32), pltpu.VMEM((S, tq), F32),
                        pltpu.VMEM((H, R, tq), BF16)],
        compiler_params=pltpu.CompilerParams(
            dimension_semantics=("parallel", "parallel"), vmem_limit_bytes=48 * 2 ** 20),
    )(qit, wit, ki, qlt, aug, ckva, ckvt, wuvt)


def _merge_kernel(x_ref, yat_ref, u_ref, halo_ref, g_ref, mod_ref, wpg_ref, psc_ref, wba_ref, wbp_ref,
                  wout_ref, ln1g_ref, ln1b_ref, wr_ref, br_ref,
                  x1_o, h2_o, te_o, tg_o, ext_scr, *, tm, D, G, alpha, E, K):
    i = pl.program_id(1)

    @pl.when(i == 0)
    def _():
        ext_scr[0:POOL_HALO, :] = jnp.zeros((POOL_HALO, ext_scr.shape[1]), F32)

    @pl.when(i > 0)
    def _():
        ext_scr[0:POOL_HALO, :] = halo_ref[0]

    ext_scr[POOL_HALO:POOL_HALO + tm, :] = u_ref[0]

    pos = i * tm + lax.broadcasted_iota(I32, (tm, 1), 0)
    ys = []
    for g, w in enumerate(POOL_WINDOWS):
        cols = slice(g * G, (g + 1) * G)
        cur = ext_scr[POOL_HALO:POOL_HALO + tm, cols]
        acc = cur
        for back in range(1, w):
            acc = acc + ext_scr[POOL_HALO - back:POOL_HALO - back + tm, cols]
        cnt = jnp.minimum(pos + 1, w).astype(F32)
        pooled = acc / cnt - cur
        ys.append(jnp.dot(pooled.astype(BF16), wpg_ref[g], preferred_element_type=F32))
    y_pool = jnp.concatenate(ys, axis=1) * psc_ref[...]

    gates = g_ref[0]
    ya = yat_ref[0].astype(F32).T.astype(BF16)
    a = jnp.dot(ya, wba_ref[...], preferred_element_type=F32)
    p = jnp.dot(y_pool.astype(BF16), wbp_ref[...], preferred_element_type=F32)
    mix = jax.nn.sigmoid(gates[:, :D].astype(F32)) * a + jax.nn.sigmoid(gates[:, D:].astype(F32)) * p
    o = jnp.dot(mix.astype(BF16), wout_ref[...], preferred_element_type=F32)
    mod = mod_ref[0]
    x1 = _layer_norm(alpha * x_ref[0] + mod[2:3] * o, ln1g_ref[...], ln1b_ref[...])
    x1_o[0] = x1
    h2 = x1 * (1.0 + mod[4:5]) + mod[3:4]
    h2_o[0] = h2

    logits = jnp.dot(h2, wr_ref[...], precision=HIGHEST, preferred_element_type=F32) + br_ref[...]
    lane = lax.broadcasted_iota(I32, logits.shape, 1)
    logits = jnp.where(lane < E, logits, -jnp.inf)
    te = jnp.zeros(logits.shape, I32)
    vals = []
    for k in range(K):
        m = jnp.max(logits, axis=1, keepdims=True)
        idx = jnp.min(jnp.where(logits == m, lane, LANES), axis=1, keepdims=True)
        te = jnp.where(lane == k, idx, te)
        vals.append(m)
        logits = jnp.where(lane == idx, -jnp.inf, logits)
    ex = [jnp.exp(v - vals[0]) for v in vals]
    den = ex[0]
    for e in ex[1:]:
        den = den + e
    tg = jnp.zeros(logits.shape, F32)
    for k in range(K):
        tg = jnp.where(lane == k, ex[k] / den, tg)
    te_o[0] = te
    tg_o[0] = tg


def _merge(x, yat, u, g, mod, wpg, psc, wba, wbp, wout, ln1g, ln1b, wr, br, *, tm, alpha, E, K):
    B, S, D = x.shape
    PW = u.shape[2]
    G = wpg.shape[1]
    kern = functools.partial(_merge_kernel, tm=tm, D=D, G=G, alpha=alpha, E=E, K=K)
    row = lambda n: pl.BlockSpec((1, tm, n), lambda b, i: (b, i, 0))
    full = lambda a: pl.BlockSpec(a.shape, lambda b, i: (0,) * a.ndim)
    hb = tm // POOL_HALO
    halo = pl.BlockSpec((1, POOL_HALO, PW), lambda b, i: (b, jnp.maximum(i * hb - 1, 0), 0))
    return pl.pallas_call(
        kern,
        grid=(B, S // tm),
        in_specs=[row(D), pl.BlockSpec((1, yat.shape[1], tm), lambda b, i: (b, 0, i)), row(PW), halo,
                  row(g.shape[2]), pl.BlockSpec((1, N_MOD, D), lambda b, i: (b, 0, 0)),
                  full(wpg), full(psc), full(wba), full(wbp), full(wout), full(ln1g), full(ln1b),
                  full(wr), full(br)],
        out_specs=[row(D), row(D), row(LANES), row(LANES)],
        out_shape=[jax.ShapeDtypeStruct((B, S, D), F32), jax.ShapeDtypeStruct((B, S, D), F32),
                   jax.ShapeDtypeStruct((B, S, LANES), I32), jax.ShapeDtypeStruct((B, S, LANES), F32)],
        scratch_shapes=[pltpu.VMEM((POOL_HALO + tm, PW), F32)],
        compiler_params=pltpu.CompilerParams(
            dimension_semantics=("parallel", "arbitrary"), vmem_limit_bytes=48 * 2 ** 20),
    )(x, yat, u, u, g, mod, wpg, psc, wba, wbp, wout, ln1g, ln1b, wr, br)


def _route_kernel(te_ref, rank_o, cnt_o, base_scr, *, tR, K):
    i = pl.program_id(0)

    @pl.when(i == 0)
    def _():
        base_scr[...] = jnp.zeros(base_scr.shape, F32)

    te = te_ref[...]
    lane = lax.broadcasted_iota(I32, te.shape, 1)
    hits = [lane == te[:, k:k + 1] for k in range(K)]
    onehot = hits[0].astype(F32)
    for k in range(1, K):
        onehot = onehot + hits[k].astype(F32)
    earlier = (lax.broadcasted_iota(I32, (tR, tR), 0) > lax.broadcasted_iota(I32, (tR, tR), 1)).astype(BF16)
    before = jnp.dot(earlier, onehot.astype(BF16), preferred_element_type=F32) + base_scr[...]
    rank = jnp.zeros(te.shape, I32)
    for k in range(K):
        rk = jnp.sum(jnp.where(hits[k], before, 0.0), axis=1, keepdims=True)
        rank = jnp.where(lane == k, rk.astype(I32), rank)
    rank_o[...] = rank
    base_scr[...] = base_scr[...] + jnp.sum(onehot, axis=0, keepdims=True)
    cnt_o[...] = base_scr[...].astype(I32)


def _route(te, *, tR, K):
    T = te.shape[0]
    return pl.pallas_call(
        functools.partial(_route_kernel, tR=tR, K=K),
        grid=(T // tR,),
        in_specs=[pl.BlockSpec((tR, LANES), lambda i: (i, 0))],
        out_specs=[pl.BlockSpec((tR, LANES), lambda i: (i, 0)), pl.BlockSpec((1, LANES), lambda i: (0, 0))],
        out_shape=[jax.ShapeDtypeStruct((T, LANES), I32), jax.ShapeDtypeStruct((1, LANES), I32)],
        scratch_shapes=[pltpu.VMEM((1, LANES), F32)],
        compiler_params=pltpu.CompilerParams(dimension_semantics=("arbitrary",)),
    )(te)


def _dispatch_kernel(pos_ref, h2_ref, xs_in, xs_out, sem, *, tD, K):
    del xs_in

    def issue(r, carry):
        for k in range(K):
            p = pos_ref[0, 0, k * tD + r]
            pltpu.async_copy(h2_ref.at[pl.ds(r, 1), :], xs_out.at[pl.ds(p, 1), :], sem, priority=k % 2)
        return carry

    lax.fori_loop(0, tD, issue, 0, unroll=8)
    for k in range(K):
        pltpu.make_async_copy(h2_ref, xs_out.at[pl.ds(0, tD), :], sem).wait()


def _dispatch(pos, h2, n_rows, *, tD, K):
    T, D = h2.shape
    xs0 = jnp.zeros((n_rows, D), h2.dtype)
    return pl.pallas_call(
        functools.partial(_dispatch_kernel, tD=tD, K=K),
        grid=(T // tD,),
        in_specs=[pl.BlockSpec((1, 1, K * tD), lambda i: (i, 0, 0), memory_space=pltpu.SMEM),
                  pl.BlockSpec((tD, D), lambda i: (i, 0)),
                  pl.BlockSpec(memory_space=pl.ANY)],
        out_specs=pl.BlockSpec(memory_space=pl.ANY),
        out_shape=jax.ShapeDtypeStruct((n_rows, D), h2.dtype),
        scratch_shapes=[pltpu.SemaphoreType.DMA(())],
        input_output_aliases={2: 0},
        compiler_params=pltpu.CompilerParams(
            dimension_semantics=("arbitrary",), has_side_effects=True),
    )(pos, h2, xs0)


def _expert_kernel(be_ref, nu_ref, x_ref, wgu_ref, bgu_ref, wd_ref, bd_ref, y_ref, wgu_bf, wd_bf, *, F):
    i = pl.program_id(0)
    new_expert = (i == 0) | (be_ref[i] != be_ref[jnp.maximum(i - 1, 0)])

    @pl.when(new_expert)
    def _():
        wgu_bf[...] = wgu_ref[0].astype(BF16)
        wd_bf[...] = wd_ref[0].astype(BF16)

    @pl.when(i < nu_ref[0])
    def _():
        x = x_ref[...].astype(BF16)
        gu = jnp.dot(x, wgu_bf[...], preferred_element_type=F32) + bgu_ref[0]
        glu = jnp.minimum(gu[:, :F], SWIGLU_LIMIT)
        lin = jnp.clip(gu[:, F:], -SWIGLU_LIMIT, SWIGLU_LIMIT)
        act = glu * jax.nn.sigmoid(SWIGLU_ALPHA * glu) * (lin + 1.0)
        y_ref[...] = jnp.dot(act.astype(BF16), wd_bf[...], preferred_element_type=F32) + bd_ref[0]

    @pl.when(i >= nu_ref[0])
    def _():
        y_ref[...] = jnp.zeros(y_ref.shape, F32)


def _experts(block_e, n_used, xs, wgu, bgu, wd, bd, *, tmE):
    n_rows, D = xs.shape
    E, _, F2 = wgu.shape
    F = F2 // 2
    n_blocks = n_rows // tmE
    grid_spec = pltpu.PrefetchScalarGridSpec(
        num_scalar_prefetch=2,
        grid=(n_blocks,),
        in_specs=[pl.BlockSpec((tmE, D), lambda i, be, nu: (i, 0)),
                  pl.BlockSpec((1, D, F2), lambda i, be, nu: (be[i], 0, 0)),
                  pl.BlockSpec((1, 1, F2), lambda i, be, nu: (be[i], 0, 0)),
                  pl.BlockSpec((1, F, D), lambda i, be, nu: (be[i], 0, 0)),
                  pl.BlockSpec((1, 1, D), lambda i, be, nu: (be[i], 0, 0))],
        out_specs=pl.BlockSpec((tmE, D), lambda i, be, nu: (i, 0)),
        scratch_shapes=[pltpu.VMEM((D, F2), BF16), pltpu.VMEM((F, D), BF16)],
    )
    return pl.pallas_call(
        functools.partial(_expert_kernel, F=F),
        grid_spec=grid_spec,
        out_shape=jax.ShapeDtypeStruct((n_rows, D), F32),
        compiler_params=pltpu.CompilerParams(
            dimension_semantics=("arbitrary",), vmem_limit_bytes=56 * 2 ** 20),
    )(block_e, n_used, xs, wgu, bgu, wd, bd)


def _combine_kernel(pos_ref, nxt_ref, y_hbm, x1_ref, tg_ref, mod_ref, g_ref, b_ref, o_ref, buf, sem,
                    *, tmC, K, alpha, nt):
    i = pl.program_id(0)
    slot = lax.rem(i, 2)

    def fetch(p_ref, s):
        def issue(r, carry):
            for k in range(K):
                p = p_ref[0, 0, k * tmC + r]
                pltpu.async_copy(y_hbm.at[pl.ds(p, 1), :], buf.at[s, k, pl.ds(r, 1), :], sem.at[s],
                                 priority=k % 2)
            return carry

        lax.fori_loop(0, tmC, issue, 0, unroll=8)

    @pl.when(i == 0)
    def _():
        fetch(pos_ref, slot)

    @pl.when(i + 1 < nt)
    def _():
        fetch(nxt_ref, 1 - slot)

    for k in range(K):
        pltpu.make_async_copy(y_hbm.at[pl.ds(0, tmC), :], buf.at[slot, k], sem.at[slot]).wait()

    tg = tg_ref[...]
    y = tg[:, 0:1] * buf[slot, 0]
    for k in range(1, K):
        y = y + tg[:, k:k + 1] * buf[slot, k]
    mod = mod_ref[0]
    o_ref[...] = _layer_norm(alpha * x1_ref[...] + mod[5:6] * y, g_ref[...], b_ref[...])


def _combine(pos, y_sorted, x1, tg, mod, ln2g, ln2b, *, tmC, K, alpha, S):
    T, D = x1.shape
    nt = T // tmC
    per_seq = S // tmC
    return pl.pallas_call(
        functools.partial(_combine_kernel, tmC=tmC, K=K, alpha=alpha, nt=nt),
        grid=(nt,),
        in_specs=[pl.BlockSpec((1, 1, K * tmC), lambda i: (i, 0, 0), memory_space=pltpu.SMEM),
                  pl.BlockSpec((1, 1, K * tmC), lambda i: (jnp.minimum(i + 1, nt - 1), 0, 0),
                               memory_space=pltpu.SMEM),
                  pl.BlockSpec(memory_space=pl.ANY),
                  pl.BlockSpec((tmC, D), lambda i: (i, 0)),
                  pl.BlockSpec((tmC, LANES), lambda i: (i, 0)),
                  pl.BlockSpec((1, N_MOD, D), lambda i: (i // per_seq, 0, 0)),
                  pl.BlockSpec((1, D), lambda i: (0, 0)),
                  pl.BlockSpec((1, D), lambda i: (0, 0))],
        out_specs=pl.BlockSpec((tmC, D), lambda i: (i, 0)),
        out_shape=jax.ShapeDtypeStruct((T, D), F32),
        scratch_shapes=[pltpu.VMEM((2, K, tmC, D), F32), pltpu.SemaphoreType.DMA((2,))],
        compiler_params=pltpu.CompilerParams(
            dimension_semantics=("arbitrary",), vmem_limit_bytes=48 * 2 ** 20),
    )(pos, pos, y_sorted, x1, tg, mod, ln2g, ln2b)


def _routing_tables(top_e, rank, counts, E, tmE):
    T, K = top_e.shape
    padded = (counts + tmE - 1) // tmE * tmE
    pad_end = jnp.cumsum(padded)
    pad_start = pad_end - padded
    n_blocks = -(-(T * K) // tmE) + E
    block_start = jnp.arange(n_blocks, dtype=I32) * tmE
    block_e = jnp.minimum(jnp.sum((pad_end[None, :] <= block_start[:, None]).astype(I32), axis=1), E - 1)
    n_used = (pad_end[-1] // tmE).astype(I32).reshape(1)
    pos = (jnp.take(pad_start, top_e) + rank).astype(I32)
    return block_e.astype(I32), n_used, pos, n_blocks


def _tile_major(pos, tile):
    T, K = pos.shape
    return pos.reshape(T // tile, tile, K).transpose(0, 2, 1).reshape(T // tile, 1, K * tile)


def _layer(x, mod, w_in, kv_norm_g, w_uk, w_uv, w_pool_group, pool_scale, w_branch_attn, w_branch_pool,
           w_out, ln1_g, ln1_b, w_router, b_router, w_gate_up, b_gate_up, w_down, b_down, ln2_g, ln2_b,
           *, alpha):
    B, S, D = x.shape
    R, H, dh = w_uk.shape
    AW = H * dh
    PW = pool_scale.shape[0]
    E = w_router.shape[1]
    K = TOP_K_EXPERTS
    n_idx = w_in.shape[1] - (AW + R + PW + 2 * D)
    di = dh
    Hi = (n_idx - di) // (di + 1)
    sizes = (AW, R, Hi * di, di, Hi, PW, 2 * D)
    offs = [0]
    for s in sizes:
        offs.append(offs[-1] + s)
    ws = [w_in[:, offs[k]:offs[k + 1]].astype(BF16) for k in range(len(sizes))]
    ws[4] = jnp.pad(ws[4], ((0, 0), (0, LANES - Hi)))

    wukt = w_uk.transpose(1, 0, 2).astype(BF16)
    wuvt = w_uv.transpose(1, 2, 0).astype(BF16)
    qlt, ckva, ckvt, qit, ki, wit, u, g = _inproj(
        x, mod, ws, kv_norm_g.reshape(1, R), wukt, q_scale=dh ** -0.5 * LOG2E, qi_scale=di ** -0.5,
        wi_scale=Hi ** -0.5, Hi=Hi, tm=min(S, 512))

    topk = min(TOPK_MAX, S // 4)
    y_attn_t = _attention(qit, wit, ki, qlt, ckva, ckvt, wuvt, tq=min(S, 256), topk=topk)

    wr = jnp.pad(w_router, ((0, 0), (0, LANES - E)))
    br = jnp.pad(b_router, (0, LANES - E)).reshape(1, LANES)
    x1, h2, te, tg = _merge(x, y_attn_t, u, g, mod, w_pool_group.astype(BF16), pool_scale.reshape(1, PW),
                            w_branch_attn.astype(BF16), w_branch_pool.astype(BF16), w_out.astype(BF16),
                            ln1_g.reshape(1, D), ln1_b.reshape(1, D), wr, br,
                            tm=min(S, 256), alpha=alpha, E=E, K=K)

    T = B * S
    tmE = 256
    tmC = min(S, 256)
    tD = min(T, 1024)
    te = te.reshape(T, LANES)
    rank, counts = _route(te, tR=min(T, 512), K=K)
    block_e, n_used, pos, n_blocks = _routing_tables(te[:, :K], rank[:, :K], counts[0, :E], E, tmE)
    xs = _dispatch(_tile_major(pos, tD), h2.reshape(T, D), n_blocks * tmE, tD=tD, K=K)
    y_sorted = _experts(block_e, n_used, xs, w_gate_up, b_gate_up.reshape(E, 1, -1),
                        w_down, b_down.reshape(E, 1, D), tmE=tmE)
    out = _combine(_tile_major(pos, tmC), y_sorted, x1.reshape(T, D), tg.reshape(T, LANES), mod,
                   ln2_g.reshape(1, D), ln2_b.reshape(1, D), tmC=tmC, K=K, alpha=alpha, S=S)
    return out.reshape(B, S, D)


def kernel(x, c, w_ada, b_ada, w_in, kv_norm_g, w_uk, w_uv, w_pool_group, pool_scale, w_branch_attn,
           w_branch_pool, w_out, ln1_g, ln1_b, w_router, b_router, w_gate_up, b_gate_up, w_down, b_down,
           ln2_g, ln2_b):
    B, S, D = x.shape
    depth = w_ada.shape[0]
    alpha = (2.0 * depth) ** 0.25
    for l in range(depth):
        mod = _ada(c, w_ada[l], b_ada[l]).reshape(B, N_MOD, D)
        x = _layer(x, mod, w_in[l], kv_norm_g[l], w_uk[l], w_uv[l], w_pool_group[l], pool_scale[l],
                   w_branch_attn[l], w_branch_pool[l], w_out[l], ln1_g[l], ln1_b[l], w_router[l],
                   b_router[l], w_gate_up[l], b_gate_up[l], w_down[l], b_down[l], ln2_g[l], ln2_b[l],
                   alpha=alpha)
    return x
```

```python
import functools

import jax
import jax.numpy as jnp
from jax import lax
from jax.experimental import pallas as pl
from jax.experimental.pallas import tpu as pltpu

F32 = jnp.float32
BF16 = jnp.bfloat16
I32 = jnp.int32
HIGHEST = lax.Precision.HIGHEST

LN_EPS = 1e-5
TOPK_MAX = 256
TOP_K_EXPERTS = 4
POOL_WINDOWS = (2, 4, 8, 16)
SWIGLU_LIMIT = 7.0
SWIGLU_ALPHA = 1.702
N_MOD = 6
INT_MIN = -(2 ** 31)
NEG_BIG = -1e30
LANES = 128
POOL_HALO = 16
ONES_ROWS = 16
LOG2E = 1.4426950408889634
ATTN_KEY_CHUNK = 256
ATTN_HEAD_UNROLL = 4


def _store_row_slabs(ref, x):
    n, D = x.shape
    sub = D // LANES
    for s in range(sub):
        ref[pl.ds(s, n, stride=sub), :] = x[:, s * LANES:(s + 1) * LANES]


def _load_row_slabs(ref, n, sub):
    return jnp.concatenate([ref[pl.ds(s, n, stride=sub), :] for s in range(sub)], axis=1)


def _layer_norm(z, g, b):
    mu = jnp.mean(z, axis=-1, keepdims=True)
    zc = z - mu
    var = jnp.mean(zc * zc, axis=-1, keepdims=True)
    return zc * lax.rsqrt(var + LN_EPS) * g + b


def _ada_kernel(c_ref, w_ref, b_ref, o_ref):
    c = c_ref[...]
    cond = c * jax.nn.sigmoid(c)
    o_ref[...] = jnp.dot(cond, w_ref[...], precision=HIGHEST, preferred_element_type=F32) + b_ref[...]


def _ada(c, w_ada, b_ada):
    B, D = c.shape
    N = w_ada.shape[1]
    tn = D
    return pl.pallas_call(
        _ada_kernel,
        grid=(N // tn,),
        in_specs=[pl.BlockSpec((B, D), lambda j: (0, 0)),
                  pl.BlockSpec((D, tn), lambda j: (0, j)),
                  pl.BlockSpec((1, tn), lambda j: (0, j))],
        out_specs=pl.BlockSpec((B, tn), lambda j: (0, j)),
        out_shape=jax.ShapeDtypeStruct((B, N), F32),
    )(c, w_ada, b_ada.reshape(1, N))


def _inproj_kernel(x_ref, mod_ref, wq, wckv, wqi, wki, wwi, wu, wg, kvg_ref, wukt_ref,
                   qlt_o, ckva_o, ckvt_o, qit_o, ki_o, wit_o, u_o, g_o,
                   *, tm, H, dh, Hi, q_scale, qi_scale, wi_scale):
    i = pl.program_id(1)
    x = x_ref[0]
    mod = mod_ref[0]
    h = (x * (1.0 + mod[1:2]) + mod[0:1]).astype(BF16)

    def proj(w):
        return jnp.dot(h, w[...], preferred_element_type=F32)

    qt = proj(wq).T.astype(BF16)
    for hh in range(H):
        qlt = jnp.dot(wukt_ref[hh], qt[hh * dh:(hh + 1) * dh, :], preferred_element_type=F32) * q_scale
        qlt_o[0, hh] = qlt.astype(BF16)

    ckv = proj(wckv)
    ckv = ckv * lax.rsqrt(jnp.mean(ckv * ckv, axis=-1, keepdims=True) + LN_EPS) * kvg_ref[...]
    ckvt_o[0, 0:ckv.shape[1], :] = ckv.T.astype(BF16)
    ckvt_o[0, ckv.shape[1]:, :] = jnp.ones((ONES_ROWS, tm), BF16)
    R = ckv.shape[1]
    pos = i * tm + lax.broadcasted_iota(I32, (tm, R), 0)
    lane = lax.broadcasted_iota(I32, (tm, R), 1)
    pos_hi = ((pos >> 8) << 8).astype(F32)
    pos_lo = (pos & 255).astype(F32)
    extra = jnp.where((lane == 0) | (lane == 2), pos_hi, jnp.where((lane == 1) | (lane == 3), pos_lo, 0.0))
    ckva_o[0, :, 0:R] = ckv.astype(BF16)
    ckva_o[0, :, R:2 * R] = extra.astype(BF16)

    qit_o[0] = (proj(wqi) * qi_scale).T.astype(BF16)
    ki_o[0] = proj(wki).astype(BF16)
    wit_o[0] = (proj(wwi) * wi_scale).T[0:Hi, :]
    u_o[0] = proj(wu)
    g_o[0] = proj(wg).astype(BF16)


def _inproj(x, mod, ws, kv_norm_g, wukt, *, q_scale, qi_scale, wi_scale, Hi, tm):
    B, S, D = x.shape
    wq, wckv, wqi, wki, wwi, wu, wg = ws
    H, R, dh = wukt.shape
    row = lambda n: pl.BlockSpec((1, tm, n), lambda b, i: (b, i, 0))
    col = lambda n: pl.BlockSpec((1, n, tm), lambda b, i: (b, 0, i))
    full = lambda a: pl.BlockSpec(a.shape, lambda b, i: (0,) * a.ndim)
    out_specs = [pl.BlockSpec((1, H, R, tm), lambda b, i: (b, 0, 0, i)), row(2 * R), col(R + ONES_ROWS),
                 col(wqi.shape[1]),
                 row(wki.shape[1]), col(Hi), row(wu.shape[1]), row(wg.shape[1])]
    out_shape = [jax.ShapeDtypeStruct((B, H, R, S), BF16), jax.ShapeDtypeStruct((B, S, 2 * R), BF16),
                 jax.ShapeDtypeStruct((B, R + ONES_ROWS, S), BF16),
                 jax.ShapeDtypeStruct((B, wqi.shape[1], S), BF16),
                 jax.ShapeDtypeStruct((B, S, wki.shape[1]), BF16), jax.ShapeDtypeStruct((B, Hi, S), F32),
                 jax.ShapeDtypeStruct((B, S, wu.shape[1]), F32), jax.ShapeDtypeStruct((B, S, wg.shape[1]), BF16)]
    return pl.pallas_call(
        functools.partial(_inproj_kernel, tm=tm, H=H, dh=dh, Hi=Hi, q_scale=q_scale, qi_scale=qi_scale,
                          wi_scale=wi_scale),
        grid=(B, S // tm),
        in_specs=[row(D), pl.BlockSpec((1, N_MOD, D), lambda b, i: (b, 0, 0))]
                 + [full(w) for w in ws] + [full(kv_norm_g), full(wukt)],
        out_specs=out_specs,
        out_shape=out_shape,
        compiler_params=pltpu.CompilerParams(
            dimension_semantics=("parallel", "parallel"), vmem_limit_bytes=48 * 2 ** 20),
    )(x, mod, *ws, kv_norm_g, wukt)


def _key_to_f32(t):
    return pltpu.bitcast(jnp.where(t >= 0, t, t ^ 0x7FFFFFFF), F32)


def _attn_body(qit_ref, wit_ref, ki_ref, qlt_ref, aug_ref, ckva_ref, ckvt_ref, wuvt_ref, yt_ref,
               score_scr, bias_scr, lg_scr, o_scr, *, W, tq, S, H, dh, Hi, di, topk, cw):
    j = pl.program_id(1)
    qpos = j * tq + lax.broadcasted_iota(I32, (1, tq), 1)
    score = score_scr.at[0:W, :]

    wit = wit_ref[0]
    for c in range(W // cw):
        kic = ki_ref[0, c * cw:(c + 1) * cw, :]
        acc = jnp.zeros((cw, tq), F32)
        for h in range(Hi):
            d = jnp.dot(kic, qit_ref[0, h * di:(h + 1) * di, :], preferred_element_type=F32)
            acc = acc + wit[h:h + 1, :] * jnp.maximum(d, 0.0)
        kpos_c = c * cw + lax.broadcasted_iota(I32, (cw, tq), 0)
        score_scr[c * cw:(c + 1) * cw, :] = jnp.where(kpos_c <= qpos, acc, -jnp.inf)

    def count(hit):
        return jnp.sum(jnp.where(hit, 1.0, 0.0), axis=0, keepdims=True)

    def thr_step(i, t):
        cand = t + lax.shift_left(jnp.int32(1), 31 - i)
        cnt = count(score[...] >= _key_to_f32(cand))
        return jnp.where(cnt >= float(topk), cand, t)

    t = lax.fori_loop(0, 32, thr_step, jnp.full((1, tq), INT_MIN, I32))
    few = t == INT_MIN
    thr_f = _key_to_f32(t)
    up_f = _key_to_f32(t + 1)
    s = score[...]
    n_gt = count(s >= up_f)
    n_ge = count(s >= thr_f)
    need = float(topk) - n_gt

    kpos = lax.broadcasted_iota(I32, (W, tq), 0)
    nbits = max(1, (S - 1).bit_length())

    def tie_search():
        def step(i, jm):
            cand = jm + lax.shift_right_logical(jnp.int32(2 ** (nbits - 1)), i)
            sc = score[...]
            f = count((sc >= thr_f) & jnp.logical_not(sc >= up_f) & (kpos < cand))
            return jnp.where(f < need, cand, jm)
        return lax.fori_loop(0, nbits, step, jnp.zeros((1, tq), I32))

    any_tie = jnp.max(jnp.where(few, 0.0, n_ge - n_gt - need)) > 0.0
    jmax = lax.cond(any_tie, tie_search, lambda: jnp.full((1, tq), S, I32))
    sel = (few | (s >= up_f) | ((s >= thr_f) & (kpos <= jmax))) & (kpos <= qpos)
    bias_scr[0:W, :] = jnp.where(sel, 0.0, NEG_BIG)

    R = qlt_ref.shape[2]
    ck = min(ATTN_KEY_CHUNK, W)

    def head(h, carry):
        qa = jnp.concatenate([qlt_ref[0, h], aug_ref[h]], axis=0)
        m = None
        for c in range(W // ck):
            keys = slice(c * ck, (c + 1) * ck)
            lg = jnp.dot(ckva_ref[0, keys, :], qa, preferred_element_type=F32) + bias_scr[keys, :]
            lg_scr[keys, :] = lg
            mc = jnp.max(lg, axis=0, keepdims=True)
            m = mc if m is None else jnp.maximum(m, mc)
        acc = jnp.zeros((R + ONES_ROWS, tq), F32)
        for c in range(W // ck):
            keys = slice(c * ck, (c + 1) * ck)
            p = jnp.exp2(lg_scr[keys, :] - m)
            acc = acc + jnp.dot(ckvt_ref[0, :, keys], p.astype(BF16), preferred_element_type=F32)
        o_scr[h] = (acc[0:R] / acc[R:R + 1]).astype(BF16)
        return carry

    lax.fori_loop(0, H, head, 0, unroll=ATTN_HEAD_UNROLL)
    for h in range(H):
        yt = jnp.dot(wuvt_ref[h], o_scr[h], preferred_element_type=F32)
        yt_ref[0, h * dh:(h + 1) * dh, :] = yt.astype(BF16)


def _attn_kernel(*refs, tq, cw, S, **kw):
    j = pl.program_id(1)
    nc = ((j + 1) * tq + cw - 1) // cw
    for w in range(1, S // cw + 1):
        @pl.when(nc == w)
        def _():
            _attn_body(*refs, W=w * cw, tq=tq, S=S, cw=cw, **kw)


def _attention(qit, wit, ki, qlt, ckva, ckvt, wuvt, *, tq, topk):
    B, H, R, S = qlt.shape
    dh = wuvt.shape[1]
    di = ki.shape[2]
    Hi = wit.shape[1]
    slopes = jnp.asarray([LOG2E * 2.0 ** (-8.0 * (h + 1) / H) for h in range(H)], F32)
    s_hi = slopes.astype(BF16)
    s_lo = (slopes - s_hi.astype(F32)).astype(BF16)
    aug = jnp.zeros((H, R, tq), BF16)
    aug = aug.at[:, 0:2, :].set(s_hi[:, None, None]).at[:, 2:4, :].set(s_lo[:, None, None])
    kern = functools.partial(_attn_kernel, tq=tq, S=S, H=H, dh=dh, Hi=Hi, di=di, topk=topk, cw=min(S, 512))
    col = lambda n: pl.BlockSpec((1, n, tq), lambda b, j: (b, 0, j))
    full = lambda a: pl.BlockSpec(a.shape, lambda b, j: (0,) * a.ndim)
    return pl.pallas_call(
        kern,
        grid=(B, S // tq),
        in_specs=[col(qit.shape[1]), col(Hi), pl.BlockSpec((1, S, di), lambda b, j: (b, 0, 0)),
                  pl.BlockSpec((1, H, R, tq), lambda b, j: (b, 0, 0, j)), full(aug),
                  pl.BlockSpec((1, S, 2 * R), lambda b, j: (b, 0, 0)),
                  pl.BlockSpec((1, R + ONES_ROWS, S), lambda b, j: (b, 0, 0)), full(wuvt)],
        out_specs=col(H * dh),
        out_shape=jax.ShapeDtypeStruct((B, H * dh, S), BF16),
        scratch_shapes=[pltpu.VMEM((S, tq), F32), pltpu.VMEM((S, tq), F32), pltpu.VMEM((S, tq), F32),
                        pltpu.VMEM((H, R, tq), BF16)],
        compiler_params=pltpu.CompilerParams(
            dimension_semantics=("parallel", "parallel"), vmem_limit_bytes=48 * 2 ** 20),
    )(qit, wit, ki, qlt, aug, ckva, ckvt, wuvt)


def _merge_kernel(x_ref, yat_ref, u_ref, halo_ref, g_ref, mod_ref, wpg_ref, psc_ref, wba_ref, wbp_ref,
                  wout_ref, ln1g_ref, ln1b_ref, wr_ref, br_ref,
                  x1_o, h2_o, te_o, tg_o, ext_scr, *, tm, D, G, alpha, E, K):
    i = pl.program_id(1)

    @pl.when(i == 0)
    def _():
        ext_scr[0:POOL_HALO, :] = jnp.zeros((POOL_HALO, ext_scr.shape[1]), F32)

    @pl.when(i > 0)
    def _():
        ext_scr[0:POOL_HALO, :] = halo_ref[0]

    ext_scr[POOL_HALO:POOL_HALO + tm, :] = u_ref[0]

    pos = i * tm + lax.broadcasted_iota(I32, (tm, 1), 0)
    ys = []
    for g, w in enumerate(POOL_WINDOWS):
        cols = slice(g * G, (g + 1) * G)
        cur = ext_scr[POOL_HALO:POOL_HALO + tm, cols]
        acc = cur
        for back in range(1, w):
            acc = acc + ext_scr[POOL_HALO - back:POOL_HALO - back + tm, cols]
        cnt = jnp.minimum(pos + 1, w).astype(F32)
        pooled = acc / cnt - cur
        ys.append(jnp.dot(pooled.astype(BF16), wpg_ref[g], preferred_element_type=F32))
    y_pool = jnp.concatenate(ys, axis=1) * psc_ref[...]

    gates = g_ref[0]
    ya = yat_ref[0].astype(F32).T.astype(BF16)
    a = jnp.dot(ya, wba_ref[...], preferred_element_type=F32)
    p = jnp.dot(y_pool.astype(BF16), wbp_ref[...], preferred_element_type=F32)
    mix = jax.nn.sigmoid(gates[:, :D].astype(F32)) * a + jax.nn.sigmoid(gates[:, D:].astype(F32)) * p
    o = jnp.dot(mix.astype(BF16), wout_ref[...], preferred_element_type=F32)
    mod = mod_ref[0]
    x1 = _layer_norm(alpha * x_ref[0] + mod[2:3] * o, ln1g_ref[...], ln1b_ref[...])
    x1_o[0] = x1
    h2 = x1 * (1.0 + mod[4:5]) + mod[3:4]
    _store_row_slabs(h2_o.at[0], h2)

    logits = jnp.dot(h2, wr_ref[...], precision=HIGHEST, preferred_element_type=F32) + br_ref[...]
    lane = lax.broadcasted_iota(I32, logits.shape, 1)
    logits = jnp.where(lane < E, logits, -jnp.inf)
    te = jnp.zeros(logits.shape, I32)
    vals = []
    for k in range(K):
        m = jnp.max(logits, axis=1, keepdims=True)
        idx = jnp.min(jnp.where(logits == m, lane, LANES), axis=1, keepdims=True)
        te = jnp.where(lane == k, idx, te)
        vals.append(m)
        logits = jnp.where(lane == idx, -jnp.inf, logits)
    ex = [jnp.exp(v - vals[0]) for v in vals]
    den = ex[0]
    for e in ex[1:]:
        den = den + e
    tg = jnp.zeros(logits.shape, F32)
    for k in range(K):
        tg = jnp.where(lane == k, ex[k] / den, tg)
    te_o[0] = te
    tg_o[0] = tg


def _merge(x, yat, u, g, mod, wpg, psc, wba, wbp, wout, ln1g, ln1b, wr, br, *, tm, alpha, E, K):
    B, S, D = x.shape
    PW = u.shape[2]
    G = wpg.shape[1]
    kern = functools.partial(_merge_kernel, tm=tm, D=D, G=G, alpha=alpha, E=E, K=K)
    row = lambda n: pl.BlockSpec((1, tm, n), lambda b, i: (b, i, 0))
    full = lambda a: pl.BlockSpec(a.shape, lambda b, i: (0,) * a.ndim)
    hb = tm // POOL_HALO
    halo = pl.BlockSpec((1, POOL_HALO, PW), lambda b, i: (b, jnp.maximum(i * hb - 1, 0), 0))
    return pl.pallas_call(
        kern,
        grid=(B, S // tm),
        in_specs=[row(D), pl.BlockSpec((1, yat.shape[1], tm), lambda b, i: (b, 0, i)), row(PW), halo,
                  row(g.shape[2]), pl.BlockSpec((1, N_MOD, D), lambda b, i: (b, 0, 0)),
                  full(wpg), full(psc), full(wba), full(wbp), full(wout), full(ln1g), full(ln1b),
                  full(wr), full(br)],
        out_specs=[row(D), pl.BlockSpec((1, tm * (D // LANES), LANES), lambda b, i: (b, i, 0)),
                   row(LANES), row(LANES)],
        out_shape=[jax.ShapeDtypeStruct((B, S, D), F32),
                   jax.ShapeDtypeStruct((B, S * (D // LANES), LANES), F32),
                   jax.ShapeDtypeStruct((B, S, LANES), I32), jax.ShapeDtypeStruct((B, S, LANES), F32)],
        scratch_shapes=[pltpu.VMEM((POOL_HALO + tm, PW), F32)],
        compiler_params=pltpu.CompilerParams(
            dimension_semantics=("parallel", "arbitrary"), vmem_limit_bytes=48 * 2 ** 20),
    )(x, yat, u, u, g, mod, wpg, psc, wba, wbp, wout, ln1g, ln1b, wr, br)


def _route_kernel(te_ref, rank_o, cnt_o, base_scr, *, tR, K):
    i = pl.program_id(0)

    @pl.when(i == 0)
    def _():
        base_scr[...] = jnp.zeros(base_scr.shape, F32)

    te = te_ref[...]
    lane = lax.broadcasted_iota(I32, te.shape, 1)
    hits = [lane == te[:, k:k + 1] for k in range(K)]
    onehot = hits[0].astype(F32)
    for k in range(1, K):
        onehot = onehot + hits[k].astype(F32)
    earlier = (lax.broadcasted_iota(I32, (tR, tR), 0) > lax.broadcasted_iota(I32, (tR, tR), 1)).astype(BF16)
    before = jnp.dot(earlier, onehot.astype(BF16), preferred_element_type=F32) + base_scr[...]
    rank = jnp.zeros(te.shape, I32)
    for k in range(K):
        rk = jnp.sum(jnp.where(hits[k], before, 0.0), axis=1, keepdims=True)
        rank = jnp.where(lane == k, rk.astype(I32), rank)
    rank_o[...] = rank
    base_scr[...] = base_scr[...] + jnp.sum(onehot, axis=0, keepdims=True)
    cnt_o[...] = base_scr[...].astype(I32)


def _route(te, *, tR, K):
    T = te.shape[0]
    return pl.pallas_call(
        functools.partial(_route_kernel, tR=tR, K=K),
        grid=(T // tR,),
        in_specs=[pl.BlockSpec((tR, LANES), lambda i: (i, 0))],
        out_specs=[pl.BlockSpec((tR, LANES), lambda i: (i, 0)), pl.BlockSpec((1, LANES), lambda i: (0, 0))],
        out_shape=[jax.ShapeDtypeStruct((T, LANES), I32), jax.ShapeDtypeStruct((1, LANES), I32)],
        scratch_shapes=[pltpu.VMEM((1, LANES), F32)],
        compiler_params=pltpu.CompilerParams(dimension_semantics=("arbitrary",)),
    )(te)


def _dispatch_kernel(pos_ref, h2_ref, xs_in, xs_out, sem, *, tD, K, sub):
    del xs_in

    def issue(r, carry):
        src = h2_ref.at[pl.ds(pl.multiple_of(r * sub, sub), sub), :]
        for k in range(K):
            p = pl.multiple_of(pos_ref[0, 0, k * tD + r], sub)
            pltpu.async_copy(src, xs_out.at[pl.ds(p, sub), :], sem, priority=k % 2)
        return carry

    lax.fori_loop(0, tD, issue, 0, unroll=8)
    for k in range(K):
        pltpu.make_async_copy(h2_ref, xs_out.at[pl.ds(0, tD * sub), :], sem).wait()


def _dispatch(pos, h2, n_rows, *, tD, K, sub):
    T = h2.shape[0] // sub
    xs0 = jnp.zeros((n_rows * sub, LANES), h2.dtype)
    return pl.pallas_call(
        functools.partial(_dispatch_kernel, tD=tD, K=K, sub=sub),
        grid=(T // tD,),
        in_specs=[pl.BlockSpec((1, 1, K * tD), lambda i: (i, 0, 0), memory_space=pltpu.SMEM),
                  pl.BlockSpec((tD * sub, LANES), lambda i: (i, 0)),
                  pl.BlockSpec(memory_space=pl.ANY)],
        out_specs=pl.BlockSpec(memory_space=pl.ANY),
        out_shape=jax.ShapeDtypeStruct((n_rows * sub, LANES), h2.dtype),
        scratch_shapes=[pltpu.SemaphoreType.DMA(())],
        input_output_aliases={2: 0},
        compiler_params=pltpu.CompilerParams(
            dimension_semantics=("arbitrary",), has_side_effects=True),
    )(pos, h2, xs0)


def _expert_kernel(be_ref, nu_ref, x_ref, wgu_ref, bgu_ref, wd_ref, bd_ref, y_ref, wgu_bf, wd_bf,
                   *, F, tmE, sub):
    i = pl.program_id(0)
    new_expert = (i == 0) | (be_ref[i] != be_ref[jnp.maximum(i - 1, 0)])

    @pl.when(new_expert)
    def _():
        wgu_bf[...] = wgu_ref[0].astype(BF16)
        wd_bf[...] = wd_ref[0].astype(BF16)

    @pl.when(i < nu_ref[0])
    def _():
        x = _load_row_slabs(x_ref, tmE, sub).astype(BF16)
        gu = jnp.dot(x, wgu_bf[...], preferred_element_type=F32) + bgu_ref[0]
        glu = jnp.minimum(gu[:, :F], SWIGLU_LIMIT)
        lin = jnp.clip(gu[:, F:], -SWIGLU_LIMIT, SWIGLU_LIMIT)
        act = glu * jax.nn.sigmoid(SWIGLU_ALPHA * glu) * (lin + 1.0)
        y = jnp.dot(act.astype(BF16), wd_bf[...], preferred_element_type=F32) + bd_ref[0]
        _store_row_slabs(y_ref, y)

    @pl.when(i >= nu_ref[0])
    def _():
        y_ref[...] = jnp.zeros(y_ref.shape, F32)


def _experts(block_e, n_used, xs, wgu, bgu, wd, bd, *, tmE):
    E, D, F2 = wgu.shape
    F = F2 // 2
    sub = D // LANES
    n_blocks = xs.shape[0] // (tmE * sub)
    slab_block = pl.BlockSpec((tmE * sub, LANES), lambda i, be, nu: (i, 0))
    grid_spec = pltpu.PrefetchScalarGridSpec(
        num_scalar_prefetch=2,
        grid=(n_blocks,),
        in_specs=[slab_block,
                  pl.BlockSpec((1, D, F2), lambda i, be, nu: (be[i], 0, 0)),
                  pl.BlockSpec((1, 1, F2), lambda i, be, nu: (be[i], 0, 0)),
                  pl.BlockSpec((1, F, D), lambda i, be, nu: (be[i], 0, 0)),
                  pl.BlockSpec((1, 1, D), lambda i, be, nu: (be[i], 0, 0))],
        out_specs=slab_block,
        scratch_shapes=[pltpu.VMEM((D, F2), BF16), pltpu.VMEM((F, D), BF16)],
    )
    return pl.pallas_call(
        functools.partial(_expert_kernel, F=F, tmE=tmE, sub=sub),
        grid_spec=grid_spec,
        out_shape=jax.ShapeDtypeStruct(xs.shape, F32),
        compiler_params=pltpu.CompilerParams(
            dimension_semantics=("arbitrary",), vmem_limit_bytes=56 * 2 ** 20),
    )(block_e, n_used, xs, wgu, bgu, wd, bd)


def _combine_kernel(pos_ref, nxt_ref, y_hbm, x1_ref, tg_ref, mod_ref, g_ref, b_ref, o_ref, buf, sem,
                    *, tmC, K, alpha, nt, sub):
    i = pl.program_id(0)
    slot = lax.rem(i, 2)

    def fetch(p_ref, s):
        def issue(r, carry):
            dst_rows = pl.ds(pl.multiple_of(r * sub, sub), sub)
            for k in range(K):
                p = pl.multiple_of(p_ref[0, 0, k * tmC + r], sub)
                pltpu.async_copy(y_hbm.at[pl.ds(p, sub), :], buf.at[s, k, dst_rows, :], sem.at[s],
                                 priority=k % 2)
            return carry

        lax.fori_loop(0, tmC, issue, 0, unroll=8)

    @pl.when(i == 0)
    def _():
        fetch(pos_ref, slot)

    @pl.when(i + 1 < nt)
    def _():
        fetch(nxt_ref, 1 - slot)

    for k in range(K):
        pltpu.make_async_copy(y_hbm.at[pl.ds(0, tmC * sub), :], buf.at[slot, k], sem.at[slot]).wait()

    tg = tg_ref[...]
    y = tg[:, 0:1] * _load_row_slabs(buf.at[slot, 0], tmC, sub)
    for k in range(1, K):
        y = y + tg[:, k:k + 1] * _load_row_slabs(buf.at[slot, k], tmC, sub)
    mod = mod_ref[0]
    o_ref[...] = _layer_norm(alpha * x1_ref[...] + mod[5:6] * y, g_ref[...], b_ref[...])


def _combine(pos, y_sorted, x1, tg, mod, ln2g, ln2b, *, tmC, K, alpha, S):
    T, D = x1.shape
    nt = T // tmC
    per_seq = S // tmC
    sub = D // LANES
    return pl.pallas_call(
        functools.partial(_combine_kernel, tmC=tmC, K=K, alpha=alpha, nt=nt, sub=sub),
        grid=(nt,),
        in_specs=[pl.BlockSpec((1, 1, K * tmC), lambda i: (i, 0, 0), memory_space=pltpu.SMEM),
                  pl.BlockSpec((1, 1, K * tmC), lambda i: (jnp.minimum(i + 1, nt - 1), 0, 0),
                               memory_space=pltpu.SMEM),
                  pl.BlockSpec(memory_space=pl.ANY),
                  pl.BlockSpec((tmC, D), lambda i: (i, 0)),
                  pl.BlockSpec((tmC, LANES), lambda i: (i, 0)),
                  pl.BlockSpec((1, N_MOD, D), lambda i: (i // per_seq, 0, 0)),
                  pl.BlockSpec((1, D), lambda i: (0, 0)),
                  pl.BlockSpec((1, D), lambda i: (0, 0))],
        out_specs=pl.BlockSpec((tmC, D), lambda i: (i, 0)),
        out_shape=jax.ShapeDtypeStruct((T, D), F32),
        scratch_shapes=[pltpu.VMEM((2, K, tmC * sub, LANES), F32), pltpu.SemaphoreType.DMA((2,))],
        compiler_params=pltpu.CompilerParams(
            dimension_semantics=("arbitrary",), vmem_limit_bytes=48 * 2 ** 20),
    )(pos, pos, y_sorted, x1, tg, mod, ln2g, ln2b)


def _routing_tables(top_e, rank, counts, E, tmE, sub):
    T, K = top_e.shape
    padded = (counts + tmE - 1) // tmE * tmE
    pad_end = jnp.cumsum(padded)
    pad_start = pad_end - padded
    n_blocks = -(-(T * K) // tmE) + E
    block_start = jnp.arange(n_blocks, dtype=I32) * tmE
    block_e = jnp.minimum(jnp.sum((pad_end[None, :] <= block_start[:, None]).astype(I32), axis=1), E - 1)
    n_used = (pad_end[-1] // tmE).astype(I32).reshape(1)
    pos = (jnp.take(pad_start, top_e) + rank).astype(I32) * sub
    return block_e.astype(I32), n_used, pos, n_blocks


def _tile_major(pos, tile):
    T, K = pos.shape
    return pos.reshape(T // tile, tile, K).transpose(0, 2, 1).reshape(T // tile, 1, K * tile)


def _layer(x, mod, w_in, kv_norm_g, w_uk, w_uv, w_pool_group, pool_scale, w_branch_attn, w_branch_pool,
           w_out, ln1_g, ln1_b, w_router, b_router, w_gate_up, b_gate_up, w_down, b_down, ln2_g, ln2_b,
           *, alpha):
    B, S, D = x.shape
    R, H, dh = w_uk.shape
    AW = H * dh
    PW = pool_scale.shape[0]
    E = w_router.shape[1]
    K = TOP_K_EXPERTS
    n_idx = w_in.shape[1] - (AW + R + PW + 2 * D)
    di = dh
    Hi = (n_idx - di) // (di + 1)
    sizes = (AW, R, Hi * di, di, Hi, PW, 2 * D)
    offs = [0]
    for s in sizes:
        offs.append(offs[-1] + s)
    ws = [w_in[:, offs[k]:offs[k + 1]].astype(BF16) for k in range(len(sizes))]
    ws[4] = jnp.pad(ws[4], ((0, 0), (0, LANES - Hi)))

    wukt = w_uk.transpose(1, 0, 2).astype(BF16)
    wuvt = w_uv.transpose(1, 2, 0).astype(BF16)
    qlt, ckva, ckvt, qit, ki, wit, u, g = _inproj(
        x, mod, ws, kv_norm_g.reshape(1, R), wukt, q_scale=dh ** -0.5 * LOG2E, qi_scale=di ** -0.5,
        wi_scale=Hi ** -0.5, Hi=Hi, tm=min(S, 512))

    topk = min(TOPK_MAX, S // 4)
    y_attn_t = _attention(qit, wit, ki, qlt, ckva, ckvt, wuvt, tq=min(S, 256), topk=topk)

    wr = jnp.pad(w_router, ((0, 0), (0, LANES - E)))
    br = jnp.pad(b_router, (0, LANES - E)).reshape(1, LANES)
    x1, h2, te, tg = _merge(x, y_attn_t, u, g, mod, w_pool_group.astype(BF16), pool_scale.reshape(1, PW),
                            w_branch_attn.astype(BF16), w_branch_pool.astype(BF16), w_out.astype(BF16),
                            ln1_g.reshape(1, D), ln1_b.reshape(1, D), wr, br,
                            tm=min(S, 256), alpha=alpha, E=E, K=K)

    T = B * S
    tmE = 256
    tmC = min(S, 256)
    tD = min(T, 1024)
    sub = D // LANES
    te = te.reshape(T, LANES)
    rank, counts = _route(te, tR=min(T, 512), K=K)
    block_e, n_used, pos, n_blocks = _routing_tables(te[:, :K], rank[:, :K], counts[0, :E], E, tmE, sub)
    xs = _dispatch(_tile_major(pos, tD), h2.reshape(T * sub, LANES), n_blocks * tmE, tD=tD, K=K, sub=sub)
    y_sorted = _experts(block_e, n_used, xs, w_gate_up, b_gate_up.reshape(E, 1, -1),
                        w_down, b_down.reshape(E, 1, D), tmE=tmE)
    out = _combine(_tile_major(pos, tmC), y_sorted, x1.reshape(T, D), tg.reshape(T, LANES), mod,
                   ln2_g.reshape(1, D), ln2_b.reshape(1, D), tmC=tmC, K=K, alpha=alpha, S=S)
    return out.reshape(B, S, D)


def kernel(x, c, w_ada, b_ada, w_in, kv_norm_g, w_uk, w_uv, w_pool_group, pool_scale, w_branch_attn,
           w_branch_pool, w_out, ln1_g, ln1_b, w_router, b_router, w_gate_up, b_gate_up, w_down, b_down,
           ln2_g, ln2_b):
    B, S, D = x.shape
    depth = w_ada.shape[0]
    alpha = (2.0 * depth) ** 0.25
    for l in range(depth):
        mod = _ada(c, w_ada[l], b_ada[l]).reshape(B, N_MOD, D)
        x = _layer(x, mod, w_in[l], kv_norm_g[l], w_uk[l], w_uv[l], w_pool_group[l], pool_scale[l],
                   w_branch_attn[l], w_branch_pool[l], w_out[l], ln1_g[l], ln1_b[l], w_router[l],
                   b_router[l], w_gate_up[l], b_gate_up[l], w_down[l], b_down[l], ln2_g[l], ln2_b[l],
                   alpha=alpha)
    return x
```

```python
import functools

import jax
import jax.numpy as jnp
from jax import lax
from jax.experimental import pallas as pl
from jax.experimental.pallas import tpu as pltpu

F32 = jnp.float32
BF16 = jnp.bfloat16
I32 = jnp.int32
HIGHEST = lax.Precision.HIGHEST

LN_EPS = 1e-5
TOPK_MAX = 256
TOP_K_EXPERTS = 4
POOL_WINDOWS = (2, 4, 8, 16)
SWIGLU_LIMIT = 7.0
SWIGLU_ALPHA = 1.702
N_MOD = 6
INT_MIN = -(2 ** 31)
NEG_BIG = -1e30
LANES = 128
POOL_HALO = 16
ONES_ROWS = 16
LOG2E = 1.4426950408889634
ATTN_KEY_CHUNK = 256
ATTN_HEAD_UNROLL = 4


def _store_row_slabs(ref, x):
    n, D = x.shape
    sub = D // LANES
    for s in range(sub):
        ref[pl.ds(s, n, stride=sub), :] = x[:, s * LANES:(s + 1) * LANES]


def _load_row_slabs(ref, n, sub):
    return jnp.concatenate([ref[pl.ds(s, n, stride=sub), :] for s in range(sub)], axis=1)


def _layer_norm(z, g, b):
    mu = jnp.mean(z, axis=-1, keepdims=True)
    zc = z - mu
    var = jnp.mean(zc * zc, axis=-1, keepdims=True)
    return zc * lax.rsqrt(var + LN_EPS) * g + b


def _ada_kernel(c_ref, w_ref, b_ref, o_ref):
    c = c_ref[...]
    cond = c * jax.nn.sigmoid(c)
    o_ref[...] = jnp.dot(cond, w_ref[...], precision=HIGHEST, preferred_element_type=F32) + b_ref[...]


def _ada(c, w_ada, b_ada):
    B, D = c.shape
    N = w_ada.shape[1]
    tn = D
    return pl.pallas_call(
        _ada_kernel,
        grid=(N // tn,),
        in_specs=[pl.BlockSpec((B, D), lambda j: (0, 0)),
                  pl.BlockSpec((D, tn), lambda j: (0, j)),
                  pl.BlockSpec((1, tn), lambda j: (0, j))],
        out_specs=pl.BlockSpec((B, tn), lambda j: (0, j)),
        out_shape=jax.ShapeDtypeStruct((B, N), F32),
    )(c, w_ada, b_ada.reshape(1, N))


def _inproj_kernel(x_ref, mod_ref, wq, wckv, wqi, wki, wwi, wu, wg, kvg_ref, wukt_ref,
                   qlt_o, ckva_o, ckvt_o, qit_o, ki_o, wit_o, u_o, g_o,
                   *, tm, H, dh, Hi, q_scale, qi_scale, wi_scale):
    i = pl.program_id(1)
    x = x_ref[0]
    mod = mod_ref[0]
    h = (x * (1.0 + mod[1:2]) + mod[0:1]).astype(BF16)

    def proj(w):
        return jnp.dot(h, w[...], preferred_element_type=F32)

    qt = proj(wq).T.astype(BF16)
    for hh in range(H):
        qlt = jnp.dot(wukt_ref[hh], qt[hh * dh:(hh + 1) * dh, :], preferred_element_type=F32) * q_scale
        qlt_o[0, hh] = qlt.astype(BF16)

    ckv = proj(wckv)
    ckv = ckv * lax.rsqrt(jnp.mean(ckv * ckv, axis=-1, keepdims=True) + LN_EPS) * kvg_ref[...]
    ckvt_o[0, 0:ckv.shape[1], :] = ckv.T.astype(BF16)
    ckvt_o[0, ckv.shape[1]:, :] = jnp.ones((ONES_ROWS, tm), BF16)
    R = ckv.shape[1]
    pos = i * tm + lax.broadcasted_iota(I32, (tm, R), 0)
    lane = lax.broadcasted_iota(I32, (tm, R), 1)
    pos_hi = ((pos >> 8) << 8).astype(F32)
    pos_lo = (pos & 255).astype(F32)
    extra = jnp.where((lane == 0) | (lane == 2), pos_hi, jnp.where((lane == 1) | (lane == 3), pos_lo, 0.0))
    ckva_o[0, :, 0:R] = ckv.astype(BF16)
    ckva_o[0, :, R:2 * R] = extra.astype(BF16)

    qit_o[0] = (proj(wqi) * qi_scale).T.astype(BF16)
    ki_o[0] = proj(wki).astype(BF16)
    wit_o[0] = (proj(wwi) * wi_scale).T[0:Hi, :]
    u_o[0] = proj(wu)
    g_o[0] = proj(wg).astype(BF16)


def _inproj(x, mod, ws, kv_norm_g, wukt, *, q_scale, qi_scale, wi_scale, Hi, tm):
    B, S, D = x.shape
    wq, wckv, wqi, wki, wwi, wu, wg = ws
    H, R, dh = wukt.shape
    row = lambda n: pl.BlockSpec((1, tm, n), lambda b, i: (b, i, 0))
    col = lambda n: pl.BlockSpec((1, n, tm), lambda b, i: (b, 0, i))
    full = lambda a: pl.BlockSpec(a.shape, lambda b, i: (0,) * a.ndim)
    out_specs = [pl.BlockSpec((1, H, R, tm), lambda b, i: (b, 0, 0, i)), row(2 * R), col(R + ONES_ROWS),
                 col(wqi.shape[1]),
                 row(wki.shape[1]), col(Hi), row(wu.shape[1]), row(wg.shape[1])]
    out_shape = [jax.ShapeDtypeStruct((B, H, R, S), BF16), jax.ShapeDtypeStruct((B, S, 2 * R), BF16),
                 jax.ShapeDtypeStruct((B, R + ONES_ROWS, S), BF16),
                 jax.ShapeDtypeStruct((B, wqi.shape[1], S), BF16),
                 jax.ShapeDtypeStruct((B, S, wki.shape[1]), BF16), jax.ShapeDtypeStruct((B, Hi, S), F32),
                 jax.ShapeDtypeStruct((B, S, wu.shape[1]), F32), jax.ShapeDtypeStruct((B, S, wg.shape[1]), BF16)]
    return pl.pallas_call(
        functools.partial(_inproj_kernel, tm=tm, H=H, dh=dh, Hi=Hi, q_scale=q_scale, qi_scale=qi_scale,
                          wi_scale=wi_scale),
        grid=(B, S // tm),
        in_specs=[row(D), pl.BlockSpec((1, N_MOD, D), lambda b, i: (b, 0, 0))]
                 + [full(w) for w in ws] + [full(kv_norm_g), full(wukt)],
        out_specs=out_specs,
        out_shape=out_shape,
        compiler_params=pltpu.CompilerParams(
            dimension_semantics=("parallel", "parallel"), vmem_limit_bytes=48 * 2 ** 20),
    )(x, mod, *ws, kv_norm_g, wukt)


def _key_to_f32(t):
    return pltpu.bitcast(jnp.where(t >= 0, t, t ^ 0x7FFFFFFF), F32)


def _attn_body(qit_ref, wit_ref, ki_ref, qlt_ref, aug_ref, ckva_ref, ckvt_ref, wuvt_ref, yt_ref,
               score_scr, bias_scr, lg_scr, o_scr, *, W, tq, S, H, dh, Hi, di, topk, cw):
    j = pl.program_id(1)
    qpos = j * tq + lax.broadcasted_iota(I32, (1, tq), 1)
    score = score_scr.at[0:W, :]

    wit = wit_ref[0]
    for c in range(W // cw):
        kic = ki_ref[0, c * cw:(c + 1) * cw, :]
        acc = jnp.zeros((cw, tq), F32)
        for h in range(Hi):
            d = jnp.dot(kic, qit_ref[0, h * di:(h + 1) * di, :], preferred_element_type=F32)
            acc = acc + wit[h:h + 1, :] * jnp.maximum(d, 0.0)
        kpos_c = c * cw + lax.broadcasted_iota(I32, (cw, tq), 0)
        score_scr[c * cw:(c + 1) * cw, :] = jnp.where(kpos_c <= qpos, acc, -jnp.inf)

    def count(hit):
        return jnp.sum(jnp.where(hit, 1.0, 0.0), axis=0, keepdims=True)

    def thr_step(i, t):
        cand = t + lax.shift_left(jnp.int32(1), 31 - i)
        cnt = count(score[...] >= _key_to_f32(cand))
        return jnp.where(cnt >= float(topk), cand, t)

    t = lax.fori_loop(0, 32, thr_step, jnp.full((1, tq), INT_MIN, I32))
    few = t == INT_MIN
    thr_f = _key_to_f32(t)
    up_f = _key_to_f32(t + 1)
    s = score[...]
    n_gt = count(s >= up_f)
    n_ge = count(s >= thr_f)
    need = float(topk) - n_gt

    kpos = lax.broadcasted_iota(I32, (W, tq), 0)
    nbits = max(1, (S - 1).bit_length())

    def tie_search():
        def step(i, jm):
            cand = jm + lax.shift_right_logical(jnp.int32(2 ** (nbits - 1)), i)
            sc = score[...]
            f = count((sc >= thr_f) & jnp.logical_not(sc >= up_f) & (kpos < cand))
            return jnp.where(f < need, cand, jm)
        return lax.fori_loop(0, nbits, step, jnp.zeros((1, tq), I32))

    any_tie = jnp.max(jnp.where(few, 0.0, n_ge - n_gt - need)) > 0.0
    jmax = lax.cond(any_tie, tie_search, lambda: jnp.full((1, tq), S, I32))
    sel = (few | (s >= up_f) | ((s >= thr_f) & (kpos <= jmax))) & (kpos <= qpos)
    bias_scr[0:W, :] = jnp.where(sel, 0.0, NEG_BIG)

    R = qlt_ref.shape[2]
    ck = min(ATTN_KEY_CHUNK, W)

    def head(h, carry):
        qa = jnp.concatenate([qlt_ref[0, h], aug_ref[h]], axis=0)
        m = None
        for c in range(W // ck):
            keys = slice(c * ck, (c + 1) * ck)
            lg = jnp.dot(ckva_ref[0, keys, :], qa, preferred_element_type=F32) + bias_scr[keys, :]
            lg_scr[keys, :] = lg
            mc = jnp.max(lg, axis=0, keepdims=True)
            m = mc if m is None else jnp.maximum(m, mc)
        acc = jnp.zeros((R + ONES_ROWS, tq), F32)
        for c in range(W // ck):
            keys = slice(c * ck, (c + 1) * ck)
            p = jnp.exp2(lg_scr[keys, :] - m)
            acc = acc + jnp.dot(ckvt_ref[0, :, keys], p.astype(BF16), preferred_element_type=F32)
        o_scr[h] = (acc[0:R] / acc[R:R + 1]).astype(BF16)
        return carry

    lax.fori_loop(0, H, head, 0, unroll=ATTN_HEAD_UNROLL)
    for h in range(H):
        yt = jnp.dot(wuvt_ref[h], o_scr[h], preferred_element_type=F32)
        yt_ref[0, h * dh:(h + 1) * dh, :] = yt.astype(BF16)


def _attn_kernel(*refs, tq, cw, S, **kw):
    j = pl.program_id(1)
    nc = ((j + 1) * tq + cw - 1) // cw
    for w in range(1, S // cw + 1):
        @pl.when(nc == w)
        def _():
            _attn_body(*refs, W=w * cw, tq=tq, S=S, cw=cw, **kw)


def _attention(qit, wit, ki, qlt, ckva, ckvt, wuvt, *, tq, topk):
    B, H, R, S = qlt.shape
    dh = wuvt.shape[1]
    di = ki.shape[2]
    Hi = wit.shape[1]
    slopes = jnp.asarray([LOG2E * 2.0 ** (-8.0 * (h + 1) / H) for h in range(H)], F32)
    s_hi = slopes.astype(BF16)
    s_lo = (slopes - s_hi.astype(F32)).astype(BF16)
    aug = jnp.zeros((H, R, tq), BF16)
    aug = aug.at[:, 0:2, :].set(s_hi[:, None, None]).at[:, 2:4, :].set(s_lo[:, None, None])
    kern = functools.partial(_attn_kernel, tq=tq, S=S, H=H, dh=dh, Hi=Hi, di=di, topk=topk, cw=min(S, 512))
    col = lambda n: pl.BlockSpec((1, n, tq), lambda b, j: (b, 0, j))
    full = lambda a: pl.BlockSpec(a.shape, lambda b, j: (0,) * a.ndim)
    return pl.pallas_call(
        kern,
        grid=(B, S // tq),
        in_specs=[col(qit.shape[1]), col(Hi), pl.BlockSpec((1, S, di), lambda b, j: (b, 0, 0)),
                  pl.BlockSpec((1, H, R, tq), lambda b, j: (b, 0, 0, j)), full(aug),
                  pl.BlockSpec((1, S, 2 * R), lambda b, j: (b, 0, 0)),
                  pl.BlockSpec((1, R + ONES_ROWS, S), lambda b, j: (b, 0, 0)), full(wuvt)],
        out_specs=col(H * dh),
        out_shape=jax.ShapeDtypeStruct((B, H * dh, S), BF16),
        scratch_shapes=[pltpu.VMEM((S, tq), F32), pltpu.VMEM((S, tq), F32), pltpu.VMEM((S, tq), F32),
                        pltpu.VMEM((H, R, tq), BF16)],
        compiler_params=pltpu.CompilerParams(
            dimension_semantics=("parallel", "parallel"), vmem_limit_bytes=48 * 2 ** 20),
    )(qit, wit, ki, qlt, aug, ckva, ckvt, wuvt)


def _merge_kernel(x_ref, yat_ref, u_ref, halo_ref, g_ref, mod_ref, wpg_ref, psc_ref, wba_ref, wbp_ref,
                  wout_ref, ln1g_ref, ln1b_ref, wr_ref, br_ref,
                  x1_o, h2_o, te_o, tg_o, ext_scr, *, tm, D, G, alpha, E, K):
    i = pl.program_id(1)

    @pl.when(i == 0)
    def _():
        ext_scr[0:POOL_HALO, :] = jnp.zeros((POOL_HALO, ext_scr.shape[1]), F32)

    @pl.when(i > 0)
    def _():
        ext_scr[0:POOL_HALO, :] = halo_ref[0]

    ext_scr[POOL_HALO:POOL_HALO + tm, :] = u_ref[0]

    pos = i * tm + lax.broadcasted_iota(I32, (tm, 1), 0)
    ys = []
    for g, w in enumerate(POOL_WINDOWS):
        cols = slice(g * G, (g + 1) * G)
        cur = ext_scr[POOL_HALO:POOL_HALO + tm, cols]
        acc = cur
        for back in range(1, w):
            acc = acc + ext_scr[POOL_HALO - back:POOL_HALO - back + tm, cols]
        cnt = jnp.minimum(pos + 1, w).astype(F32)
        pooled = acc / cnt - cur
        ys.append(jnp.dot(pooled.astype(BF16), wpg_ref[g], preferred_element_type=F32))
    y_pool = jnp.concatenate(ys, axis=1) * psc_ref[...]

    gates = g_ref[0]
    ya = yat_ref[0].astype(F32).T.astype(BF16)
    a = jnp.dot(ya, wba_ref[...], preferred_element_type=F32)
    p = jnp.dot(y_pool.astype(BF16), wbp_ref[...], preferred_element_type=F32)
    mix = jax.nn.sigmoid(gates[:, :D].astype(F32)) * a + jax.nn.sigmoid(gates[:, D:].astype(F32)) * p
    o = jnp.dot(mix.astype(BF16), wout_ref[...], preferred_element_type=F32)
    mod = mod_ref[0]
    x1 = _layer_norm(alpha * x_ref[0] + mod[2:3] * o, ln1g_ref[...], ln1b_ref[...])
    x1_o[0] = x1
    h2 = x1 * (1.0 + mod[4:5]) + mod[3:4]
    _store_row_slabs(h2_o.at[0], h2)

    h2_hi = h2.astype(BF16)
    h2_lo = (h2 - h2_hi.astype(F32)).astype(BF16)
    logits = (jnp.dot(h2_hi, wr_ref[0], preferred_element_type=F32)
              + jnp.dot(h2_hi, wr_ref[1], preferred_element_type=F32)
              + jnp.dot(h2_lo, wr_ref[0], preferred_element_type=F32)) + br_ref[...]
    lane = lax.broadcasted_iota(I32, logits.shape, 1)
    logits = jnp.where(lane < E, logits, -jnp.inf)
    te = jnp.zeros(logits.shape, I32)
    vals = []
    for k in range(K):
        m = jnp.max(logits, axis=1, keepdims=True)
        idx = jnp.min(jnp.where(logits == m, lane, LANES), axis=1, keepdims=True)
        te = jnp.where(lane == k, idx, te)
        vals.append(m)
        logits = jnp.where(lane == idx, -jnp.inf, logits)
    ex = [jnp.exp(v - vals[0]) for v in vals]
    den = ex[0]
    for e in ex[1:]:
        den = den + e
    tg = jnp.zeros(logits.shape, F32)
    for k in range(K):
        tg = jnp.where(lane == k, ex[k] / den, tg)
    te_o[0] = te
    tg_o[0] = tg


def _merge(x, yat, u, g, mod, wpg, psc, wba, wbp, wout, ln1g, ln1b, wr, br, *, tm, alpha, E, K):
    B, S, D = x.shape
    PW = u.shape[2]
    G = wpg.shape[1]
    kern = functools.partial(_merge_kernel, tm=tm, D=D, G=G, alpha=alpha, E=E, K=K)
    row = lambda n: pl.BlockSpec((1, tm, n), lambda b, i: (b, i, 0))
    full = lambda a: pl.BlockSpec(a.shape, lambda b, i: (0,) * a.ndim)
    hb = tm // POOL_HALO
    halo = pl.BlockSpec((1, POOL_HALO, PW), lambda b, i: (b, jnp.maximum(i * hb - 1, 0), 0))
    return pl.pallas_call(
        kern,
        grid=(B, S // tm),
        in_specs=[row(D), pl.BlockSpec((1, yat.shape[1], tm), lambda b, i: (b, 0, i)), row(PW), halo,
                  row(g.shape[2]), pl.BlockSpec((1, N_MOD, D), lambda b, i: (b, 0, 0)),
                  full(wpg), full(psc), full(wba), full(wbp), full(wout), full(ln1g), full(ln1b),
                  full(wr), full(br)],
        out_specs=[row(D), pl.BlockSpec((1, tm * (D // LANES), LANES), lambda b, i: (b, i, 0)),
                   row(LANES), row(LANES)],
        out_shape=[jax.ShapeDtypeStruct((B, S, D), F32),
                   jax.ShapeDtypeStruct((B, S * (D // LANES), LANES), F32),
                   jax.ShapeDtypeStruct((B, S, LANES), I32), jax.ShapeDtypeStruct((B, S, LANES), F32)],
        scratch_shapes=[pltpu.VMEM((POOL_HALO + tm, PW), F32)],
        compiler_params=pltpu.CompilerParams(
            dimension_semantics=("parallel", "arbitrary"), vmem_limit_bytes=48 * 2 ** 20),
    )(x, yat, u, u, g, mod, wpg, psc, wba, wbp, wout, ln1g, ln1b, wr, br)


def _route_kernel(te_ref, rank_o, cnt_o, base_scr, *, tR, K):
    i = pl.program_id(0)

    @pl.when(i == 0)
    def _():
        base_scr[...] = jnp.zeros(base_scr.shape, F32)

    te = te_ref[...]
    lane = lax.broadcasted_iota(I32, te.shape, 1)
    hits = [lane == te[:, k:k + 1] for k in range(K)]
    onehot = hits[0].astype(F32)
    for k in range(1, K):
        onehot = onehot + hits[k].astype(F32)
    earlier = (lax.broadcasted_iota(I32, (tR, tR), 0) > lax.broadcasted_iota(I32, (tR, tR), 1)).astype(BF16)
    before = jnp.dot(earlier, onehot.astype(BF16), preferred_element_type=F32) + base_scr[...]
    rank = jnp.zeros(te.shape, I32)
    for k in range(K):
        rk = jnp.sum(jnp.where(hits[k], before, 0.0), axis=1, keepdims=True)
        rank = jnp.where(lane == k, rk.astype(I32), rank)
    rank_o[...] = rank
    base_scr[...] = base_scr[...] + jnp.sum(onehot, axis=0, keepdims=True)
    cnt_o[...] = base_scr[...].astype(I32)


def _route(te, *, tR, K):
    T = te.shape[0]
    return pl.pallas_call(
        functools.partial(_route_kernel, tR=tR, K=K),
        grid=(T // tR,),
        in_specs=[pl.BlockSpec((tR, LANES), lambda i: (i, 0))],
        out_specs=[pl.BlockSpec((tR, LANES), lambda i: (i, 0)), pl.BlockSpec((1, LANES), lambda i: (0, 0))],
        out_shape=[jax.ShapeDtypeStruct((T, LANES), I32), jax.ShapeDtypeStruct((1, LANES), I32)],
        scratch_shapes=[pltpu.VMEM((1, LANES), F32)],
        compiler_params=pltpu.CompilerParams(dimension_semantics=("arbitrary",)),
    )(te)


def _dispatch_kernel(pe_ref, nu_ref, pos_ref, h2_ref, xs_out, zero_scr, sem, zsem,
                     *, tD, K, sub, E, tmE, n_blocks):
    blk = tmE * sub

    @pl.when(pl.program_id(0) == 0)
    def _():
        zero_scr[...] = jnp.zeros(zero_scr.shape, zero_scr.dtype)

        def zero_block(b):
            return pltpu.make_async_copy(zero_scr, xs_out.at[pl.ds(pl.multiple_of(b * blk, blk), blk), :], zsem)

        def last_block(e):
            return jnp.maximum(pe_ref[e] // tmE - 1, 0)

        def start_last(e, carry):
            zero_block(last_block(e)).start()
            return carry

        def start_tail(b, carry):
            zero_block(b).start()
            return carry

        def wait_last(e, carry):
            zero_block(last_block(e)).wait()
            return carry

        def wait_tail(b, carry):
            zero_block(b).wait()
            return carry

        lax.fori_loop(0, E, start_last, 0)
        lax.fori_loop(nu_ref[0], n_blocks, start_tail, 0)
        lax.fori_loop(0, E, wait_last, 0)
        lax.fori_loop(nu_ref[0], n_blocks, wait_tail, 0)

    def issue(r, carry):
        src = h2_ref.at[pl.ds(pl.multiple_of(r * sub, sub), sub), :]
        for k in range(K):
            p = pl.multiple_of(pos_ref[0, 0, k * tD + r], sub)
            pltpu.async_copy(src, xs_out.at[pl.ds(p, sub), :], sem, priority=k % 2)
        return carry

    lax.fori_loop(0, tD, issue, 0, unroll=8)
    for k in range(K):
        pltpu.make_async_copy(h2_ref, xs_out.at[pl.ds(0, tD * sub), :], sem).wait()


def _dispatch(pad_end, n_used, pos, h2, n_blocks, *, tD, tmE, K, sub):
    T = h2.shape[0] // sub
    E = pad_end.shape[0]
    grid_spec = pltpu.PrefetchScalarGridSpec(
        num_scalar_prefetch=2,
        grid=(T // tD,),
        in_specs=[pl.BlockSpec((1, 1, K * tD), lambda i, pe, nu: (i, 0, 0), memory_space=pltpu.SMEM),
                  pl.BlockSpec((tD * sub, LANES), lambda i, pe, nu: (i, 0))],
        out_specs=pl.BlockSpec(memory_space=pl.ANY),
        scratch_shapes=[pltpu.VMEM((tmE * sub, LANES), h2.dtype), pltpu.SemaphoreType.DMA(()),
                        pltpu.SemaphoreType.DMA(())],
    )
    return pl.pallas_call(
        functools.partial(_dispatch_kernel, tD=tD, K=K, sub=sub, E=E, tmE=tmE, n_blocks=n_blocks),
        grid_spec=grid_spec,
        out_shape=jax.ShapeDtypeStruct((n_blocks * tmE * sub, LANES), h2.dtype),
        compiler_params=pltpu.CompilerParams(
            dimension_semantics=("arbitrary",), has_side_effects=True),
    )(pad_end, n_used, pos, h2)


def _expert_kernel(be_ref, nu_ref, x_ref, wgu_ref, bgu_ref, wd_ref, bd_ref, y_ref, wgu_bf, wd_bf,
                   *, F, tmE, sub):
    i = pl.program_id(0)
    new_expert = (i == 0) | (be_ref[i] != be_ref[jnp.maximum(i - 1, 0)])

    @pl.when(new_expert)
    def _():
        wgu_bf[...] = wgu_ref[0].astype(BF16)
        wd_bf[...] = wd_ref[0].astype(BF16)

    @pl.when(i < nu_ref[0])
    def _():
        x = _load_row_slabs(x_ref, tmE, sub).astype(BF16)
        gu = jnp.dot(x, wgu_bf[...], preferred_element_type=F32) + bgu_ref[0]
        glu = jnp.minimum(gu[:, :F], SWIGLU_LIMIT)
        lin = jnp.clip(gu[:, F:], -SWIGLU_LIMIT, SWIGLU_LIMIT)
        act = glu * jax.nn.sigmoid(SWIGLU_ALPHA * glu) * (lin + 1.0)
        y = jnp.dot(act.astype(BF16), wd_bf[...], preferred_element_type=F32) + bd_ref[0]
        _store_row_slabs(y_ref, y)

    @pl.when(i >= nu_ref[0])
    def _():
        y_ref[...] = jnp.zeros(y_ref.shape, F32)


def _experts(block_e, n_used, xs, wgu, bgu, wd, bd, *, tmE):
    E, D, F2 = wgu.shape
    F = F2 // 2
    sub = D // LANES
    n_blocks = xs.shape[0] // (tmE * sub)
    slab_block = pl.BlockSpec((tmE * sub, LANES), lambda i, be, nu: (i, 0))
    grid_spec = pltpu.PrefetchScalarGridSpec(
        num_scalar_prefetch=2,
        grid=(n_blocks,),
        in_specs=[slab_block,
                  pl.BlockSpec((1, D, F2), lambda i, be, nu: (be[i], 0, 0)),
                  pl.BlockSpec((1, 1, F2), lambda i, be, nu: (be[i], 0, 0)),
                  pl.BlockSpec((1, F, D), lambda i, be, nu: (be[i], 0, 0)),
                  pl.BlockSpec((1, 1, D), lambda i, be, nu: (be[i], 0, 0))],
        out_specs=slab_block,
        scratch_shapes=[pltpu.VMEM((D, F2), BF16), pltpu.VMEM((F, D), BF16)],
    )
    return pl.pallas_call(
        functools.partial(_expert_kernel, F=F, tmE=tmE, sub=sub),
        grid_spec=grid_spec,
        out_shape=jax.ShapeDtypeStruct(xs.shape, F32),
        compiler_params=pltpu.CompilerParams(
            dimension_semantics=("arbitrary",), vmem_limit_bytes=56 * 2 ** 20),
    )(block_e, n_used, xs, wgu, bgu, wd, bd)


def _combine_kernel(pos_ref, nxt_ref, y_hbm, x1_ref, tg_ref, mod_ref, g_ref, b_ref, o_ref, buf, sem,
                    *, tmC, K, alpha, nt, sub):
    i = pl.program_id(0)
    slot = lax.rem(i, 2)

    def fetch(p_ref, s):
        def issue(r, carry):
            dst_rows = pl.ds(pl.multiple_of(r * sub, sub), sub)
            for k in range(K):
                p = pl.multiple_of(p_ref[0, 0, k * tmC + r], sub)
                pltpu.async_copy(y_hbm.at[pl.ds(p, sub), :], buf.at[s, k, dst_rows, :], sem.at[s],
                                 priority=k % 2)
            return carry

        lax.fori_loop(0, tmC, issue, 0, unroll=8)

    @pl.when(i == 0)
    def _():
        fetch(pos_ref, slot)

    @pl.when(i + 1 < nt)
    def _():
        fetch(nxt_ref, 1 - slot)

    for k in range(K):
        pltpu.make_async_copy(y_hbm.at[pl.ds(0, tmC * sub), :], buf.at[slot, k], sem.at[slot]).wait()

    tg = tg_ref[...]
    y = tg[:, 0:1] * _load_row_slabs(buf.at[slot, 0], tmC, sub)
    for k in range(1, K):
        y = y + tg[:, k:k + 1] * _load_row_slabs(buf.at[slot, k], tmC, sub)
    mod = mod_ref[0]
    o_ref[...] = _layer_norm(alpha * x1_ref[...] + mod[5:6] * y, g_ref[...], b_ref[...])


def _combine(pos, y_sorted, x1, tg, mod, ln2g, ln2b, *, tmC, K, alpha, S):
    T, D = x1.shape
    nt = T // tmC
    per_seq = S // tmC
    sub = D // LANES
    return pl.pallas_call(
        functools.partial(_combine_kernel, tmC=tmC, K=K, alpha=alpha, nt=nt, sub=sub),
        grid=(nt,),
        in_specs=[pl.BlockSpec((1, 1, K * tmC), lambda i: (i, 0, 0), memory_space=pltpu.SMEM),
                  pl.BlockSpec((1, 1, K * tmC), lambda i: (jnp.minimum(i + 1, nt - 1), 0, 0),
                               memory_space=pltpu.SMEM),
                  pl.BlockSpec(memory_space=pl.ANY),
                  pl.BlockSpec((tmC, D), lambda i: (i, 0)),
                  pl.BlockSpec((tmC, LANES), lambda i: (i, 0)),
                  pl.BlockSpec((1, N_MOD, D), lambda i: (i // per_seq, 0, 0)),
                  pl.BlockSpec((1, D), lambda i: (0, 0)),
                  pl.BlockSpec((1, D), lambda i: (0, 0))],
        out_specs=pl.BlockSpec((tmC, D), lambda i: (i, 0)),
        out_shape=jax.ShapeDtypeStruct((T, D), F32),
        scratch_shapes=[pltpu.VMEM((2, K, tmC * sub, LANES), F32), pltpu.SemaphoreType.DMA((2,))],
        compiler_params=pltpu.CompilerParams(
            dimension_semantics=("arbitrary",), vmem_limit_bytes=48 * 2 ** 20),
    )(pos, pos, y_sorted, x1, tg, mod, ln2g, ln2b)


def _routing_tables(top_e, rank, counts, E, tmE, sub):
    T, K = top_e.shape
    padded = (counts + tmE - 1) // tmE * tmE
    pad_end = jnp.cumsum(padded)
    pad_start = pad_end - padded
    n_blocks = -(-(T * K) // tmE) + E
    block_start = jnp.arange(n_blocks, dtype=I32) * tmE
    block_e = jnp.minimum(jnp.sum((pad_end[None, :] <= block_start[:, None]).astype(I32), axis=1), E - 1)
    n_used = (pad_end[-1] // tmE).astype(I32).reshape(1)
    pos = (jnp.take(pad_start, top_e) + rank).astype(I32) * sub
    return block_e.astype(I32), n_used, pad_end.astype(I32), pos, n_blocks


def _tile_major(pos, tile):
    T, K = pos.shape
    return pos.reshape(T // tile, tile, K).transpose(0, 2, 1).reshape(T // tile, 1, K * tile)


def _layer(x, mod, w_in, kv_norm_g, w_uk, w_uv, w_pool_group, pool_scale, w_branch_attn, w_branch_pool,
           w_out, ln1_g, ln1_b, w_router, b_router, w_gate_up, b_gate_up, w_down, b_down, ln2_g, ln2_b,
           *, alpha):
    B, S, D = x.shape
    R, H, dh = w_uk.shape
    AW = H * dh
    PW = pool_scale.shape[0]
    E = w_router.shape[1]
    K = TOP_K_EXPERTS
    n_idx = w_in.shape[1] - (AW + R + PW + 2 * D)
    di = dh
    Hi = (n_idx - di) // (di + 1)
    sizes = (AW, R, Hi * di, di, Hi, PW, 2 * D)
    offs = [0]
    for s in sizes:
        offs.append(offs[-1] + s)
    ws = [w_in[:, offs[k]:offs[k + 1]].astype(BF16) for k in range(len(sizes))]
    ws[4] = jnp.pad(ws[4], ((0, 0), (0, LANES - Hi)))

    wukt = w_uk.transpose(1, 0, 2).astype(BF16)
    wuvt = w_uv.transpose(1, 2, 0).astype(BF16)
    qlt, ckva, ckvt, qit, ki, wit, u, g = _inproj(
        x, mod, ws, kv_norm_g.reshape(1, R), wukt, q_scale=dh ** -0.5 * LOG2E, qi_scale=di ** -0.5,
        wi_scale=Hi ** -0.5, Hi=Hi, tm=min(S, 512))

    topk = min(TOPK_MAX, S // 4)
    y_attn_t = _attention(qit, wit, ki, qlt, ckva, ckvt, wuvt, tq=min(S, 256), topk=topk)

    wr = jnp.pad(w_router, ((0, 0), (0, LANES - E)))
    wr_hi = wr.astype(BF16)
    wr = jnp.stack([wr_hi, (wr - wr_hi.astype(F32)).astype(BF16)])
    br = jnp.pad(b_router, (0, LANES - E)).reshape(1, LANES)
    x1, h2, te, tg = _merge(x, y_attn_t, u, g, mod, w_pool_group.astype(BF16), pool_scale.reshape(1, PW),
                            w_branch_attn.astype(BF16), w_branch_pool.astype(BF16), w_out.astype(BF16),
                            ln1_g.reshape(1, D), ln1_b.reshape(1, D), wr, br,
                            tm=min(S, 256), alpha=alpha, E=E, K=K)

    T = B * S
    tmE = 512
    tmC = min(S, 256)
    tD = min(T, 1024)
    sub = D // LANES
    te = te.reshape(T, LANES)
    rank, counts = _route(te, tR=min(T, 512), K=K)
    block_e, n_used, pad_end, pos, n_blocks = _routing_tables(te[:, :K], rank[:, :K], counts[0, :E], E, tmE,
                                                              sub)
    xs = _dispatch(pad_end, n_used, _tile_major(pos, tD), h2.reshape(T * sub, LANES), n_blocks,
                   tD=tD, tmE=tmE, K=K, sub=sub)
    y_sorted = _experts(block_e, n_used, xs, w_gate_up, b_gate_up.reshape(E, 1, -1),
                        w_down, b_down.reshape(E, 1, D), tmE=tmE)
    out = _combine(_tile_major(pos, tmC), y_sorted, x1.reshape(T, D), tg.reshape(T, LANES), mod,
                   ln2_g.reshape(1, D), ln2_b.reshape(1, D), tmC=tmC, K=K, alpha=alpha, S=S)
    return out.reshape(B, S, D)


def kernel(x, c, w_ada, b_ada, w_in, kv_norm_g, w_uk, w_uv, w_pool_group, pool_scale, w_branch_attn,
           w_branch_pool, w_out, ln1_g, ln1_b, w_router, b_router, w_gate_up, b_gate_up, w_down, b_down,
           ln2_g, ln2_b):
    B, S, D = x.shape
    depth = w_ada.shape[0]
    alpha = (2.0 * depth) ** 0.25
    for l in range(depth):
        mod = _ada(c, w_ada[l], b_ada[l]).reshape(B, N_MOD, D)
        x = _layer(x, mod, w_in[l], kv_norm_g[l], w_uk[l], w_uv[l], w_pool_group[l], pool_scale[l],
                   w_branch_attn[l], w_branch_pool[l], w_out[l], ln1_g[l], ln1_b[l], w_router[l],
                   b_router[l], w_gate_up[l], b_gate_up[l], w_down[l], b_down[l], ln2_g[l], ln2_b[l],
                   alpha=alpha)
    return x
```

```python
import functools

import jax
import jax.numpy as jnp
from jax import lax
from jax.experimental import pallas as pl
from jax.experimental.pallas import tpu as pltpu

F32 = jnp.float32
BF16 = jnp.bfloat16
I32 = jnp.int32
HIGHEST = lax.Precision.HIGHEST

LN_EPS = 1e-5
TOPK_MAX = 256
TOP_K_EXPERTS = 4
POOL_WINDOWS = (2, 4, 8, 16)
SWIGLU_LIMIT = 7.0
SWIGLU_ALPHA = 1.702
N_MOD = 6
INT_MIN = -(2 ** 31)
NEG_BIG = -1e30
LANES = 128
POOL_HALO = 16
ONES_ROWS = 16
LOG2E = 1.4426950408889634
ATTN_KEY_CHUNK = 256
ATTN_HEAD_UNROLL = 4


def _store_row_slabs(ref, x):
    n, D = x.shape
    sub = D // LANES
    for s in range(sub):
        ref[pl.ds(s, n, stride=sub), :] = x[:, s * LANES:(s + 1) * LANES]


def _load_row_slabs(ref, n, sub):
    return jnp.concatenate([ref[pl.ds(s, n, stride=sub), :] for s in range(sub)], axis=1)


def _layer_norm(z, g, b):
    mu = jnp.mean(z, axis=-1, keepdims=True)
    zc = z - mu
    var = jnp.mean(zc * zc, axis=-1, keepdims=True)
    return zc * lax.rsqrt(var + LN_EPS) * g + b


def _ada_kernel(c_ref, w_ref, b_ref, o_ref):
    c = c_ref[...]
    cond = c * jax.nn.sigmoid(c)
    o_ref[...] = jnp.dot(cond, w_ref[...], precision=HIGHEST, preferred_element_type=F32) + b_ref[...]


def _ada(c, w_ada, b_ada):
    B, D = c.shape
    N = w_ada.shape[1]
    tn = D
    return pl.pallas_call(
        _ada_kernel,
        grid=(N // tn,),
        in_specs=[pl.BlockSpec((B, D), lambda j: (0, 0)),
                  pl.BlockSpec((D, tn), lambda j: (0, j)),
                  pl.BlockSpec((1, tn), lambda j: (0, j))],
        out_specs=pl.BlockSpec((B, tn), lambda j: (0, j)),
        out_shape=jax.ShapeDtypeStruct((B, N), F32),
    )(c, w_ada, b_ada.reshape(1, N))


def _inproj_kernel(x_ref, mod_ref, wq, wckv, wqi, wki, wwi, wu, wg, kvg_ref, wukt_ref,
                   qlt_o, ckva_o, ckvt_o, qit_o, ki_o, wit_o, u_o, g_o,
                   *, tm, H, dh, Hi, q_scale, qi_scale, wi_scale):
    i = pl.program_id(1)
    x = x_ref[0]
    mod = mod_ref[0]
    h = (x * (1.0 + mod[1:2]) + mod[0:1]).astype(BF16)

    def proj(w):
        return jnp.dot(h, w[...], preferred_element_type=F32)

    qt = proj(wq).T.astype(BF16)
    for hh in range(H):
        qlt = jnp.dot(wukt_ref[hh], qt[hh * dh:(hh + 1) * dh, :], preferred_element_type=F32) * q_scale
        qlt_o[0, hh] = qlt.astype(BF16)

    ckv = proj(wckv)
    ckv = ckv * lax.rsqrt(jnp.mean(ckv * ckv, axis=-1, keepdims=True) + LN_EPS) * kvg_ref[...]
    ckvt_o[0, 0:ckv.shape[1], :] = ckv.T.astype(BF16)
    ckvt_o[0, ckv.shape[1]:, :] = jnp.ones((ONES_ROWS, tm), BF16)
    R = ckv.shape[1]
    pos = i * tm + lax.broadcasted_iota(I32, (tm, R), 0)
    lane = lax.broadcasted_iota(I32, (tm, R), 1)
    pos_hi = ((pos >> 8) << 8).astype(F32)
    pos_lo = (pos & 255).astype(F32)
    extra = jnp.where((lane == 0) | (lane == 2), pos_hi, jnp.where((lane == 1) | (lane == 3), pos_lo, 0.0))
    ckva_o[0, :, 0:R] = ckv.astype(BF16)
    ckva_o[0, :, R:2 * R] = extra.astype(BF16)

    qit_o[0] = (proj(wqi) * qi_scale).T.astype(BF16)
    ki_o[0] = proj(wki).astype(BF16)
    wit_o[0] = (proj(wwi) * wi_scale).T[0:Hi, :]
    u_o[0] = proj(wu)
    g_o[0] = proj(wg).astype(BF16)


def _inproj(x, mod, ws, kv_norm_g, wukt, *, q_scale, qi_scale, wi_scale, Hi, tm):
    B, S, D = x.shape
    wq, wckv, wqi, wki, wwi, wu, wg = ws
    H, R, dh = wukt.shape
    row = lambda n: pl.BlockSpec((1, tm, n), lambda b, i: (b, i, 0))
    col = lambda n: pl.BlockSpec((1, n, tm), lambda b, i: (b, 0, i))
    full = lambda a: pl.BlockSpec(a.shape, lambda b, i: (0,) * a.ndim)
    out_specs = [pl.BlockSpec((1, H, R, tm), lambda b, i: (b, 0, 0, i)), row(2 * R), col(R + ONES_ROWS),
                 col(wqi.shape[1]),
                 row(wki.shape[1]), col(Hi), row(wu.shape[1]), row(wg.shape[1])]
    out_shape = [jax.ShapeDtypeStruct((B, H, R, S), BF16), jax.ShapeDtypeStruct((B, S, 2 * R), BF16),
                 jax.ShapeDtypeStruct((B, R + ONES_ROWS, S), BF16),
                 jax.ShapeDtypeStruct((B, wqi.shape[1], S), BF16),
                 jax.ShapeDtypeStruct((B, S, wki.shape[1]), BF16), jax.ShapeDtypeStruct((B, Hi, S), F32),
                 jax.ShapeDtypeStruct((B, S, wu.shape[1]), F32), jax.ShapeDtypeStruct((B, S, wg.shape[1]), BF16)]
    return pl.pallas_call(
        functools.partial(_inproj_kernel, tm=tm, H=H, dh=dh, Hi=Hi, q_scale=q_scale, qi_scale=qi_scale,
                          wi_scale=wi_scale),
        grid=(B, S // tm),
        in_specs=[row(D), pl.BlockSpec((1, N_MOD, D), lambda b, i: (b, 0, 0))]
                 + [full(w) for w in ws] + [full(kv_norm_g), full(wukt)],
        out_specs=out_specs,
        out_shape=out_shape,
        compiler_params=pltpu.CompilerParams(
            dimension_semantics=("parallel", "parallel"), vmem_limit_bytes=48 * 2 ** 20),
    )(x, mod, *ws, kv_norm_g, wukt)


def _key_to_f32(t):
    return pltpu.bitcast(jnp.where(t >= 0, t, t ^ 0x7FFFFFFF), F32)


def _attn_body(qit_ref, wit_ref, ki_ref, qlt_ref, aug_ref, ckva_ref, ckvt_ref, wuvt_ref, yt_ref,
               score_scr, bias_scr, lg_scr, o_scr, *, W, tq, S, H, dh, Hi, di, topk, cw):
    j = pl.program_id(1)
    qpos = j * tq + lax.broadcasted_iota(I32, (1, tq), 1)
    score = score_scr.at[0:W, :]

    wit = wit_ref[0]
    for c in range(W // cw):
        kic = ki_ref[0, c * cw:(c + 1) * cw, :]
        acc = jnp.zeros((cw, tq), F32)
        for h in range(Hi):
            d = jnp.dot(kic, qit_ref[0, h * di:(h + 1) * di, :], preferred_element_type=F32)
            acc = acc + wit[h:h + 1, :] * jnp.maximum(d, 0.0)
        kpos_c = c * cw + lax.broadcasted_iota(I32, (cw, tq), 0)
        score_scr[c * cw:(c + 1) * cw, :] = jnp.where(kpos_c <= qpos, acc, -jnp.inf)

    def count(hit):
        return jnp.sum(jnp.where(hit, 1.0, 0.0), axis=0, keepdims=True)

    def thr_step(i, t):
        cand = t + lax.shift_left(jnp.int32(1), 31 - i)
        cnt = count(score[...] >= _key_to_f32(cand))
        return jnp.where(cnt >= float(topk), cand, t)

    t = lax.fori_loop(0, 32, thr_step, jnp.full((1, tq), INT_MIN, I32))
    few = t == INT_MIN
    thr_f = _key_to_f32(t)
    up_f = _key_to_f32(t + 1)
    s = score[...]
    n_gt = count(s >= up_f)
    n_ge = count(s >= thr_f)
    need = float(topk) - n_gt

    kpos = lax.broadcasted_iota(I32, (W, tq), 0)
    nbits = max(1, (S - 1).bit_length())

    def tie_search():
        def step(i, jm):
            cand = jm + lax.shift_right_logical(jnp.int32(2 ** (nbits - 1)), i)
            sc = score[...]
            f = count((sc >= thr_f) & jnp.logical_not(sc >= up_f) & (kpos < cand))
            return jnp.where(f < need, cand, jm)
        return lax.fori_loop(0, nbits, step, jnp.zeros((1, tq), I32))

    any_tie = jnp.max(jnp.where(few, 0.0, n_ge - n_gt - need)) > 0.0
    jmax = lax.cond(any_tie, tie_search, lambda: jnp.full((1, tq), S, I32))
    sel = (few | (s >= up_f) | ((s >= thr_f) & (kpos <= jmax))) & (kpos <= qpos)
    bias_scr[0:W, :] = jnp.where(sel, 0.0, NEG_BIG)

    R = qlt_ref.shape[2]
    ck = min(ATTN_KEY_CHUNK, W)

    def head(h, carry):
        qa = jnp.concatenate([qlt_ref[0, h], aug_ref[h]], axis=0)
        m = None
        for c in range(W // ck):
            keys = slice(c * ck, (c + 1) * ck)
            lg = jnp.dot(ckva_ref[0, keys, :], qa, preferred_element_type=F32) + bias_scr[keys, :]
            lg_scr[keys, :] = lg
            mc = jnp.max(lg, axis=0, keepdims=True)
            m = mc if m is None else jnp.maximum(m, mc)
        acc = jnp.zeros((R + ONES_ROWS, tq), F32)
        for c in range(W // ck):
            keys = slice(c * ck, (c + 1) * ck)
            p = jnp.exp2(lg_scr[keys, :] - m)
            acc = acc + jnp.dot(ckvt_ref[0, :, keys], p.astype(BF16), preferred_element_type=F32)
        o_scr[h] = (acc[0:R] / acc[R:R + 1]).astype(BF16)
        return carry

    lax.fori_loop(0, H, head, 0, unroll=ATTN_HEAD_UNROLL)
    for h in range(H):
        yt = jnp.dot(wuvt_ref[h], o_scr[h], preferred_element_type=F32)
        yt_ref[0, h * dh:(h + 1) * dh, :] = yt.astype(BF16)


def _attn_kernel(*refs, tq, cw, S, **kw):
    j = pl.program_id(1)
    nc = ((j + 1) * tq + cw - 1) // cw
    for w in range(1, S // cw + 1):
        @pl.when(nc == w)
        def _():
            _attn_body(*refs, W=w * cw, tq=tq, S=S, cw=cw, **kw)


def _attention(qit, wit, ki, qlt, ckva, ckvt, wuvt, *, tq, topk):
    B, H, R, S = qlt.shape
    dh = wuvt.shape[1]
    di = ki.shape[2]
    Hi = wit.shape[1]
    slopes = jnp.asarray([LOG2E * 2.0 ** (-8.0 * (h + 1) / H) for h in range(H)], F32)
    s_hi = slopes.astype(BF16)
    s_lo = (slopes - s_hi.astype(F32)).astype(BF16)
    aug = jnp.zeros((H, R, tq), BF16)
    aug = aug.at[:, 0:2, :].set(s_hi[:, None, None]).at[:, 2:4, :].set(s_lo[:, None, None])
    kern = functools.partial(_attn_kernel, tq=tq, S=S, H=H, dh=dh, Hi=Hi, di=di, topk=topk, cw=min(S, 512))
    col = lambda n: pl.BlockSpec((1, n, tq), lambda b, j: (b, 0, j))
    full = lambda a: pl.BlockSpec(a.shape, lambda b, j: (0,) * a.ndim)
    return pl.pallas_call(
        kern,
        grid=(B, S // tq),
        in_specs=[col(qit.shape[1]), col(Hi), pl.BlockSpec((1, S, di), lambda b, j: (b, 0, 0)),
                  pl.BlockSpec((1, H, R, tq), lambda b, j: (b, 0, 0, j)), full(aug),
                  pl.BlockSpec((1, S, 2 * R), lambda b, j: (b, 0, 0)),
                  pl.BlockSpec((1, R + ONES_ROWS, S), lambda b, j: (b, 0, 0)), full(wuvt)],
        out_specs=col(H * dh),
        out_shape=jax.ShapeDtypeStruct((B, H * dh, S), BF16),
        scratch_shapes=[pltpu.VMEM((S, tq), F32), pltpu.VMEM((S, tq), F32), pltpu.VMEM((S, tq), F32),
                        pltpu.VMEM((H, R, tq), BF16)],
        compiler_params=pltpu.CompilerParams(
            dimension_semantics=("parallel", "parallel"), vmem_limit_bytes=48 * 2 ** 20),
    )(qit, wit, ki, qlt, aug, ckva, ckvt, wuvt)


def _merge_kernel(x_ref, yat_ref, u_ref, halo_ref, g_ref, mod_ref, wpg_ref, psc_ref, wba_ref, wbp_ref,
                  wout_ref, ln1g_ref, ln1b_ref, wr_ref, br_ref,
                  x1_o, h2_o, te_o, tg_o, ext_scr, *, tm, D, G, alpha, E, K):
    i = pl.program_id(1)

    @pl.when(i == 0)
    def _():
        ext_scr[0:POOL_HALO, :] = jnp.zeros((POOL_HALO, ext_scr.shape[1]), F32)

    @pl.when(i > 0)
    def _():
        ext_scr[0:POOL_HALO, :] = halo_ref[0]

    ext_scr[POOL_HALO:POOL_HALO + tm, :] = u_ref[0]

    pos = i * tm + lax.broadcasted_iota(I32, (tm, 1), 0)
    ys = []
    for g, w in enumerate(POOL_WINDOWS):
        cols = slice(g * G, (g + 1) * G)
        cur = ext_scr[POOL_HALO:POOL_HALO + tm, cols]
        acc = cur
        for back in range(1, w):
            acc = acc + ext_scr[POOL_HALO - back:POOL_HALO - back + tm, cols]
        cnt = jnp.minimum(pos + 1, w).astype(F32)
        pooled = acc / cnt - cur
        ys.append(jnp.dot(pooled.astype(BF16), wpg_ref[g], preferred_element_type=F32))
    y_pool = jnp.concatenate(ys, axis=1) * psc_ref[...]

    gates = g_ref[0]
    ya = yat_ref[0].astype(F32).T.astype(BF16)
    a = jnp.dot(ya, wba_ref[...], preferred_element_type=F32)
    p = jnp.dot(y_pool.astype(BF16), wbp_ref[...], preferred_element_type=F32)
    mix = jax.nn.sigmoid(gates[:, :D].astype(F32)) * a + jax.nn.sigmoid(gates[:, D:].astype(F32)) * p
    o = jnp.dot(mix.astype(BF16), wout_ref[...], preferred_element_type=F32)
    mod = mod_ref[0]
    x1 = _layer_norm(alpha * x_ref[0] + mod[2:3] * o, ln1g_ref[...], ln1b_ref[...])
    x1_o[0] = x1
    h2 = x1 * (1.0 + mod[4:5]) + mod[3:4]
    _store_row_slabs(h2_o.at[0], h2)

    h2_hi = h2.astype(BF16)
    h2_lo = (h2 - h2_hi.astype(F32)).astype(BF16)
    logits = (jnp.dot(h2_hi, wr_ref[0], preferred_element_type=F32)
              + jnp.dot(h2_hi, wr_ref[1], preferred_element_type=F32)
              + jnp.dot(h2_lo, wr_ref[0], preferred_element_type=F32)) + br_ref[...]
    lane = lax.broadcasted_iota(I32, logits.shape, 1)
    logits = jnp.where(lane < E, logits, -jnp.inf)
    te = jnp.zeros(logits.shape, I32)
    vals = []
    for k in range(K):
        m = jnp.max(logits, axis=1, keepdims=True)
        idx = jnp.min(jnp.where(logits == m, lane, LANES), axis=1, keepdims=True)
        te = jnp.where(lane == k, idx, te)
        vals.append(m)
        logits = jnp.where(lane == idx, -jnp.inf, logits)
    ex = [jnp.exp(v - vals[0]) for v in vals]
    den = ex[0]
    for e in ex[1:]:
        den = den + e
    tg = jnp.zeros(logits.shape, F32)
    for k in range(K):
        tg = jnp.where(lane == k, ex[k] / den, tg)
    te_o[0] = te
    tg_o[0] = tg


def _merge(x, yat, u, g, mod, wpg, psc, wba, wbp, wout, ln1g, ln1b, wr, br, *, tm, alpha, E, K):
    B, S, D = x.shape
    PW = u.shape[2]
    G = wpg.shape[1]
    kern = functools.partial(_merge_kernel, tm=tm, D=D, G=G, alpha=alpha, E=E, K=K)
    row = lambda n: pl.BlockSpec((1, tm, n), lambda b, i: (b, i, 0))
    full = lambda a: pl.BlockSpec(a.shape, lambda b, i: (0,) * a.ndim)
    hb = tm // POOL_HALO
    halo = pl.BlockSpec((1, POOL_HALO, PW), lambda b, i: (b, jnp.maximum(i * hb - 1, 0), 0))
    return pl.pallas_call(
        kern,
        grid=(B, S // tm),
        in_specs=[row(D), pl.BlockSpec((1, yat.shape[1], tm), lambda b, i: (b, 0, i)), row(PW), halo,
                  row(g.shape[2]), pl.BlockSpec((1, N_MOD, D), lambda b, i: (b, 0, 0)),
                  full(wpg), full(psc), full(wba), full(wbp), full(wout), full(ln1g), full(ln1b),
                  full(wr), full(br)],
        out_specs=[row(D), pl.BlockSpec((1, tm * (D // LANES), LANES), lambda b, i: (b, i, 0)),
                   row(LANES), row(LANES)],
        out_shape=[jax.ShapeDtypeStruct((B, S, D), F32),
                   jax.ShapeDtypeStruct((B, S * (D // LANES), LANES), F32),
                   jax.ShapeDtypeStruct((B, S, LANES), I32), jax.ShapeDtypeStruct((B, S, LANES), F32)],
        scratch_shapes=[pltpu.VMEM((POOL_HALO + tm, PW), F32)],
        compiler_params=pltpu.CompilerParams(
            dimension_semantics=("parallel", "arbitrary"), vmem_limit_bytes=48 * 2 ** 20),
    )(x, yat, u, u, g, mod, wpg, psc, wba, wbp, wout, ln1g, ln1b, wr, br)


def _route_kernel(te_ref, rank_o, cnt_o, base_scr, *, tR, K):
    i = pl.program_id(0)

    @pl.when(i == 0)
    def _():
        base_scr[...] = jnp.zeros(base_scr.shape, F32)

    te = te_ref[...]
    lane = lax.broadcasted_iota(I32, te.shape, 1)
    hits = [lane == te[:, k:k + 1] for k in range(K)]
    onehot = hits[0].astype(F32)
    for k in range(1, K):
        onehot = onehot + hits[k].astype(F32)
    earlier = (lax.broadcasted_iota(I32, (tR, tR), 0) > lax.broadcasted_iota(I32, (tR, tR), 1)).astype(BF16)
    before = jnp.dot(earlier, onehot.astype(BF16), preferred_element_type=F32) + base_scr[...]
    rank = jnp.zeros(te.shape, I32)
    for k in range(K):
        rk = jnp.sum(jnp.where(hits[k], before, 0.0), axis=1, keepdims=True)
        rank = jnp.where(lane == k, rk.astype(I32), rank)
    rank_o[...] = rank
    base_scr[...] = base_scr[...] + jnp.sum(onehot, axis=0, keepdims=True)
    cnt_o[...] = base_scr[...].astype(I32)


def _route(te, *, tR, K):
    T = te.shape[0]
    return pl.pallas_call(
        functools.partial(_route_kernel, tR=tR, K=K),
        grid=(T // tR,),
        in_specs=[pl.BlockSpec((tR, LANES), lambda i: (i, 0))],
        out_specs=[pl.BlockSpec((tR, LANES), lambda i: (i, 0)), pl.BlockSpec((1, LANES), lambda i: (0, 0))],
        out_shape=[jax.ShapeDtypeStruct((T, LANES), I32), jax.ShapeDtypeStruct((1, LANES), I32)],
        scratch_shapes=[pltpu.VMEM((1, LANES), F32)],
        compiler_params=pltpu.CompilerParams(dimension_semantics=("arbitrary",)),
    )(te)


def _dispatch_kernel(pe_ref, nu_ref, pos_ref, h2_ref, xs_out, zero_scr, sem, zsem,
                     *, tD, K, sub, E, tmE, n_blocks):
    blk = tmE * sub

    @pl.when(pl.program_id(0) == 0)
    def _():
        zero_scr[...] = jnp.zeros(zero_scr.shape, zero_scr.dtype)

        def zero_block(b):
            return pltpu.make_async_copy(zero_scr, xs_out.at[pl.ds(pl.multiple_of(b * blk, blk), blk), :], zsem)

        def last_block(e):
            return jnp.maximum(pe_ref[e] // tmE - 1, 0)

        def start_last(e, carry):
            zero_block(last_block(e)).start()
            return carry

        def start_tail(b, carry):
            zero_block(b).start()
            return carry

        def wait_last(e, carry):
            zero_block(last_block(e)).wait()
            return carry

        def wait_tail(b, carry):
            zero_block(b).wait()
            return carry

        lax.fori_loop(0, E, start_last, 0)
        lax.fori_loop(nu_ref[0], n_blocks, start_tail, 0)
        lax.fori_loop(0, E, wait_last, 0)
        lax.fori_loop(nu_ref[0], n_blocks, wait_tail, 0)

    def issue(r, carry):
        src = h2_ref.at[pl.ds(pl.multiple_of(r * sub, sub), sub), :]
        for k in range(K):
            p = pl.multiple_of(pos_ref[0, 0, k * tD + r], sub)
            pltpu.async_copy(src, xs_out.at[pl.ds(p, sub), :], sem, priority=k % 2)
        return carry

    lax.fori_loop(0, tD, issue, 0, unroll=8)
    for k in range(K):
        pltpu.make_async_copy(h2_ref, xs_out.at[pl.ds(0, tD * sub), :], sem).wait()


def _dispatch(pad_end, n_used, pos, h2, n_blocks, *, tD, tmE, K, sub):
    T = h2.shape[0] // sub
    E = pad_end.shape[0]
    grid_spec = pltpu.PrefetchScalarGridSpec(
        num_scalar_prefetch=2,
        grid=(T // tD,),
        in_specs=[pl.BlockSpec((1, 1, K * tD), lambda i, pe, nu: (i, 0, 0), memory_space=pltpu.SMEM),
                  pl.BlockSpec((tD * sub, LANES), lambda i, pe, nu: (i, 0))],
        out_specs=pl.BlockSpec(memory_space=pl.ANY),
        scratch_shapes=[pltpu.VMEM((tmE * sub, LANES), h2.dtype), pltpu.SemaphoreType.DMA(()),
                        pltpu.SemaphoreType.DMA(())],
    )
    return pl.pallas_call(
        functools.partial(_dispatch_kernel, tD=tD, K=K, sub=sub, E=E, tmE=tmE, n_blocks=n_blocks),
        grid_spec=grid_spec,
        out_shape=jax.ShapeDtypeStruct((n_blocks * tmE * sub, LANES), h2.dtype),
        compiler_params=pltpu.CompilerParams(
            dimension_semantics=("arbitrary",), has_side_effects=True),
    )(pad_end, n_used, pos, h2)


def _expert_kernel(be_ref, nu_ref, x_ref, wgu_ref, bgu_ref, wd_ref, bd_ref, y_ref, wgu_bf, wd_bf,
                   *, F, tmE, sub):
    i = pl.program_id(0)
    new_expert = (i == 0) | (be_ref[i] != be_ref[jnp.maximum(i - 1, 0)])

    @pl.when(new_expert)
    def _():
        wgu_bf[...] = wgu_ref[0].astype(BF16)
        wd_bf[...] = wd_ref[0].astype(BF16)

    @pl.when(i < nu_ref[0])
    def _():
        x = _load_row_slabs(x_ref, tmE, sub).astype(BF16)
        gu = jnp.dot(x, wgu_bf[...], preferred_element_type=F32) + bgu_ref[0]
        glu = jnp.minimum(gu[:, :F], SWIGLU_LIMIT)
        lin = jnp.clip(gu[:, F:], -SWIGLU_LIMIT, SWIGLU_LIMIT)
        act = glu * jax.nn.sigmoid(SWIGLU_ALPHA * glu) * (lin + 1.0)
        y = jnp.dot(act.astype(BF16), wd_bf[...], preferred_element_type=F32) + bd_ref[0]
        _store_row_slabs(y_ref, y)

    @pl.when(i >= nu_ref[0])
    def _():
        y_ref[...] = jnp.zeros(y_ref.shape, F32)


def _experts(block_e, n_used, xs, wgu, bgu, wd, bd, *, tmE):
    E, D, F2 = wgu.shape
    F = F2 // 2
    sub = D // LANES
    n_blocks = xs.shape[0] // (tmE * sub)
    slab_block = pl.BlockSpec((tmE * sub, LANES), lambda i, be, nu: (i, 0))
    grid_spec = pltpu.PrefetchScalarGridSpec(
        num_scalar_prefetch=2,
        grid=(n_blocks,),
        in_specs=[slab_block,
                  pl.BlockSpec((1, D, F2), lambda i, be, nu: (be[i], 0, 0)),
                  pl.BlockSpec((1, 1, F2), lambda i, be, nu: (be[i], 0, 0)),
                  pl.BlockSpec((1, F, D), lambda i, be, nu: (be[i], 0, 0)),
                  pl.BlockSpec((1, 1, D), lambda i, be, nu: (be[i], 0, 0))],
        out_specs=slab_block,
        scratch_shapes=[pltpu.VMEM((D, F2), BF16), pltpu.VMEM((F, D), BF16)],
    )
    return pl.pallas_call(
        functools.partial(_expert_kernel, F=F, tmE=tmE, sub=sub),
        grid_spec=grid_spec,
        out_shape=jax.ShapeDtypeStruct(xs.shape, F32),
        compiler_params=pltpu.CompilerParams(
            dimension_semantics=("arbitrary",), vmem_limit_bytes=56 * 2 ** 20),
    )(block_e, n_used, xs, wgu, bgu, wd, bd)


def _combine_kernel(pos_ref, nxt_ref, y_hbm, x1_ref, tg_ref, mod_ref, g_ref, b_ref, o_ref, buf, sem,
                    *, tmC, K, alpha, nt, sub):
    i = pl.program_id(0)
    slot = lax.rem(i, 2)

    def fetch(p_ref, s):
        def issue(r, carry):
            dst_rows = pl.ds(pl.multiple_of(r * sub, sub), sub)
            for k in range(K):
                p = pl.multiple_of(p_ref[0, 0, k * tmC + r], sub)
                pltpu.async_copy(y_hbm.at[pl.ds(p, sub), :], buf.at[s, k, dst_rows, :], sem.at[s],
                                 priority=k % 2)
            return carry

        lax.fori_loop(0, tmC, issue, 0, unroll=8)

    @pl.when(i == 0)
    def _():
        fetch(pos_ref, slot)

    @pl.when(i + 1 < nt)
    def _():
        fetch(nxt_ref, 1 - slot)

    for k in range(K):
        pltpu.make_async_copy(y_hbm.at[pl.ds(0, tmC * sub), :], buf.at[slot, k], sem.at[slot]).wait()

    tg = tg_ref[...]
    y = tg[:, 0:1] * _load_row_slabs(buf.at[slot, 0], tmC, sub)
    for k in range(1, K):
        y = y + tg[:, k:k + 1] * _load_row_slabs(buf.at[slot, k], tmC, sub)
    mod = mod_ref[0]
    o_ref[...] = _layer_norm(alpha * x1_ref[...] + mod[5:6] * y, g_ref[...], b_ref[...])


def _combine(pos, y_sorted, x1, tg, mod, ln2g, ln2b, *, tmC, K, alpha, S):
    T, D = x1.shape
    nt = T // tmC
    per_seq = S // tmC
    sub = D // LANES
    return pl.pallas_call(
        functools.partial(_combine_kernel, tmC=tmC, K=K, alpha=alpha, nt=nt, sub=sub),
        grid=(nt,),
        in_specs=[pl.BlockSpec((1, 1, K * tmC), lambda i: (i, 0, 0), memory_space=pltpu.SMEM),
                  pl.BlockSpec((1, 1, K * tmC), lambda i: (jnp.minimum(i + 1, nt - 1), 0, 0),
                               memory_space=pltpu.SMEM),
                  pl.BlockSpec(memory_space=pl.ANY),
                  pl.BlockSpec((tmC, D), lambda i: (i, 0)),
                  pl.BlockSpec((tmC, LANES), lambda i: (i, 0)),
                  pl.BlockSpec((1, N_MOD, D), lambda i: (i // per_seq, 0, 0)),
                  pl.BlockSpec((1, D), lambda i: (0, 0)),
                  pl.BlockSpec((1, D), lambda i: (0, 0))],
        out_specs=pl.BlockSpec((tmC, D), lambda i: (i, 0)),
        out_shape=jax.ShapeDtypeStruct((T, D), F32),
        scratch_shapes=[pltpu.VMEM((2, K, tmC * sub, LANES), F32), pltpu.SemaphoreType.DMA((2,))],
        compiler_params=pltpu.CompilerParams(
            dimension_semantics=("arbitrary",), vmem_limit_bytes=48 * 2 ** 20),
    )(pos, pos, y_sorted, x1, tg, mod, ln2g, ln2b)


def _routing_tables(top_e, rank, counts, E, tmE, sub):
    T, K = top_e.shape
    padded = (counts + tmE - 1) // tmE * tmE
    pad_end = jnp.cumsum(padded)
    pad_start = pad_end - padded
    n_blocks = -(-(T * K) // tmE) + E
    block_start = jnp.arange(n_blocks, dtype=I32) * tmE
    block_e = jnp.minimum(jnp.sum((pad_end[None, :] <= block_start[:, None]).astype(I32), axis=1), E - 1)
    n_used = (pad_end[-1] // tmE).astype(I32).reshape(1)
    pos = (jnp.take(pad_start, top_e) + rank).astype(I32) * sub
    return block_e.astype(I32), n_used, pad_end.astype(I32), pos, n_blocks


def _tile_major(pos, tile):
    T, K = pos.shape
    return pos.reshape(T // tile, tile, K).transpose(0, 2, 1).reshape(T // tile, 1, K * tile)


def _layer(x, mod, w_in, kv_norm_g, w_uk, w_uv, w_pool_group, pool_scale, w_branch_attn, w_branch_pool,
           w_out, ln1_g, ln1_b, w_router, b_router, w_gate_up, b_gate_up, w_down, b_down, ln2_g, ln2_b,
           *, alpha):
    B, S, D = x.shape
    R, H, dh = w_uk.shape
    AW = H * dh
    PW = pool_scale.shape[0]
    E = w_router.shape[1]
    K = TOP_K_EXPERTS
    n_idx = w_in.shape[1] - (AW + R + PW + 2 * D)
    di = dh
    Hi = (n_idx - di) // (di + 1)
    sizes = (AW, R, Hi * di, di, Hi, PW, 2 * D)
    offs = [0]
    for s in sizes:
        offs.append(offs[-1] + s)
    ws = [w_in[:, offs[k]:offs[k + 1]].astype(BF16) for k in range(len(sizes))]
    ws[4] = jnp.pad(ws[4], ((0, 0), (0, LANES - Hi)))

    wukt = w_uk.transpose(1, 0, 2).astype(BF16)
    wuvt = w_uv.transpose(1, 2, 0).astype(BF16)
    qlt, ckva, ckvt, qit, ki, wit, u, g = _inproj(
        x, mod, ws, kv_norm_g.reshape(1, R), wukt, q_scale=dh ** -0.5 * LOG2E, qi_scale=di ** -0.5,
        wi_scale=Hi ** -0.5, Hi=Hi, tm=min(S, 512))

    topk = min(TOPK_MAX, S // 4)
    y_attn_t = _attention(qit, wit, ki, qlt, ckva, ckvt, wuvt, tq=min(S, 512), topk=topk)

    wr = jnp.pad(w_router, ((0, 0), (0, LANES - E)))
    wr_hi = wr.astype(BF16)
    wr = jnp.stack([wr_hi, (wr - wr_hi.astype(F32)).astype(BF16)])
    br = jnp.pad(b_router, (0, LANES - E)).reshape(1, LANES)
    x1, h2, te, tg = _merge(x, y_attn_t, u, g, mod, w_pool_group.astype(BF16), pool_scale.reshape(1, PW),
                            w_branch_attn.astype(BF16), w_branch_pool.astype(BF16), w_out.astype(BF16),
                            ln1_g.reshape(1, D), ln1_b.reshape(1, D), wr, br,
                            tm=min(S, 256), alpha=alpha, E=E, K=K)

    T = B * S
    tmE = 512
    tmC = min(S, 256)
    tD = min(T, 1024)
    sub = D // LANES
    te = te.reshape(T, LANES)
    rank, counts = _route(te, tR=min(T, 512), K=K)
    block_e, n_used, pad_end, pos, n_blocks = _routing_tables(te[:, :K], rank[:, :K], counts[0, :E], E, tmE,
                                                              sub)
    xs = _dispatch(pad_end, n_used, _tile_major(pos, tD), h2.reshape(T * sub, LANES), n_blocks,
                   tD=tD, tmE=tmE, K=K, sub=sub)
    y_sorted = _experts(block_e, n_used, xs, w_gate_up, b_gate_up.reshape(E, 1, -1),
                        w_down, b_down.reshape(E, 1, D), tmE=tmE)
    out = _combine(_tile_major(pos, tmC), y_sorted, x1.reshape(T, D), tg.reshape(T, LANES), mod,
                   ln2_g.reshape(1, D), ln2_b.reshape(1, D), tmC=tmC, K=K, alpha=alpha, S=S)
    return out.reshape(B, S, D)


def kernel(x, c, w_ada, b_ada, w_in, kv_norm_g, w_uk, w_uv, w_pool_group, pool_scale, w_branch_attn,
           w_branch_pool, w_out, ln1_g, ln1_b, w_router, b_router, w_gate_up, b_gate_up, w_down, b_down,
           ln2_g, ln2_b):
    B, S, D = x.shape
    depth = w_ada.shape[0]
    alpha = (2.0 * depth) ** 0.25
    for l in range(depth):
        mod = _ada(c, w_ada[l], b_ada[l]).reshape(B, N_MOD, D)
        x = _layer(x, mod, w_in[l], kv_norm_g[l], w_uk[l], w_uv[l], w_pool_group[l], pool_scale[l],
                   w_branch_attn[l], w_branch_pool[l], w_out[l], ln1_g[l], ln1_b[l], w_router[l],
                   b_router[l], w_gate_up[l], b_gate_up[l], w_down[l], b_down[l], ln2_g[l], ln2_b[l],
                   alpha=alpha)
    return x
```

```python
import functools

import jax
import jax.numpy as jnp
from jax import lax
from jax.experimental import pallas as pl
from jax.experimental.pallas import tpu as pltpu

F32 = jnp.float32
BF16 = jnp.bfloat16
I32 = jnp.int32
HIGHEST = lax.Precision.HIGHEST

LN_EPS = 1e-5
TOPK_MAX = 256
TOP_K_EXPERTS = 4
POOL_WINDOWS = (2, 4, 8, 16)
SWIGLU_LIMIT = 7.0
SWIGLU_ALPHA = 1.702
N_MOD = 6
INT_MIN = -(2 ** 31)
NEG_BIG = -1e30
LANES = 128
SUBLANES = 8
SORT_GROUP = 16
POOL_HALO = 16
ONES_ROWS = 16
LOG2E = 1.4426950408889634
ATTN_KEY_CHUNK = 256
ATTN_HEAD_UNROLL = 4


def _store_row_slabs(ref, x):
    n, D = x.shape
    sub = D // LANES
    for s in range(sub):
        ref[pl.ds(s, n, stride=sub), :] = x[:, s * LANES:(s + 1) * LANES]


def _load_row_slabs(ref, n, sub):
    return jnp.concatenate([ref[pl.ds(s, n, stride=sub), :] for s in range(sub)], axis=1)


def _layer_norm(z, g, b):
    mu = jnp.mean(z, axis=-1, keepdims=True)
    zc = z - mu
    var = jnp.mean(zc * zc, axis=-1, keepdims=True)
    return zc * lax.rsqrt(var + LN_EPS) * g + b


def _ada_kernel(c_ref, w_ref, b_ref, o_ref):
    c = c_ref[...]
    cond = c * jax.nn.sigmoid(c)
    o_ref[...] = jnp.dot(cond, w_ref[...], precision=HIGHEST, preferred_element_type=F32) + b_ref[...]


def _ada(c, w_ada, b_ada):
    B, D = c.shape
    N = w_ada.shape[1]
    tn = D
    return pl.pallas_call(
        _ada_kernel,
        grid=(N // tn,),
        in_specs=[pl.BlockSpec((B, D), lambda j: (0, 0)),
                  pl.BlockSpec((D, tn), lambda j: (0, j)),
                  pl.BlockSpec((1, tn), lambda j: (0, j))],
        out_specs=pl.BlockSpec((B, tn), lambda j: (0, j)),
        out_shape=jax.ShapeDtypeStruct((B, N), F32),
    )(c, w_ada, b_ada.reshape(1, N))


def _inproj_kernel(x_ref, mod_ref, wq, wckv, wqi, wki, wwi, wu, wg, kvg_ref, wukt_ref,
                   qlt_o, ckva_o, ckvt_o, qit_o, ki_o, wit_o, u_o, g_o,
                   *, tm, H, dh, Hi, q_scale, qi_scale, wi_scale):
    i = pl.program_id(1)
    x = x_ref[0]
    mod = mod_ref[0]
    h = (x * (1.0 + mod[1:2]) + mod[0:1]).astype(BF16)

    def proj(w):
        return jnp.dot(h, w[...], preferred_element_type=F32)

    qt = proj(wq).T.astype(BF16)
    for hh in range(H):
        qlt = jnp.dot(wukt_ref[hh], qt[hh * dh:(hh + 1) * dh, :], preferred_element_type=F32) * q_scale
        qlt_o[0, hh] = qlt.astype(BF16)

    ckv = proj(wckv)
    ckv = ckv * lax.rsqrt(jnp.mean(ckv * ckv, axis=-1, keepdims=True) + LN_EPS) * kvg_ref[...]
    ckvt_o[0, 0:ckv.shape[1], :] = ckv.T.astype(BF16)
    ckvt_o[0, ckv.shape[1]:, :] = jnp.ones((ONES_ROWS, tm), BF16)
    R = ckv.shape[1]
    pos = i * tm + lax.broadcasted_iota(I32, (tm, R), 0)
    lane = lax.broadcasted_iota(I32, (tm, R), 1)
    pos_hi = ((pos >> 8) << 8).astype(F32)
    pos_lo = (pos & 255).astype(F32)
    extra = jnp.where((lane == 0) | (lane == 2), pos_hi, jnp.where((lane == 1) | (lane == 3), pos_lo, 0.0))
    ckva_o[0, :, 0:R] = ckv.astype(BF16)
    ckva_o[0, :, R:2 * R] = extra.astype(BF16)

    qit_o[0] = (proj(wqi) * qi_scale).T.astype(BF16)
    ki_o[0] = proj(wki).astype(BF16)
    wit_o[0] = (proj(wwi) * wi_scale).T[0:Hi, :]
    u_o[0] = proj(wu)
    g_o[0] = proj(wg).astype(BF16)


def _inproj(x, mod, ws, kv_norm_g, wukt, *, q_scale, qi_scale, wi_scale, Hi, tm):
    B, S, D = x.shape
    wq, wckv, wqi, wki, wwi, wu, wg = ws
    H, R, dh = wukt.shape
    row = lambda n: pl.BlockSpec((1, tm, n), lambda b, i: (b, i, 0))
    col = lambda n: pl.BlockSpec((1, n, tm), lambda b, i: (b, 0, i))
    full = lambda a: pl.BlockSpec(a.shape, lambda b, i: (0,) * a.ndim)
    out_specs = [pl.BlockSpec((1, H, R, tm), lambda b, i: (b, 0, 0, i)), row(2 * R), col(R + ONES_ROWS),
                 col(wqi.shape[1]),
                 row(wki.shape[1]), col(Hi), row(wu.shape[1]), row(wg.shape[1])]
    out_shape = [jax.ShapeDtypeStruct((B, H, R, S), BF16), jax.ShapeDtypeStruct((B, S, 2 * R), BF16),
                 jax.ShapeDtypeStruct((B, R + ONES_ROWS, S), BF16),
                 jax.ShapeDtypeStruct((B, wqi.shape[1], S), BF16),
                 jax.ShapeDtypeStruct((B, S, wki.shape[1]), BF16), jax.ShapeDtypeStruct((B, Hi, S), F32),
                 jax.ShapeDtypeStruct((B, S, wu.shape[1]), F32), jax.ShapeDtypeStruct((B, S, wg.shape[1]), BF16)]
    return pl.pallas_call(
        functools.partial(_inproj_kernel, tm=tm, H=H, dh=dh, Hi=Hi, q_scale=q_scale, qi_scale=qi_scale,
                          wi_scale=wi_scale),
        grid=(B, S // tm),
        in_specs=[row(D), pl.BlockSpec((1, N_MOD, D), lambda b, i: (b, 0, 0))]
                 + [full(w) for w in ws] + [full(kv_norm_g), full(wukt)],
        out_specs=out_specs,
        out_shape=out_shape,
        compiler_params=pltpu.CompilerParams(
            dimension_semantics=("parallel", "parallel"), vmem_limit_bytes=48 * 2 ** 20),
    )(x, mod, *ws, kv_norm_g, wukt)


def _key_to_f32(t):
    return pltpu.bitcast(jnp.where(t >= 0, t, t ^ 0x7FFFFFFF), F32)


def _sort_network(n):
    pairs, p = [], 1
    while p < n:
        k = p
        while k >= 1:
            for j in range(k % p, n - k, 2 * k):
                for i in range(min(k, n - j - k)):
                    if (i + j) // (2 * p) == (i + j + k) // (2 * p):
                        pairs.append((i + j, i + j + k))
            k //= 2
        p *= 2
    return pairs


def _count_ge_sorted(blocks, cand):
    n = len(blocks)

    def pick(level, bits):
        if len(level) == 1:
            return level[0]
        half = len(level) // 2
        return jnp.where(bits[0], pick(level[half:], bits[1:]), pick(level[:half], bits[1:]))

    total = jnp.where(blocks[n - 1] >= cand, 1.0, 0.0)
    bits = []
    step = n // 2
    while step >= 1:
        level = [blocks[c + step - 1] for c in range(0, n, 2 * step)]
        b = pick(level, bits) >= cand
        bits.append(b)
        total = total + jnp.where(b, float(step), 0.0)
        step //= 2
    return total


def _attn_body(qit_ref, wit_ref, ki_ref, qlt_ref, aug_ref, ckva_ref, ckvt_ref, wuvt_ref, yt_ref,
               score_scr, bias_scr, lg_scr, o_scr, sort_scr, *, W, tq, S, H, dh, Hi, di, topk, cw):
    j = pl.program_id(1)
    qpos = j * tq + lax.broadcasted_iota(I32, (1, tq), 1)
    score = score_scr.at[0:W, :]

    wit = wit_ref[0]
    for c in range(W // cw):
        kic = ki_ref[0, c * cw:(c + 1) * cw, :]
        acc = jnp.zeros((cw, tq), F32)
        for h in range(Hi):
            d = jnp.dot(kic, qit_ref[0, h * di:(h + 1) * di, :], preferred_element_type=F32)
            acc = acc + wit[h:h + 1, :] * jnp.maximum(d, 0.0)
        kpos_c = c * cw + lax.broadcasted_iota(I32, (cw, tq), 0)
        score_scr[c * cw:(c + 1) * cw, :] = jnp.where(kpos_c <= qpos, acc, -jnp.inf)

    def count(hit):
        return jnp.sum(jnp.where(hit, 1.0, 0.0), axis=0, keepdims=True)

    Wb = W // SORT_GROUP
    net = _sort_network(SORT_GROUP)

    def column(r, lg):
        return [(slice(g * Wb + r * SUBLANES, g * Wb + (r + 1) * SUBLANES), slice(lg * LANES, (lg + 1) * LANES))
                for g in range(SORT_GROUP)]

    for r in range(Wb // SUBLANES):
        for lg in range(tq // LANES):
            v = [score_scr[idx] for idx in column(r, lg)]
            for a, b in net:
                v[a], v[b] = jnp.maximum(v[a], v[b]), jnp.minimum(v[a], v[b])
            for idx, val in zip(column(r, lg), v):
                sort_scr[idx] = val

    def count_ge(cand_f):
        parts = []
        for lg in range(tq // LANES):
            c = cand_f[:, lg * LANES:(lg + 1) * LANES]
            acc = None
            for r in range(Wb // SUBLANES):
                n = _count_ge_sorted([sort_scr[idx] for idx in column(r, lg)], c)
                acc = n if acc is None else acc + n
            parts.append(jnp.sum(acc, axis=0, keepdims=True))
        return jnp.concatenate(parts, axis=1)

    def thr_step(i, t):
        cand = t + lax.shift_left(jnp.int32(1), 31 - i)
        cnt = count_ge(_key_to_f32(cand))
        return jnp.where(cnt >= float(topk), cand, t)

    t = lax.fori_loop(0, 32, thr_step, jnp.full((1, tq), INT_MIN, I32))
    few = t == INT_MIN
    thr_f = _key_to_f32(t)
    up_f = _key_to_f32(t + 1)
    s = score[...]
    n_gt = count(s >= up_f)
    n_ge = count(s >= thr_f)
    need = float(topk) - n_gt

    kpos = lax.broadcasted_iota(I32, (W, tq), 0)
    nbits = max(1, (S - 1).bit_length())

    def tie_search():
        def step(i, jm):
            cand = jm + lax.shift_right_logical(jnp.int32(2 ** (nbits - 1)), i)
            sc = score[...]
            f = count((sc >= thr_f) & jnp.logical_not(sc >= up_f) & (kpos < cand))
            return jnp.where(f < need, cand, jm)
        return lax.fori_loop(0, nbits, step, jnp.zeros((1, tq), I32))

    any_tie = jnp.max(jnp.where(few, 0.0, n_ge - n_gt - need)) > 0.0
    jmax = lax.cond(any_tie, tie_search, lambda: jnp.full((1, tq), S, I32))
    sel = (few | (s >= up_f) | ((s >= thr_f) & (kpos <= jmax))) & (kpos <= qpos)
    bias_scr[0:W, :] = jnp.where(sel, 0.0, NEG_BIG)

    R = qlt_ref.shape[2]
    ck = min(ATTN_KEY_CHUNK, W)

    def head(h, carry):
        qa = jnp.concatenate([qlt_ref[0, h], aug_ref[h]], axis=0)
        m = None
        for c in range(W // ck):
            keys = slice(c * ck, (c + 1) * ck)
            lg = jnp.dot(ckva_ref[0, keys, :], qa, preferred_element_type=F32) + bias_scr[keys, :]
            lg_scr[keys, :] = lg
            mc = jnp.max(lg, axis=0, keepdims=True)
            m = mc if m is None else jnp.maximum(m, mc)
        acc = jnp.zeros((R + ONES_ROWS, tq), F32)
        for c in range(W // ck):
            keys = slice(c * ck, (c + 1) * ck)
            p = jnp.exp2(lg_scr[keys, :] - m)
            acc = acc + jnp.dot(ckvt_ref[0, :, keys], p.astype(BF16), preferred_element_type=F32)
        o_scr[h] = (acc[0:R] / acc[R:R + 1]).astype(BF16)
        return carry

    lax.fori_loop(0, H, head, 0, unroll=ATTN_HEAD_UNROLL)
    for h in range(H):
        yt = jnp.dot(wuvt_ref[h], o_scr[h], preferred_element_type=F32)
        yt_ref[0, h * dh:(h + 1) * dh, :] = yt.astype(BF16)


def _attn_kernel(*refs, tq, cw, S, **kw):
    j = pl.program_id(1)
    nc = ((j + 1) * tq + cw - 1) // cw
    for w in range(1, S // cw + 1):
        @pl.when(nc == w)
        def _():
            _attn_body(*refs, W=w * cw, tq=tq, S=S, cw=cw, **kw)


def _attention(qit, wit, ki, qlt, ckva, ckvt, wuvt, *, tq, topk):
    B, H, R, S = qlt.shape
    dh = wuvt.shape[1]
    di = ki.shape[2]
    Hi = wit.shape[1]
    slopes = jnp.asarray([LOG2E * 2.0 ** (-8.0 * (h + 1) / H) for h in range(H)], F32)
    s_hi = slopes.astype(BF16)
    s_lo = (slopes - s_hi.astype(F32)).astype(BF16)
    aug = jnp.zeros((H, R, tq), BF16)
    aug = aug.at[:, 0:2, :].set(s_hi[:, None, None]).at[:, 2:4, :].set(s_lo[:, None, None])
    kern = functools.partial(_attn_kernel, tq=tq, S=S, H=H, dh=dh, Hi=Hi, di=di, topk=topk, cw=min(S, 512))
    col = lambda n: pl.BlockSpec((1, n, tq), lambda b, j: (b, 0, j))
    full = lambda a: pl.BlockSpec(a.shape, lambda b, j: (0,) * a.ndim)
    return pl.pallas_call(
        kern,
        grid=(B, S // tq),
        in_specs=[col(qit.shape[1]), col(Hi), pl.BlockSpec((1, S, di), lambda b, j: (b, 0, 0)),
                  pl.BlockSpec((1, H, R, tq), lambda b, j: (b, 0, 0, j)), full(aug),
                  pl.BlockSpec((1, S, 2 * R), lambda b, j: (b, 0, 0)),
                  pl.BlockSpec((1, R + ONES_ROWS, S), lambda b, j: (b, 0, 0)), full(wuvt)],
        out_specs=col(H * dh),
        out_shape=jax.ShapeDtypeStruct((B, H * dh, S), BF16),
        scratch_shapes=[pltpu.VMEM((S, tq), F32), pltpu.VMEM((S, tq), F32), pltpu.VMEM((S, tq), F32),
                        pltpu.VMEM((H, R, tq), BF16), pltpu.VMEM((S, tq), F32)],
        compiler_params=pltpu.CompilerParams(
            dimension_semantics=("parallel", "parallel"), vmem_limit_bytes=48 * 2 ** 20),
    )(qit, wit, ki, qlt, aug, ckva, ckvt, wuvt)


def _merge_kernel(x_ref, yat_ref, u_ref, halo_ref, g_ref, mod_ref, wpg_ref, psc_ref, wba_ref, wbp_ref,
                  wout_ref, ln1g_ref, ln1b_ref, wr_ref, br_ref,
                  x1_o, h2_o, te_o, tg_o, ext_scr, *, tm, D, G, alpha, E, K):
    i = pl.program_id(1)

    @pl.when(i == 0)
    def _():
        ext_scr[0:POOL_HALO, :] = jnp.zeros((POOL_HALO, ext_scr.shape[1]), F32)

    @pl.when(i > 0)
    def _():
        ext_scr[0:POOL_HALO, :] = halo_ref[0]

    ext_scr[POOL_HALO:POOL_HALO + tm, :] = u_ref[0]

    pos = i * tm + lax.broadcasted_iota(I32, (tm, 1), 0)
    ys = []
    for g, w in enumerate(POOL_WINDOWS):
        cols = slice(g * G, (g + 1) * G)
        cur = ext_scr[POOL_HALO:POOL_HALO + tm, cols]
        acc = cur
        for back in range(1, w):
            acc = acc + ext_scr[POOL_HALO - back:POOL_HALO - back + tm, cols]
        cnt = jnp.minimum(pos + 1, w).astype(F32)
        pooled = acc / cnt - cur
        ys.append(jnp.dot(pooled.astype(BF16), wpg_ref[g], preferred_element_type=F32))
    y_pool = jnp.concatenate(ys, axis=1) * psc_ref[...]

    gates = g_ref[0]
    ya = yat_ref[0].astype(F32).T.astype(BF16)
    a = jnp.dot(ya, wba_ref[...], preferred_element_type=F32)
    p = jnp.dot(y_pool.astype(BF16), wbp_ref[...], preferred_element_type=F32)
    mix = jax.nn.sigmoid(gates[:, :D].astype(F32)) * a + jax.nn.sigmoid(gates[:, D:].astype(F32)) * p
    o = jnp.dot(mix.astype(BF16), wout_ref[...], preferred_element_type=F32)
    mod = mod_ref[0]
    x1 = _layer_norm(alpha * x_ref[0] + mod[2:3] * o, ln1g_ref[...], ln1b_ref[...])
    x1_o[0] = x1
    h2 = x1 * (1.0 + mod[4:5]) + mod[3:4]
    _store_row_slabs(h2_o.at[0], h2)

    h2_hi = h2.astype(BF16)
    h2_lo = (h2 - h2_hi.astype(F32)).astype(BF16)
    logits = (jnp.dot(h2_hi, wr_ref[0], preferred_element_type=F32)
              + jnp.dot(h2_hi, wr_ref[1], preferred_element_type=F32)
              + jnp.dot(h2_lo, wr_ref[0], preferred_element_type=F32)) + br_ref[...]
    lane = lax.broadcasted_iota(I32, logits.shape, 1)
    logits = jnp.where(lane < E, logits, -jnp.inf)
    te = jnp.zeros(logits.shape, I32)
    vals = []
    for k in range(K):
        m = jnp.max(logits, axis=1, keepdims=True)
        idx = jnp.min(jnp.where(logits == m, lane, LANES), axis=1, keepdims=True)
        te = jnp.where(lane == k, idx, te)
        vals.append(m)
        logits = jnp.where(lane == idx, -jnp.inf, logits)
    ex = [jnp.exp(v - vals[0]) for v in vals]
    den = ex[0]
    for e in ex[1:]:
        den = den + e
    tg = jnp.zeros(logits.shape, F32)
    for k in range(K):
        tg = jnp.where(lane == k, ex[k] / den, tg)
    te_o[0] = te
    tg_o[0] = tg


def _merge(x, yat, u, g, mod, wpg, psc, wba, wbp, wout, ln1g, ln1b, wr, br, *, tm, alpha, E, K):
    B, S, D = x.shape
    PW = u.shape[2]
    G = wpg.shape[1]
    kern = functools.partial(_merge_kernel, tm=tm, D=D, G=G, alpha=alpha, E=E, K=K)
    row = lambda n: pl.BlockSpec((1, tm, n), lambda b, i: (b, i, 0))
    full = lambda a: pl.BlockSpec(a.shape, lambda b, i: (0,) * a.ndim)
    hb = tm // POOL_HALO
    halo = pl.BlockSpec((1, POOL_HALO, PW), lambda b, i: (b, jnp.maximum(i * hb - 1, 0), 0))
    return pl.pallas_call(
        kern,
        grid=(B, S // tm),
        in_specs=[row(D), pl.BlockSpec((1, yat.shape[1], tm), lambda b, i: (b, 0, i)), row(PW), halo,
                  row(g.shape[2]), pl.BlockSpec((1, N_MOD, D), lambda b, i: (b, 0, 0)),
                  full(wpg), full(psc), full(wba), full(wbp), full(wout), full(ln1g), full(ln1b),
                  full(wr), full(br)],
        out_specs=[row(D), pl.BlockSpec((1, tm * (D // LANES), LANES), lambda b, i: (b, i, 0)),
                   row(LANES), row(LANES)],
        out_shape=[jax.ShapeDtypeStruct((B, S, D), F32),
                   jax.ShapeDtypeStruct((B, S * (D // LANES), LANES), F32),
                   jax.ShapeDtypeStruct((B, S, LANES), I32), jax.ShapeDtypeStruct((B, S, LANES), F32)],
        scratch_shapes=[pltpu.VMEM((POOL_HALO + tm, PW), F32)],
        compiler_params=pltpu.CompilerParams(
            dimension_semantics=("parallel", "arbitrary"), vmem_limit_bytes=48 * 2 ** 20),
    )(x, yat, u, u, g, mod, wpg, psc, wba, wbp, wout, ln1g, ln1b, wr, br)


def _route_kernel(te_ref, rank_o, cnt_o, base_scr, *, tR, K):
    i = pl.program_id(0)

    @pl.when(i == 0)
    def _():
        base_scr[...] = jnp.zeros(base_scr.shape, F32)

    te = te_ref[...]
    lane = lax.broadcasted_iota(I32, te.shape, 1)
    hits = [lane == te[:, k:k + 1] for k in range(K)]
    onehot = hits[0].astype(F32)
    for k in range(1, K):
        onehot = onehot + hits[k].astype(F32)
    earlier = (lax.broadcasted_iota(I32, (tR, tR), 0) > lax.broadcasted_iota(I32, (tR, tR), 1)).astype(BF16)
    before = jnp.dot(earlier, onehot.astype(BF16), preferred_element_type=F32) + base_scr[...]
    rank = jnp.zeros(te.shape, I32)
    for k in range(K):
        rk = jnp.sum(jnp.where(hits[k], before, 0.0), axis=1, keepdims=True)
        rank = jnp.where(lane == k, rk.astype(I32), rank)
    rank_o[...] = rank
    base_scr[...] = base_scr[...] + jnp.sum(onehot, axis=0, keepdims=True)
    cnt_o[...] = base_scr[...].astype(I32)


def _route(te, *, tR, K):
    T = te.shape[0]
    return pl.pallas_call(
        functools.partial(_route_kernel, tR=tR, K=K),
        grid=(T // tR,),
        in_specs=[pl.BlockSpec((tR, LANES), lambda i: (i, 0))],
        out_specs=[pl.BlockSpec((tR, LANES), lambda i: (i, 0)), pl.BlockSpec((1, LANES), lambda i: (0, 0))],
        out_shape=[jax.ShapeDtypeStruct((T, LANES), I32), jax.ShapeDtypeStruct((1, LANES), I32)],
        scratch_shapes=[pltpu.VMEM((1, LANES), F32)],
        compiler_params=pltpu.CompilerParams(dimension_semantics=("arbitrary",)),
    )(te)


def _dispatch_kernel(pe_ref, nu_ref, pos_ref, h2_ref, xs_out, zero_scr, sem, zsem,
                     *, tD, K, sub, E, tmE, n_blocks):
    blk = tmE * sub

    @pl.when(pl.program_id(0) == 0)
    def _():
        zero_scr[...] = jnp.zeros(zero_scr.shape, zero_scr.dtype)

        def zero_block(b):
            return pltpu.make_async_copy(zero_scr, xs_out.at[pl.ds(pl.multiple_of(b * blk, blk), blk), :], zsem)

        def last_block(e):
            return jnp.maximum(pe_ref[e] // tmE - 1, 0)

        def start_last(e, carry):
            zero_block(last_block(e)).start()
            return carry

        def start_tail(b, carry):
            zero_block(b).start()
            return carry

        def wait_last(e, carry):
            zero_block(last_block(e)).wait()
            return carry

        def wait_tail(b, carry):
            zero_block(b).wait()
            return carry

        lax.fori_loop(0, E, start_last, 0)
        lax.fori_loop(nu_ref[0], n_blocks, start_tail, 0)
        lax.fori_loop(0, E, wait_last, 0)
        lax.fori_loop(nu_ref[0], n_blocks, wait_tail, 0)

    def issue(r, carry):
        src = h2_ref.at[pl.ds(pl.multiple_of(r * sub, sub), sub), :]
        for k in range(K):
            p = pl.multiple_of(pos_ref[0, 0, k * tD + r], sub)
            pltpu.async_copy(src, xs_out.at[pl.ds(p, sub), :], sem, priority=k % 2)
        return carry

    lax.fori_loop(0, tD, issue, 0, unroll=8)
    for k in range(K):
        pltpu.make_async_copy(h2_ref, xs_out.at[pl.ds(0, tD * sub), :], sem).wait()


def _dispatch(pad_end, n_used, pos, h2, n_blocks, *, tD, tmE, K, sub):
    T = h2.shape[0] // sub
    E = pad_end.shape[0]
    grid_spec = pltpu.PrefetchScalarGridSpec(
        num_scalar_prefetch=2,
        grid=(T // tD,),
        in_specs=[pl.BlockSpec((1, 1, K * tD), lambda i, pe, nu: (i, 0, 0), memory_space=pltpu.SMEM),
                  pl.BlockSpec((tD * sub, LANES), lambda i, pe, nu: (i, 0))],
        out_specs=pl.BlockSpec(memory_space=pl.ANY),
        scratch_shapes=[pltpu.VMEM((tmE * sub, LANES), h2.dtype), pltpu.SemaphoreType.DMA(()),
                        pltpu.SemaphoreType.DMA(())],
    )
    return pl.pallas_call(
        functools.partial(_dispatch_kernel, tD=tD, K=K, sub=sub, E=E, tmE=tmE, n_blocks=n_blocks),
        grid_spec=grid_spec,
        out_shape=jax.ShapeDtypeStruct((n_blocks * tmE * sub, LANES), h2.dtype),
        compiler_params=pltpu.CompilerParams(
            dimension_semantics=("arbitrary",), has_side_effects=True),
    )(pad_end, n_used, pos, h2)


def _expert_kernel(be_ref, nu_ref, x_ref, wgu_ref, bgu_ref, wd_ref, bd_ref, y_ref, wgu_bf, wd_bf,
                   *, F, tmE, sub):
    i = pl.program_id(0)
    new_expert = (i == 0) | (be_ref[i] != be_ref[jnp.maximum(i - 1, 0)])

    @pl.when(new_expert)
    def _():
        wgu_bf[...] = wgu_ref[0].astype(BF16)
        wd_bf[...] = wd_ref[0].astype(BF16)

    @pl.when(i < nu_ref[0])
    def _():
        x = _load_row_slabs(x_ref, tmE, sub).astype(BF16)
        gu = jnp.dot(x, wgu_bf[...], preferred_element_type=F32) + bgu_ref[0]
        glu = jnp.minimum(gu[:, :F], SWIGLU_LIMIT)
        lin = jnp.clip(gu[:, F:], -SWIGLU_LIMIT, SWIGLU_LIMIT)
        act = glu * jax.nn.sigmoid(SWIGLU_ALPHA * glu) * (lin + 1.0)
        y = jnp.dot(act.astype(BF16), wd_bf[...], preferred_element_type=F32) + bd_ref[0]
        _store_row_slabs(y_ref, y)

    @pl.when(i >= nu_ref[0])
    def _():
        y_ref[...] = jnp.zeros(y_ref.shape, F32)


def _experts(block_e, n_used, xs, wgu, bgu, wd, bd, *, tmE):
    E, D, F2 = wgu.shape
    F = F2 // 2
    sub = D // LANES
    n_blocks = xs.shape[0] // (tmE * sub)
    slab_block = pl.BlockSpec((tmE * sub, LANES), lambda i, be, nu: (i, 0))
    grid_spec = pltpu.PrefetchScalarGridSpec(
        num_scalar_prefetch=2,
        grid=(n_blocks,),
        in_specs=[slab_block,
                  pl.BlockSpec((1, D, F2), lambda i, be, nu: (be[i], 0, 0)),
                  pl.BlockSpec((1, 1, F2), lambda i, be, nu: (be[i], 0, 0)),
                  pl.BlockSpec((1, F, D), lambda i, be, nu: (be[i], 0, 0)),
                  pl.BlockSpec((1, 1, D), lambda i, be, nu: (be[i], 0, 0))],
        out_specs=slab_block,
        scratch_shapes=[pltpu.VMEM((D, F2), BF16), pltpu.VMEM((F, D), BF16)],
    )
    return pl.pallas_call(
        functools.partial(_expert_kernel, F=F, tmE=tmE, sub=sub),
        grid_spec=grid_spec,
        out_shape=jax.ShapeDtypeStruct(xs.shape, F32),
        compiler_params=pltpu.CompilerParams(
            dimension_semantics=("arbitrary",), vmem_limit_bytes=56 * 2 ** 20),
    )(block_e, n_used, xs, wgu, bgu, wd, bd)


def _combine_kernel(pos_ref, nxt_ref, y_hbm, x1_ref, tg_ref, mod_ref, g_ref, b_ref, o_ref, buf, sem,
                    *, tmC, K, alpha, nt, sub):
    i = pl.program_id(0)
    slot = lax.rem(i, 2)

    def fetch(p_ref, s):
        def issue(r, carry):
            dst_rows = pl.ds(pl.multiple_of(r * sub, sub), sub)
            for k in range(K):
                p = pl.multiple_of(p_ref[0, 0, k * tmC + r], sub)
                pltpu.async_copy(y_hbm.at[pl.ds(p, sub), :], buf.at[s, k, dst_rows, :], sem.at[s],
                                 priority=k % 2)
            return carry

        lax.fori_loop(0, tmC, issue, 0, unroll=8)

    @pl.when(i == 0)
    def _():
        fetch(pos_ref, slot)

    @pl.when(i + 1 < nt)
    def _():
        fetch(nxt_ref, 1 - slot)

    for k in range(K):
        pltpu.make_async_copy(y_hbm.at[pl.ds(0, tmC * sub), :], buf.at[slot, k], sem.at[slot]).wait()

    tg = tg_ref[...]
    y = tg[:, 0:1] * _load_row_slabs(buf.at[slot, 0], tmC, sub)
    for k in range(1, K):
        y = y + tg[:, k:k + 1] * _load_row_slabs(buf.at[slot, k], tmC, sub)
    mod = mod_ref[0]
    o_ref[...] = _layer_norm(alpha * x1_ref[...] + mod[5:6] * y, g_ref[...], b_ref[...])


def _combine(pos, y_sorted, x1, tg, mod, ln2g, ln2b, *, tmC, K, alpha, S):
    T, D = x1.shape
    nt = T // tmC
    per_seq = S // tmC
    sub = D // LANES
    return pl.pallas_call(
        functools.partial(_combine_kernel, tmC=tmC, K=K, alpha=alpha, nt=nt, sub=sub),
        grid=(nt,),
        in_specs=[pl.BlockSpec((1, 1, K * tmC), lambda i: (i, 0, 0), memory_space=pltpu.SMEM),
                  pl.BlockSpec((1, 1, K * tmC), lambda i: (jnp.minimum(i + 1, nt - 1), 0, 0),
                               memory_space=pltpu.SMEM),
                  pl.BlockSpec(memory_space=pl.ANY),
                  pl.BlockSpec((tmC, D), lambda i: (i, 0)),
                  pl.BlockSpec((tmC, LANES), lambda i: (i, 0)),
                  pl.BlockSpec((1, N_MOD, D), lambda i: (i // per_seq, 0, 0)),
                  pl.BlockSpec((1, D), lambda i: (0, 0)),
                  pl.BlockSpec((1, D), lambda i: (0, 0))],
        out_specs=pl.BlockSpec((tmC, D), lambda i: (i, 0)),
        out_shape=jax.ShapeDtypeStruct((T, D), F32),
        scratch_shapes=[pltpu.VMEM((2, K, tmC * sub, LANES), F32), pltpu.SemaphoreType.DMA((2,))],
        compiler_params=pltpu.CompilerParams(
            dimension_semantics=("arbitrary",), vmem_limit_bytes=48 * 2 ** 20),
    )(pos, pos, y_sorted, x1, tg, mod, ln2g, ln2b)


def _routing_tables(top_e, rank, counts, E, tmE, sub):
    T, K = top_e.shape
    padded = (counts + tmE - 1) // tmE * tmE
    pad_end = jnp.cumsum(padded)
    pad_start = pad_end - padded
    n_blocks = -(-(T * K) // tmE) + E
    block_start = jnp.arange(n_blocks, dtype=I32) * tmE
    block_e = jnp.minimum(jnp.sum((pad_end[None, :] <= block_start[:, None]).astype(I32), axis=1), E - 1)
    n_used = (pad_end[-1] // tmE).astype(I32).reshape(1)
    pos = (jnp.take(pad_start, top_e) + rank).astype(I32) * sub
    return block_e.astype(I32), n_used, pad_end.astype(I32), pos, n_blocks


def _tile_major(pos, tile):
    T, K = pos.shape
    return pos.reshape(T // tile, tile, K).transpose(0, 2, 1).reshape(T // tile, 1, K * tile)


def _layer(x, mod, w_in, kv_norm_g, w_uk, w_uv, w_pool_group, pool_scale, w_branch_attn, w_branch_pool,
           w_out, ln1_g, ln1_b, w_router, b_router, w_gate_up, b_gate_up, w_down, b_down, ln2_g, ln2_b,
           *, alpha):
    B, S, D = x.shape
    R, H, dh = w_uk.shape
    AW = H * dh
    PW = pool_scale.shape[0]
    E = w_router.shape[1]
    K = TOP_K_EXPERTS
    n_idx = w_in.shape[1] - (AW + R + PW + 2 * D)
    di = dh
    Hi = (n_idx - di) // (di + 1)
    sizes = (AW, R, Hi * di, di, Hi, PW, 2 * D)
    offs = [0]
    for s in sizes:
        offs.append(offs[-1] + s)
    ws = [w_in[:, offs[k]:offs[k + 1]].astype(BF16) for k in range(len(sizes))]
    ws[4] = jnp.pad(ws[4], ((0, 0), (0, LANES - Hi)))

    wukt = w_uk.transpose(1, 0, 2).astype(BF16)
    wuvt = w_uv.transpose(1, 2, 0).astype(BF16)
    qlt, ckva, ckvt, qit, ki, wit, u, g = _inproj(
        x, mod, ws, kv_norm_g.reshape(1, R), wukt, q_scale=dh ** -0.5 * LOG2E, qi_scale=di ** -0.5,
        wi_scale=Hi ** -0.5, Hi=Hi, tm=min(S, 512))

    topk = min(TOPK_MAX, S // 4)
    y_attn_t = _attention(qit, wit, ki, qlt, ckva, ckvt, wuvt, tq=min(S, 256), topk=topk)

    wr = jnp.pad(w_router, ((0, 0), (0, LANES - E)))
    wr_hi = wr.astype(BF16)
    wr = jnp.stack([wr_hi, (wr - wr_hi.astype(F32)).astype(BF16)])
    br = jnp.pad(b_router, (0, LANES - E)).reshape(1, LANES)
    x1, h2, te, tg = _merge(x, y_attn_t, u, g, mod, w_pool_group.astype(BF16), pool_scale.reshape(1, PW),
                            w_branch_attn.astype(BF16), w_branch_pool.astype(BF16), w_out.astype(BF16),
                            ln1_g.reshape(1, D), ln1_b.reshape(1, D), wr, br,
                            tm=min(S, 256), alpha=alpha, E=E, K=K)

    T = B * S
    tmE = 512
    tmC = min(S, 256)
    tD = min(T, 1024)
    sub = D // LANES
    te = te.reshape(T, LANES)
    rank, counts = _route(te, tR=min(T, 512), K=K)
    block_e, n_used, pad_end, pos, n_blocks = _routing_tables(te[:, :K], rank[:, :K], counts[0, :E], E, tmE,
                                                              sub)
    xs = _dispatch(pad_end, n_used, _tile_major(pos, tD), h2.reshape(T * sub, LANES), n_blocks,
                   tD=tD, tmE=tmE, K=K, sub=sub)
    y_sorted = _experts(block_e, n_used, xs, w_gate_up, b_gate_up.reshape(E, 1, -1),
                        w_down, b_down.reshape(E, 1, D), tmE=tmE)
    out = _combine(_tile_major(pos, tmC), y_sorted, x1.reshape(T, D), tg.reshape(T, LANES), mod,
                   ln2_g.reshape(1, D), ln2_b.reshape(1, D), tmC=tmC, K=K, alpha=alpha, S=S)
    return out.reshape(B, S, D)


def kernel(x, c, w_ada, b_ada, w_in, kv_norm_g, w_uk, w_uv, w_pool_group, pool_scale, w_branch_attn,
           w_branch_pool, w_out, ln1_g, ln1_b, w_router, b_router, w_gate_up, b_gate_up, w_down, b_down,
           ln2_g, ln2_b):
    B, S, D = x.shape
    depth = w_ada.shape[0]
    alpha = (2.0 * depth) ** 0.25
    for l in range(depth):
        mod = _ada(c, w_ada[l], b_ada[l]).reshape(B, N_MOD, D)
        x = _layer(x, mod, w_in[l], kv_norm_g[l], w_uk[l], w_uv[l], w_pool_group[l], pool_scale[l],
                   w_branch_attn[l], w_branch_pool[l], w_out[l], ln1_g[l], ln1_b[l], w_router[l],
                   b_router[l], w_gate_up[l], b_gate_up[l], w_down[l], b_down[l], ln2_g[l], ln2_b[l],
                   alpha=alpha)
    return x
```

```python
import functools

import jax
import jax.numpy as jnp
from jax import lax
from jax.experimental import pallas as pl
from jax.experimental.pallas import tpu as pltpu

F32 = jnp.float32
BF16 = jnp.bfloat16
I32 = jnp.int32
HIGHEST = lax.Precision.HIGHEST

LN_EPS = 1e-5
TOPK_MAX = 256
TOP_K_EXPERTS = 4
POOL_WINDOWS = (2, 4, 8, 16)
SWIGLU_LIMIT = 7.0
SWIGLU_ALPHA = 1.702
N_MOD = 6
INT_MIN = -(2 ** 31)
NEG_BIG = -1e30
LANES = 128
SUBLANES = 8
SORT_GROUP = 16
TIE_CHUNK = 256
POOL_HALO = 16
ONES_ROWS = 16
LOG2E = 1.4426950408889634
ATTN_KEY_CHUNK = 256
ATTN_HEAD_UNROLL = 4


def _store_row_slabs(ref, x):
    n, D = x.shape
    sub = D // LANES
    for s in range(sub):
        ref[pl.ds(s, n, stride=sub), :] = x[:, s * LANES:(s + 1) * LANES]


def _load_row_slabs(ref, n, sub):
    return jnp.concatenate([ref[pl.ds(s, n, stride=sub), :] for s in range(sub)], axis=1)


def _layer_norm(z, g, b):
    mu = jnp.mean(z, axis=-1, keepdims=True)
    zc = z - mu
    var = jnp.mean(zc * zc, axis=-1, keepdims=True)
    return zc * lax.rsqrt(var + LN_EPS) * g + b


def _ada_kernel(c_ref, w_ref, b_ref, o_ref):
    c = c_ref[...]
    cond = c * jax.nn.sigmoid(c)
    o_ref[...] = jnp.dot(cond, w_ref[...], precision=HIGHEST, preferred_element_type=F32) + b_ref[...]


def _ada(c, w_ada, b_ada):
    B, D = c.shape
    N = w_ada.shape[1]
    tn = D
    return pl.pallas_call(
        _ada_kernel,
        grid=(N // tn,),
        in_specs=[pl.BlockSpec((B, D), lambda j: (0, 0)),
                  pl.BlockSpec((D, tn), lambda j: (0, j)),
                  pl.BlockSpec((1, tn), lambda j: (0, j))],
        out_specs=pl.BlockSpec((B, tn), lambda j: (0, j)),
        out_shape=jax.ShapeDtypeStruct((B, N), F32),
    )(c, w_ada, b_ada.reshape(1, N))


def _inproj_kernel(x_ref, mod_ref, wq, wckv, wqi, wki, wwi, wu, wg, kvg_ref, wukt_ref,
                   qlt_o, ckva_o, ckvt_o, qit_o, ki_o, wit_o, u_o, g_o,
                   *, tm, H, dh, Hi, q_scale, qi_scale, wi_scale):
    i = pl.program_id(1)
    x = x_ref[0]
    mod = mod_ref[0]
    h = (x * (1.0 + mod[1:2]) + mod[0:1]).astype(BF16)

    def proj(w):
        return jnp.dot(h, w[...], preferred_element_type=F32)

    qt = proj(wq).T.astype(BF16)
    for hh in range(H):
        qlt = jnp.dot(wukt_ref[hh], qt[hh * dh:(hh + 1) * dh, :], preferred_element_type=F32) * q_scale
        qlt_o[0, hh] = qlt.astype(BF16)

    ckv = proj(wckv)
    ckv = ckv * lax.rsqrt(jnp.mean(ckv * ckv, axis=-1, keepdims=True) + LN_EPS) * kvg_ref[...]
    ckvt_o[0, 0:ckv.shape[1], :] = ckv.T.astype(BF16)
    ckvt_o[0, ckv.shape[1]:, :] = jnp.ones((ONES_ROWS, tm), BF16)
    R = ckv.shape[1]
    pos = i * tm + lax.broadcasted_iota(I32, (tm, R), 0)
    lane = lax.broadcasted_iota(I32, (tm, R), 1)
    pos_hi = ((pos >> 8) << 8).astype(F32)
    pos_lo = (pos & 255).astype(F32)
    extra = jnp.where((lane == 0) | (lane == 2), pos_hi, jnp.where((lane == 1) | (lane == 3), pos_lo, 0.0))
    ckva_o[0, :, 0:R] = ckv.astype(BF16)
    ckva_o[0, :, R:2 * R] = extra.astype(BF16)

    qit_o[0] = (proj(wqi) * qi_scale).T.astype(BF16)
    ki_o[0] = proj(wki).astype(BF16)
    wit_o[0] = (proj(wwi) * wi_scale).T[0:Hi, :]
    u_o[0] = proj(wu)
    g_o[0] = proj(wg).astype(BF16)


def _inproj(x, mod, ws, kv_norm_g, wukt, *, q_scale, qi_scale, wi_scale, Hi, tm):
    B, S, D = x.shape
    wq, wckv, wqi, wki, wwi, wu, wg = ws
    H, R, dh = wukt.shape
    row = lambda n: pl.BlockSpec((1, tm, n), lambda b, i: (b, i, 0))
    col = lambda n: pl.BlockSpec((1, n, tm), lambda b, i: (b, 0, i))
    full = lambda a: pl.BlockSpec(a.shape, lambda b, i: (0,) * a.ndim)
    out_specs = [pl.BlockSpec((1, H, R, tm), lambda b, i: (b, 0, 0, i)), row(2 * R), col(R + ONES_ROWS),
                 col(wqi.shape[1]),
                 row(wki.shape[1]), col(Hi), row(wu.shape[1]), row(wg.shape[1])]
    out_shape = [jax.ShapeDtypeStruct((B, H, R, S), BF16), jax.ShapeDtypeStruct((B, S, 2 * R), BF16),
                 jax.ShapeDtypeStruct((B, R + ONES_ROWS, S), BF16),
                 jax.ShapeDtypeStruct((B, wqi.shape[1], S), BF16),
                 jax.ShapeDtypeStruct((B, S, wki.shape[1]), BF16), jax.ShapeDtypeStruct((B, Hi, S), F32),
                 jax.ShapeDtypeStruct((B, S, wu.shape[1]), F32), jax.ShapeDtypeStruct((B, S, wg.shape[1]), BF16)]
    return pl.pallas_call(
        functools.partial(_inproj_kernel, tm=tm, H=H, dh=dh, Hi=Hi, q_scale=q_scale, qi_scale=qi_scale,
                          wi_scale=wi_scale),
        grid=(B, S // tm),
        in_specs=[row(D), pl.BlockSpec((1, N_MOD, D), lambda b, i: (b, 0, 0))]
                 + [full(w) for w in ws] + [full(kv_norm_g), full(wukt)],
        out_specs=out_specs,
        out_shape=out_shape,
        compiler_params=pltpu.CompilerParams(
            dimension_semantics=("parallel", "parallel"), vmem_limit_bytes=48 * 2 ** 20),
    )(x, mod, *ws, kv_norm_g, wukt)


def _key_to_f32(t):
    return pltpu.bitcast(jnp.where(t >= 0, t, t ^ 0x7FFFFFFF), F32)


def _sort_network(n):
    pairs, p = [], 1
    while p < n:
        k = p
        while k >= 1:
            for j in range(k % p, n - k, 2 * k):
                for i in range(min(k, n - j - k)):
                    if (i + j) // (2 * p) == (i + j + k) // (2 * p):
                        pairs.append((i + j, i + j + k))
            k //= 2
        p *= 2
    return pairs


def _count_ge_sorted(blocks, cand):
    n = len(blocks)

    def pick(level, bits):
        if len(level) == 1:
            return level[0]
        half = len(level) // 2
        return jnp.where(bits[0], pick(level[half:], bits[1:]), pick(level[:half], bits[1:]))

    total = jnp.where(blocks[n - 1] >= cand, 1.0, 0.0)
    bits = []
    step = n // 2
    while step >= 1:
        level = [blocks[c + step - 1] for c in range(0, n, 2 * step)]
        b = pick(level, bits) >= cand
        bits.append(b)
        total = total + jnp.where(b, float(step), 0.0)
        step //= 2
    return total


def _attn_body(qit_ref, wit_ref, ki_ref, qlt_ref, aug_ref, ckva_ref, ckvt_ref, wuvt_ref, yt_ref,
               score_scr, bias_scr, lg_scr, o_scr, sort_scr, *, W, tq, S, H, dh, Hi, di, topk, cw):
    j = pl.program_id(1)
    qpos = j * tq + lax.broadcasted_iota(I32, (1, tq), 1)
    score = score_scr.at[0:W, :]

    wit = wit_ref[0]
    for c in range(W // cw):
        kic = ki_ref[0, c * cw:(c + 1) * cw, :]
        acc = jnp.zeros((cw, tq), F32)
        for h in range(Hi):
            d = jnp.dot(kic, qit_ref[0, h * di:(h + 1) * di, :], preferred_element_type=F32)
            acc = acc + wit[h:h + 1, :] * jnp.maximum(d, 0.0)
        kpos_c = c * cw + lax.broadcasted_iota(I32, (cw, tq), 0)
        score_scr[c * cw:(c + 1) * cw, :] = jnp.where(kpos_c <= qpos, acc, -jnp.inf)

    def count(hit):
        return jnp.sum(jnp.where(hit, 1.0, 0.0), axis=0, keepdims=True)

    Wb = W // SORT_GROUP
    net = _sort_network(SORT_GROUP)

    def column(r, lg):
        return [(slice(g * Wb + r * SUBLANES, g * Wb + (r + 1) * SUBLANES), slice(lg * LANES, (lg + 1) * LANES))
                for g in range(SORT_GROUP)]

    for r in range(Wb // SUBLANES):
        for lg in range(tq // LANES):
            v = [score_scr[idx] for idx in column(r, lg)]
            for a, b in net:
                v[a], v[b] = jnp.maximum(v[a], v[b]), jnp.minimum(v[a], v[b])
            for idx, val in zip(column(r, lg), v):
                sort_scr[idx] = val

    def count_ge(cand_f):
        parts = []
        for lg in range(tq // LANES):
            c = cand_f[:, lg * LANES:(lg + 1) * LANES]
            acc = None
            for r in range(Wb // SUBLANES):
                n = _count_ge_sorted([sort_scr[idx] for idx in column(r, lg)], c)
                acc = n if acc is None else acc + n
            parts.append(jnp.sum(acc, axis=0, keepdims=True))
        return jnp.concatenate(parts, axis=1)

    def thr_step(i, t):
        cand = t + lax.shift_left(jnp.int32(1), 31 - i)
        cnt = count_ge(_key_to_f32(cand))
        return jnp.where(cnt >= float(topk), cand, t)

    t = lax.fori_loop(0, 32, thr_step, jnp.full((1, tq), INT_MIN, I32))
    few = t == INT_MIN
    thr_f = _key_to_f32(t)
    up_f = _key_to_f32(t + 1)
    need = float(topk) - count(score[...] >= up_f)

    tc = min(TIE_CHUNK, W)
    earlier = (lax.broadcasted_iota(I32, (tc, tc), 1) < lax.broadcasted_iota(I32, (tc, tc), 0)).astype(BF16)
    tied_before = jnp.zeros((1, tq), F32)
    for c in range(W // tc):
        rows = slice(c * tc, (c + 1) * tc)
        s = score_scr[rows, :]
        gt = s >= up_f
        eq = (s >= thr_f) & jnp.logical_not(gt)
        eq_f = jnp.where(eq, 1.0, 0.0)
        rank = jnp.dot(earlier, eq_f.astype(BF16), preferred_element_type=F32) + tied_before
        tied_before = tied_before + jnp.sum(eq_f, axis=0, keepdims=True)
        kpos = c * tc + lax.broadcasted_iota(I32, (tc, tq), 0)
        sel = (few | gt | (eq & (rank < need))) & (kpos <= qpos)
        bias_scr[rows, :] = jnp.where(sel, 0.0, NEG_BIG)

    R = qlt_ref.shape[2]
    ck = min(ATTN_KEY_CHUNK, W)

    def head(h, carry):
        qa = jnp.concatenate([qlt_ref[0, h], aug_ref[h]], axis=0)
        m = None
        for c in range(W // ck):
            keys = slice(c * ck, (c + 1) * ck)
            lg = jnp.dot(ckva_ref[0, keys, :], qa, preferred_element_type=F32) + bias_scr[keys, :]
            lg_scr[keys, :] = lg
            mc = jnp.max(lg, axis=0, keepdims=True)
            m = mc if m is None else jnp.maximum(m, mc)
        acc = jnp.zeros((R + ONES_ROWS, tq), F32)
        for c in range(W // ck):
            keys = slice(c * ck, (c + 1) * ck)
            p = jnp.exp2(lg_scr[keys, :] - m)
            acc = acc + jnp.dot(ckvt_ref[0, :, keys], p.astype(BF16), preferred_element_type=F32)
        o_scr[h] = (acc[0:R] / acc[R:R + 1]).astype(BF16)
        return carry

    lax.fori_loop(0, H, head, 0, unroll=ATTN_HEAD_UNROLL)
    for h in range(H):
        yt = jnp.dot(wuvt_ref[h], o_scr[h], preferred_element_type=F32)
        yt_ref[0, h * dh:(h + 1) * dh, :] = yt.astype(BF16)


def _attn_kernel(*refs, tq, cw, S, **kw):
    j = pl.program_id(1)
    nc = ((j + 1) * tq + cw - 1) // cw
    for w in range(1, S // cw + 1):
        @pl.when(nc == w)
        def _():
            _attn_body(*refs, W=w * cw, tq=tq, S=S, cw=cw, **kw)


def _attention(qit, wit, ki, qlt, ckva, ckvt, wuvt, *, tq, topk):
    B, H, R, S = qlt.shape
    dh = wuvt.shape[1]
    di = ki.shape[2]
    Hi = wit.shape[1]
    slopes = jnp.asarray([LOG2E * 2.0 ** (-8.0 * (h + 1) / H) for h in range(H)], F32)
    s_hi = slopes.astype(BF16)
    s_lo = (slopes - s_hi.astype(F32)).astype(BF16)
    aug = jnp.zeros((H, R, tq), BF16)
    aug = aug.at[:, 0:2, :].set(s_hi[:, None, None]).at[:, 2:4, :].set(s_lo[:, None, None])
    kern = functools.partial(_attn_kernel, tq=tq, S=S, H=H, dh=dh, Hi=Hi, di=di, topk=topk, cw=min(S, 512))
    col = lambda n: pl.BlockSpec((1, n, tq), lambda b, j: (b, 0, j))
    full = lambda a: pl.BlockSpec(a.shape, lambda b, j: (0,) * a.ndim)
    return pl.pallas_call(
        kern,
        grid=(B, S // tq),
        in_specs=[col(qit.shape[1]), col(Hi), pl.BlockSpec((1, S, di), lambda b, j: (b, 0, 0)),
                  pl.BlockSpec((1, H, R, tq), lambda b, j: (b, 0, 0, j)), full(aug),
                  pl.BlockSpec((1, S, 2 * R), lambda b, j: (b, 0, 0)),
                  pl.BlockSpec((1, R + ONES_ROWS, S), lambda b, j: (b, 0, 0)), full(wuvt)],
        out_specs=col(H * dh),
        out_shape=jax.ShapeDtypeStruct((B, H * dh, S), BF16),
        scratch_shapes=[pltpu.VMEM((S, tq), F32), pltpu.VMEM((S, tq), F32), pltpu.VMEM((S, tq), F32),
                        pltpu.VMEM((H, R, tq), BF16), pltpu.VMEM((S, tq), F32)],
        compiler_params=pltpu.CompilerParams(
            dimension_semantics=("parallel", "parallel"), vmem_limit_bytes=48 * 2 ** 20),
    )(qit, wit, ki, qlt, aug, ckva, ckvt, wuvt)


def _merge_kernel(x_ref, yat_ref, u_ref, halo_ref, g_ref, mod_ref, wpg_ref, psc_ref, wba_ref, wbp_ref,
                  wout_ref, ln1g_ref, ln1b_ref, wr_ref, br_ref,
                  x1_o, h2_o, te_o, tg_o, ext_scr, *, tm, D, G, alpha, E, K):
    i = pl.program_id(1)

    @pl.when(i == 0)
    def _():
        ext_scr[0:POOL_HALO, :] = jnp.zeros((POOL_HALO, ext_scr.shape[1]), F32)

    @pl.when(i > 0)
    def _():
        ext_scr[0:POOL_HALO, :] = halo_ref[0]

    ext_scr[POOL_HALO:POOL_HALO + tm, :] = u_ref[0]

    pos = i * tm + lax.broadcasted_iota(I32, (tm, 1), 0)
    ys = []
    for g, w in enumerate(POOL_WINDOWS):
        cols = slice(g * G, (g + 1) * G)
        cur = ext_scr[POOL_HALO:POOL_HALO + tm, cols]
        acc = cur
        for back in range(1, w):
            acc = acc + ext_scr[POOL_HALO - back:POOL_HALO - back + tm, cols]
        cnt = jnp.minimum(pos + 1, w).astype(F32)
        pooled = acc / cnt - cur
        ys.append(jnp.dot(pooled.astype(BF16), wpg_ref[g], preferred_element_type=F32))
    y_pool = jnp.concatenate(ys, axis=1) * psc_ref[...]

    gates = g_ref[0]
    ya = yat_ref[0].astype(F32).T.astype(BF16)
    a = jnp.dot(ya, wba_ref[...], preferred_element_type=F32)
    p = jnp.dot(y_pool.astype(BF16), wbp_ref[...], preferred_element_type=F32)
    mix = jax.nn.sigmoid(gates[:, :D].astype(F32)) * a + jax.nn.sigmoid(gates[:, D:].astype(F32)) * p
    o = jnp.dot(mix.astype(BF16), wout_ref[...], preferred_element_type=F32)
    mod = mod_ref[0]
    x1 = _layer_norm(alpha * x_ref[0] + mod[2:3] * o, ln1g_ref[...], ln1b_ref[...])
    x1_o[0] = x1
    h2 = x1 * (1.0 + mod[4:5]) + mod[3:4]
    _store_row_slabs(h2_o.at[0], h2)

    h2_hi = h2.astype(BF16)
    h2_lo = (h2 - h2_hi.astype(F32)).astype(BF16)
    logits = (jnp.dot(h2_hi, wr_ref[0], preferred_element_type=F32)
              + jnp.dot(h2_hi, wr_ref[1], preferred_element_type=F32)
              + jnp.dot(h2_lo, wr_ref[0], preferred_element_type=F32)) + br_ref[...]
    lane = lax.broadcasted_iota(I32, logits.shape, 1)
    logits = jnp.where(lane < E, logits, -jnp.inf)
    te = jnp.zeros(logits.shape, I32)
    vals = []
    for k in range(K):
        m = jnp.max(logits, axis=1, keepdims=True)
        idx = jnp.min(jnp.where(logits == m, lane, LANES), axis=1, keepdims=True)
        te = jnp.where(lane == k, idx, te)
        vals.append(m)
        logits = jnp.where(lane == idx, -jnp.inf, logits)
    ex = [jnp.exp(v - vals[0]) for v in vals]
    den = ex[0]
    for e in ex[1:]:
        den = den + e
    tg = jnp.zeros(logits.shape, F32)
    for k in range(K):
        tg = jnp.where(lane == k, ex[k] / den, tg)
    te_o[0] = te
    tg_o[0] = tg


def _merge(x, yat, u, g, mod, wpg, psc, wba, wbp, wout, ln1g, ln1b, wr, br, *, tm, alpha, E, K):
    B, S, D = x.shape
    PW = u.shape[2]
    G = wpg.shape[1]
    kern = functools.partial(_merge_kernel, tm=tm, D=D, G=G, alpha=alpha, E=E, K=K)
    row = lambda n: pl.BlockSpec((1, tm, n), lambda b, i: (b, i, 0))
    full = lambda a: pl.BlockSpec(a.shape, lambda b, i: (0,) * a.ndim)
    hb = tm // POOL_HALO
    halo = pl.BlockSpec((1, POOL_HALO, PW), lambda b, i: (b, jnp.maximum(i * hb - 1, 0), 0))
    return pl.pallas_call(
        kern,
        grid=(B, S // tm),
        in_specs=[row(D), pl.BlockSpec((1, yat.shape[1], tm), lambda b, i: (b, 0, i)), row(PW), halo,
                  row(g.shape[2]), pl.BlockSpec((1, N_MOD, D), lambda b, i: (b, 0, 0)),
                  full(wpg), full(psc), full(wba), full(wbp), full(wout), full(ln1g), full(ln1b),
                  full(wr), full(br)],
        out_specs=[row(D), pl.BlockSpec((1, tm * (D // LANES), LANES), lambda b, i: (b, i, 0)),
                   row(LANES), row(LANES)],
        out_shape=[jax.ShapeDtypeStruct((B, S, D), F32),
                   jax.ShapeDtypeStruct((B, S * (D // LANES), LANES), F32),
                   jax.ShapeDtypeStruct((B, S, LANES), I32), jax.ShapeDtypeStruct((B, S, LANES), F32)],
        scratch_shapes=[pltpu.VMEM((POOL_HALO + tm, PW), F32)],
        compiler_params=pltpu.CompilerParams(
            dimension_semantics=("parallel", "arbitrary"), vmem_limit_bytes=48 * 2 ** 20),
    )(x, yat, u, u, g, mod, wpg, psc, wba, wbp, wout, ln1g, ln1b, wr, br)


def _route_kernel(te_ref, rank_o, cnt_o, base_scr, *, tR, K):
    i = pl.program_id(0)

    @pl.when(i == 0)
    def _():
        base_scr[...] = jnp.zeros(base_scr.shape, F32)

    te = te_ref[...]
    lane = lax.broadcasted_iota(I32, te.shape, 1)
    hits = [lane == te[:, k:k + 1] for k in range(K)]
    onehot = hits[0].astype(F32)
    for k in range(1, K):
        onehot = onehot + hits[k].astype(F32)
    earlier = (lax.broadcasted_iota(I32, (tR, tR), 0) > lax.broadcasted_iota(I32, (tR, tR), 1)).astype(BF16)
    before = jnp.dot(earlier, onehot.astype(BF16), preferred_element_type=F32) + base_scr[...]
    rank = jnp.zeros(te.shape, I32)
    for k in range(K):
        rk = jnp.sum(jnp.where(hits[k], before, 0.0), axis=1, keepdims=True)
        rank = jnp.where(lane == k, rk.astype(I32), rank)
    rank_o[...] = rank
    base_scr[...] = base_scr[...] + jnp.sum(onehot, axis=0, keepdims=True)
    cnt_o[...] = base_scr[...].astype(I32)


def _route(te, *, tR, K):
    T = te.shape[0]
    return pl.pallas_call(
        functools.partial(_route_kernel, tR=tR, K=K),
        grid=(T // tR,),
        in_specs=[pl.BlockSpec((tR, LANES), lambda i: (i, 0))],
        out_specs=[pl.BlockSpec((tR, LANES), lambda i: (i, 0)), pl.BlockSpec((1, LANES), lambda i: (0, 0))],
        out_shape=[jax.ShapeDtypeStruct((T, LANES), I32), jax.ShapeDtypeStruct((1, LANES), I32)],
        scratch_shapes=[pltpu.VMEM((1, LANES), F32)],
        compiler_params=pltpu.CompilerParams(dimension_semantics=("arbitrary",)),
    )(te)


def _dispatch_kernel(pe_ref, nu_ref, pos_ref, h2_ref, xs_out, zero_scr, sem, zsem,
                     *, tD, K, sub, E, tmE, n_blocks):
    blk = tmE * sub

    @pl.when(pl.program_id(0) == 0)
    def _():
        zero_scr[...] = jnp.zeros(zero_scr.shape, zero_scr.dtype)

        def zero_block(b):
            return pltpu.make_async_copy(zero_scr, xs_out.at[pl.ds(pl.multiple_of(b * blk, blk), blk), :], zsem)

        def last_block(e):
            return jnp.maximum(pe_ref[e] // tmE - 1, 0)

        def start_last(e, carry):
            zero_block(last_block(e)).start()
            return carry

        def start_tail(b, carry):
            zero_block(b).start()
            return carry

        def wait_last(e, carry):
            zero_block(last_block(e)).wait()
            return carry

        def wait_tail(b, carry):
            zero_block(b).wait()
            return carry

        lax.fori_loop(0, E, start_last, 0)
        lax.fori_loop(nu_ref[0], n_blocks, start_tail, 0)
        lax.fori_loop(0, E, wait_last, 0)
        lax.fori_loop(nu_ref[0], n_blocks, wait_tail, 0)

    def issue(r, carry):
        src = h2_ref.at[pl.ds(pl.multiple_of(r * sub, sub), sub), :]
        for k in range(K):
            p = pl.multiple_of(pos_ref[0, 0, k * tD + r], sub)
            pltpu.async_copy(src, xs_out.at[pl.ds(p, sub), :], sem, priority=k % 2)
        return carry

    lax.fori_loop(0, tD, issue, 0, unroll=8)
    for k in range(K):
        pltpu.make_async_copy(h2_ref, xs_out.at[pl.ds(0, tD * sub), :], sem).wait()


def _dispatch(pad_end, n_used, pos, h2, n_blocks, *, tD, tmE, K, sub):
    T = h2.shape[0] // sub
    E = pad_end.shape[0]
    grid_spec = pltpu.PrefetchScalarGridSpec(
        num_scalar_prefetch=2,
        grid=(T // tD,),
        in_specs=[pl.BlockSpec((1, 1, K * tD), lambda i, pe, nu: (i, 0, 0), memory_space=pltpu.SMEM),
                  pl.BlockSpec((tD * sub, LANES), lambda i, pe, nu: (i, 0))],
        out_specs=pl.BlockSpec(memory_space=pl.ANY),
        scratch_shapes=[pltpu.VMEM((tmE * sub, LANES), h2.dtype), pltpu.SemaphoreType.DMA(()),
                        pltpu.SemaphoreType.DMA(())],
    )
    return pl.pallas_call(
        functools.partial(_dispatch_kernel, tD=tD, K=K, sub=sub, E=E, tmE=tmE, n_blocks=n_blocks),
        grid_spec=grid_spec,
        out_shape=jax.ShapeDtypeStruct((n_blocks * tmE * sub, LANES), h2.dtype),
        compiler_params=pltpu.CompilerParams(
            dimension_semantics=("arbitrary",), has_side_effects=True),
    )(pad_end, n_used, pos, h2)


def _expert_kernel(be_ref, nu_ref, x_ref, wgu_ref, bgu_ref, wd_ref, bd_ref, y_ref, wgu_bf, wd_bf,
                   *, F, tmE, sub):
    i = pl.program_id(0)
    new_expert = (i == 0) | (be_ref[i] != be_ref[jnp.maximum(i - 1, 0)])

    @pl.when(new_expert)
    def _():
        wgu_bf[...] = wgu_ref[0].astype(BF16)
        wd_bf[...] = wd_ref[0].astype(BF16)

    @pl.when(i < nu_ref[0])
    def _():
        x = _load_row_slabs(x_ref, tmE, sub).astype(BF16)
        gu = jnp.dot(x, wgu_bf[...], preferred_element_type=F32) + bgu_ref[0]
        glu = jnp.minimum(gu[:, :F], SWIGLU_LIMIT)
        lin = jnp.clip(gu[:, F:], -SWIGLU_LIMIT, SWIGLU_LIMIT)
        act = glu * jax.nn.sigmoid(SWIGLU_ALPHA * glu) * (lin + 1.0)
        y = jnp.dot(act.astype(BF16), wd_bf[...], preferred_element_type=F32) + bd_ref[0]
        _store_row_slabs(y_ref, y)

    @pl.when(i >= nu_ref[0])
    def _():
        y_ref[...] = jnp.zeros(y_ref.shape, F32)


def _experts(block_e, n_used, xs, wgu, bgu, wd, bd, *, tmE):
    E, D, F2 = wgu.shape
    F = F2 // 2
    sub = D // LANES
    n_blocks = xs.shape[0] // (tmE * sub)
    slab_block = pl.BlockSpec((tmE * sub, LANES), lambda i, be, nu: (i, 0))
    grid_spec = pltpu.PrefetchScalarGridSpec(
        num_scalar_prefetch=2,
        grid=(n_blocks,),
        in_specs=[slab_block,
                  pl.BlockSpec((1, D, F2), lambda i, be, nu: (be[i], 0, 0)),
                  pl.BlockSpec((1, 1, F2), lambda i, be, nu: (be[i], 0, 0)),
                  pl.BlockSpec((1, F, D), lambda i, be, nu: (be[i], 0, 0)),
                  pl.BlockSpec((1, 1, D), lambda i, be, nu: (be[i], 0, 0))],
        out_specs=slab_block,
        scratch_shapes=[pltpu.VMEM((D, F2), BF16), pltpu.VMEM((F, D), BF16)],
    )
    return pl.pallas_call(
        functools.partial(_expert_kernel, F=F, tmE=tmE, sub=sub),
        grid_spec=grid_spec,
        out_shape=jax.ShapeDtypeStruct(xs.shape, F32),
        compiler_params=pltpu.CompilerParams(
            dimension_semantics=("arbitrary",), vmem_limit_bytes=56 * 2 ** 20),
    )(block_e, n_used, xs, wgu, bgu, wd, bd)


def _combine_kernel(pos_ref, nxt_ref, y_hbm, x1_ref, tg_ref, mod_ref, g_ref, b_ref, o_ref, buf, sem,
                    *, tmC, K, alpha, nt, sub):
    i = pl.program_id(0)
    slot = lax.rem(i, 2)

    def fetch(p_ref, s):
        def issue(r, carry):
            dst_rows = pl.ds(pl.multiple_of(r * sub, sub), sub)
            for k in range(K):
                p = pl.multiple_of(p_ref[0, 0, k * tmC + r], sub)
                pltpu.async_copy(y_hbm.at[pl.ds(p, sub), :], buf.at[s, k, dst_rows, :], sem.at[s],
                                 priority=k % 2)
            return carry

        lax.fori_loop(0, tmC, issue, 0, unroll=8)

    @pl.when(i == 0)
    def _():
        fetch(pos_ref, slot)

    @pl.when(i + 1 < nt)
    def _():
        fetch(nxt_ref, 1 - slot)

    for k in range(K):
        pltpu.make_async_copy(y_hbm.at[pl.ds(0, tmC * sub), :], buf.at[slot, k], sem.at[slot]).wait()

    tg = tg_ref[...]
    y = tg[:, 0:1] * _load_row_slabs(buf.at[slot, 0], tmC, sub)
    for k in range(1, K):
        y = y + tg[:, k:k + 1] * _load_row_slabs(buf.at[slot, k], tmC, sub)
    mod = mod_ref[0]
    o_ref[...] = _layer_norm(alpha * x1_ref[...] + mod[5:6] * y, g_ref[...], b_ref[...])


def _combine(pos, y_sorted, x1, tg, mod, ln2g, ln2b, *, tmC, K, alpha, S):
    T, D = x1.shape
    nt = T // tmC
    per_seq = S // tmC
    sub = D // LANES
    return pl.pallas_call(
        functools.partial(_combine_kernel, tmC=tmC, K=K, alpha=alpha, nt=nt, sub=sub),
        grid=(nt,),
        in_specs=[pl.BlockSpec((1, 1, K * tmC), lambda i: (i, 0, 0), memory_space=pltpu.SMEM),
                  pl.BlockSpec((1, 1, K * tmC), lambda i: (jnp.minimum(i + 1, nt - 1), 0, 0),
                               memory_space=pltpu.SMEM),
                  pl.BlockSpec(memory_space=pl.ANY),
                  pl.BlockSpec((tmC, D), lambda i: (i, 0)),
                  pl.BlockSpec((tmC, LANES), lambda i: (i, 0)),
                  pl.BlockSpec((1, N_MOD, D), lambda i: (i // per_seq, 0, 0)),
                  pl.BlockSpec((1, D), lambda i: (0, 0)),
                  pl.BlockSpec((1, D), lambda i: (0, 0))],
        out_specs=pl.BlockSpec((tmC, D), lambda i: (i, 0)),
        out_shape=jax.ShapeDtypeStruct((T, D), F32),
        scratch_shapes=[pltpu.VMEM((2, K, tmC * sub, LANES), F32), pltpu.SemaphoreType.DMA((2,))],
        compiler_params=pltpu.CompilerParams(
            dimension_semantics=("arbitrary",), vmem_limit_bytes=48 * 2 ** 20),
    )(pos, pos, y_sorted, x1, tg, mod, ln2g, ln2b)


def _routing_tables(top_e, rank, counts, E, tmE, sub):
    T, K = top_e.shape
    padded = (counts + tmE - 1) // tmE * tmE
    pad_end = jnp.cumsum(padded)
    pad_start = pad_end - padded
    n_blocks = -(-(T * K) // tmE) + E
    block_start = jnp.arange(n_blocks, dtype=I32) * tmE
    block_e = jnp.minimum(jnp.sum((pad_end[None, :] <= block_start[:, None]).astype(I32), axis=1), E - 1)
    n_used = (pad_end[-1] // tmE).astype(I32).reshape(1)
    pos = (jnp.take(pad_start, top_e) + rank).astype(I32) * sub
    return block_e.astype(I32), n_used, pad_end.astype(I32), pos, n_blocks


def _tile_major(pos, tile):
    T, K = pos.shape
    return pos.reshape(T // tile, tile, K).transpose(0, 2, 1).reshape(T // tile, 1, K * tile)


def _layer(x, mod, w_in, kv_norm_g, w_uk, w_uv, w_pool_group, pool_scale, w_branch_attn, w_branch_pool,
           w_out, ln1_g, ln1_b, w_router, b_router, w_gate_up, b_gate_up, w_down, b_down, ln2_g, ln2_b,
           *, alpha):
    B, S, D = x.shape
    R, H, dh = w_uk.shape
    AW = H * dh
    PW = pool_scale.shape[0]
    E = w_router.shape[1]
    K = TOP_K_EXPERTS
    n_idx = w_in.shape[1] - (AW + R + PW + 2 * D)
    di = dh
    Hi = (n_idx - di) // (di + 1)
    sizes = (AW, R, Hi * di, di, Hi, PW, 2 * D)
    offs = [0]
    for s in sizes:
        offs.append(offs[-1] + s)
    ws = [w_in[:, offs[k]:offs[k + 1]].astype(BF16) for k in range(len(sizes))]
    ws[4] = jnp.pad(ws[4], ((0, 0), (0, LANES - Hi)))

    wukt = w_uk.transpose(1, 0, 2).astype(BF16)
    wuvt = w_uv.transpose(1, 2, 0).astype(BF16)
    qlt, ckva, ckvt, qit, ki, wit, u, g = _inproj(
        x, mod, ws, kv_norm_g.reshape(1, R), wukt, q_scale=dh ** -0.5 * LOG2E, qi_scale=di ** -0.5,
        wi_scale=Hi ** -0.5, Hi=Hi, tm=min(S, 512))

    topk = min(TOPK_MAX, S // 4)
    y_attn_t = _attention(qit, wit, ki, qlt, ckva, ckvt, wuvt, tq=min(S, 256), topk=topk)

    wr = jnp.pad(w_router, ((0, 0), (0, LANES - E)))
    wr_hi = wr.astype(BF16)
    wr = jnp.stack([wr_hi, (wr - wr_hi.astype(F32)).astype(BF16)])
    br = jnp.pad(b_router, (0, LANES - E)).reshape(1, LANES)
    x1, h2, te, tg = _merge(x, y_attn_t, u, g, mod, w_pool_group.astype(BF16), pool_scale.reshape(1, PW),
                            w_branch_attn.astype(BF16), w_branch_pool.astype(BF16), w_out.astype(BF16),
                            ln1_g.reshape(1, D), ln1_b.reshape(1, D), wr, br,
                            tm=min(S, 256), alpha=alpha, E=E, K=K)

    T = B * S
    tmE = 512
    tmC = min(S, 256)
    tD = min(T, 1024)
    sub = D // LANES
    te = te.reshape(T, LANES)
    rank, counts = _route(te, tR=min(T, 512), K=K)
    block_e, n_used, pad_end, pos, n_blocks = _routing_tables(te[:, :K], rank[:, :K], counts[0, :E], E, tmE,
                                                              sub)
    xs = _dispatch(pad_end, n_used, _tile_major(pos, tD), h2.reshape(T * sub, LANES), n_blocks,
                   tD=tD, tmE=tmE, K=K, sub=sub)
    y_sorted = _experts(block_e, n_used, xs, w_gate_up, b_gate_up.reshape(E, 1, -1),
                        w_down, b_down.reshape(E, 1, D), tmE=tmE)
    out = _combine(_tile_major(pos, tmC), y_sorted, x1.reshape(T, D), tg.reshape(T, LANES), mod,
                   ln2_g.reshape(1, D), ln2_b.reshape(1, D), tmC=tmC, K=K, alpha=alpha, S=S)
    return out.reshape(B, S, D)


def kernel(x, c, w_ada, b_ada, w_in, kv_norm_g, w_uk, w_uv, w_pool_group, pool_scale, w_branch_attn,
           w_branch_pool, w_out, ln1_g, ln1_b, w_router, b_router, w_gate_up, b_gate_up, w_down, b_down,
           ln2_g, ln2_b):
    B, S, D = x.shape
    depth = w_ada.shape[0]
    alpha = (2.0 * depth) ** 0.25
    for l in range(depth):
        mod = _ada(c, w_ada[l], b_ada[l]).reshape(B, N_MOD, D)
        x = _layer(x, mod, w_in[l], kv_norm_g[l], w_uk[l], w_uv[l], w_pool_group[l], pool_scale[l],
                   w_branch_attn[l], w_branch_pool[l], w_out[l], ln1_g[l], ln1_b[l], w_router[l],
                   b_router[l], w_gate_up[l], b_gate_up[l], w_down[l], b_down[l], ln2_g[l], ln2_b[l],
                   alpha=alpha)
    return x
```

```python
import functools

import jax
import jax.numpy as jnp
from jax import lax
from jax.experimental import pallas as pl
from jax.experimental.pallas import tpu as pltpu

F32 = jnp.float32
BF16 = jnp.bfloat16
I32 = jnp.int32
HIGHEST = lax.Precision.HIGHEST

LN_EPS = 1e-5
TOPK_MAX = 256
TOP_K_EXPERTS = 4
POOL_WINDOWS = (2, 4, 8, 16)
SWIGLU_LIMIT = 7.0
SWIGLU_ALPHA = 1.702
N_MOD = 6
INT_MIN = -(2 ** 31)
NEG_BIG = -1e30
LANES = 128
SUBLANES = 8
SORT_GROUP = 16
TIE_CHUNK = 256
POOL_HALO = 16
ONES_ROWS = 16
LOG2E = 1.4426950408889634
ATTN_KEY_CHUNK = 256
ATTN_HEAD_UNROLL = 4


def _store_row_slabs(ref, x):
    n, D = x.shape
    sub = D // LANES
    for s in range(sub):
        ref[pl.ds(s, n, stride=sub), :] = x[:, s * LANES:(s + 1) * LANES]


def _load_row_slabs(ref, n, sub):
    return jnp.concatenate([ref[pl.ds(s, n, stride=sub), :] for s in range(sub)], axis=1)


def _layer_norm(z, g, b):
    mu = jnp.mean(z, axis=-1, keepdims=True)
    zc = z - mu
    var = jnp.mean(zc * zc, axis=-1, keepdims=True)
    return zc * lax.rsqrt(var + LN_EPS) * g + b


def _ada_kernel(c_ref, w_ref, b_ref, o_ref):
    c = c_ref[...]
    cond = c * jax.nn.sigmoid(c)
    o_ref[...] = jnp.dot(cond, w_ref[...], precision=HIGHEST, preferred_element_type=F32) + b_ref[...]


def _ada(c, w_ada, b_ada):
    B, D = c.shape
    N = w_ada.shape[1]
    tn = D
    return pl.pallas_call(
        _ada_kernel,
        grid=(N // tn,),
        in_specs=[pl.BlockSpec((B, D), lambda j: (0, 0)),
                  pl.BlockSpec((D, tn), lambda j: (0, j)),
                  pl.BlockSpec((1, tn), lambda j: (0, j))],
        out_specs=pl.BlockSpec((B, tn), lambda j: (0, j)),
        out_shape=jax.ShapeDtypeStruct((B, N), F32),
    )(c, w_ada, b_ada.reshape(1, N))


def _inproj_kernel(x_ref, mod_ref, wq, wckv, wqi, wki, wwi, wu, wg, kvg_ref, wukt_ref,
                   qlt_o, ckva_o, ckvt_o, qit_o, ki_o, wit_o, u_o, g_o,
                   *, tm, H, dh, Hi, q_scale, qi_scale, wi_scale):
    i = pl.program_id(1)
    x = x_ref[0]
    mod = mod_ref[0]
    h = (x * (1.0 + mod[1:2]) + mod[0:1]).astype(BF16)

    def proj(w):
        return jnp.dot(h, w[...], preferred_element_type=F32)

    qt = proj(wq).T.astype(BF16)
    for hh in range(H):
        qlt = jnp.dot(wukt_ref[hh], qt[hh * dh:(hh + 1) * dh, :], preferred_element_type=F32) * q_scale
        qlt_o[0, hh] = qlt.astype(BF16)

    ckv = proj(wckv)
    ckv = ckv * lax.rsqrt(jnp.mean(ckv * ckv, axis=-1, keepdims=True) + LN_EPS) * kvg_ref[...]
    ckvt_o[0, 0:ckv.shape[1], :] = ckv.T.astype(BF16)
    ckvt_o[0, ckv.shape[1]:, :] = jnp.ones((ONES_ROWS, tm), BF16)
    R = ckv.shape[1]
    pos = i * tm + lax.broadcasted_iota(I32, (tm, R), 0)
    lane = lax.broadcasted_iota(I32, (tm, R), 1)
    pos_hi = ((pos >> 8) << 8).astype(F32)
    pos_lo = (pos & 255).astype(F32)
    extra = jnp.where((lane == 0) | (lane == 2), pos_hi, jnp.where((lane == 1) | (lane == 3), pos_lo, 0.0))
    ckva_o[0, :, 0:R] = ckv.astype(BF16)
    ckva_o[0, :, R:2 * R] = extra.astype(BF16)

    qit_o[0] = (proj(wqi) * qi_scale).T.astype(BF16)
    ki_o[0] = proj(wki).astype(BF16)
    wit_o[0] = (proj(wwi) * wi_scale).T[0:Hi, :]
    u_o[0] = proj(wu)
    g_o[0] = proj(wg).astype(BF16)


def _inproj(x, mod, ws, kv_norm_g, wukt, *, q_scale, qi_scale, wi_scale, Hi, tm):
    B, S, D = x.shape
    wq, wckv, wqi, wki, wwi, wu, wg = ws
    H, R, dh = wukt.shape
    row = lambda n: pl.BlockSpec((1, tm, n), lambda b, i: (b, i, 0))
    col = lambda n: pl.BlockSpec((1, n, tm), lambda b, i: (b, 0, i))
    full = lambda a: pl.BlockSpec(a.shape, lambda b, i: (0,) * a.ndim)
    out_specs = [pl.BlockSpec((1, H, R, tm), lambda b, i: (b, 0, 0, i)), row(2 * R), col(R + ONES_ROWS),
                 col(wqi.shape[1]),
                 row(wki.shape[1]), col(Hi), row(wu.shape[1]), row(wg.shape[1])]
    out_shape = [jax.ShapeDtypeStruct((B, H, R, S), BF16), jax.ShapeDtypeStruct((B, S, 2 * R), BF16),
                 jax.ShapeDtypeStruct((B, R + ONES_ROWS, S), BF16),
                 jax.ShapeDtypeStruct((B, wqi.shape[1], S), BF16),
                 jax.ShapeDtypeStruct((B, S, wki.shape[1]), BF16), jax.ShapeDtypeStruct((B, Hi, S), F32),
                 jax.ShapeDtypeStruct((B, S, wu.shape[1]), F32), jax.ShapeDtypeStruct((B, S, wg.shape[1]), BF16)]
    return pl.pallas_call(
        functools.partial(_inproj_kernel, tm=tm, H=H, dh=dh, Hi=Hi, q_scale=q_scale, qi_scale=qi_scale,
                          wi_scale=wi_scale),
        grid=(B, S // tm),
        in_specs=[row(D), pl.BlockSpec((1, N_MOD, D), lambda b, i: (b, 0, 0))]
                 + [full(w) for w in ws] + [full(kv_norm_g), full(wukt)],
        out_specs=out_specs,
        out_shape=out_shape,
        compiler_params=pltpu.CompilerParams(
            dimension_semantics=("parallel", "parallel"), vmem_limit_bytes=48 * 2 ** 20),
    )(x, mod, *ws, kv_norm_g, wukt)


def _key_to_f32(t):
    return pltpu.bitcast(jnp.where(t >= 0, t, t ^ 0x7FFFFFFF), F32)


def _sort_network(n):
    pairs, p = [], 1
    while p < n:
        k = p
        while k >= 1:
            for j in range(k % p, n - k, 2 * k):
                for i in range(min(k, n - j - k)):
                    if (i + j) // (2 * p) == (i + j + k) // (2 * p):
                        pairs.append((i + j, i + j + k))
            k //= 2
        p *= 2
    return pairs


def _count_ge_sorted(blocks, cand):
    n = len(blocks)

    def pick(level, bits):
        if len(level) == 1:
            return level[0]
        half = len(level) // 2
        return jnp.where(bits[0], pick(level[half:], bits[1:]), pick(level[:half], bits[1:]))

    total = jnp.where(blocks[n - 1] >= cand, 1.0, 0.0)
    bits = []
    step = n // 2
    while step >= 1:
        level = [blocks[c + step - 1] for c in range(0, n, 2 * step)]
        b = pick(level, bits) >= cand
        bits.append(b)
        total = total + jnp.where(b, float(step), 0.0)
        step //= 2
    return total


def _attn_body(qit_ref, wit_ref, ki_ref, qlt_ref, aug_ref, ckva_ref, ckvt_ref, wuvt_ref, yt_ref,
               score_scr, bias_scr, lg_scr, o_scr, sort_scr, *, W, tq, S, H, dh, Hi, di, topk, cw):
    j = pl.program_id(1)
    qpos = j * tq + lax.broadcasted_iota(I32, (1, tq), 1)
    score = score_scr.at[0:W, :]

    wit = wit_ref[0]
    for c in range(W // cw):
        kic = ki_ref[0, c * cw:(c + 1) * cw, :]
        acc = jnp.zeros((cw, tq), F32)
        for h in range(Hi):
            d = jnp.dot(kic, qit_ref[0, h * di:(h + 1) * di, :], preferred_element_type=F32)
            acc = acc + wit[h:h + 1, :] * jnp.maximum(d, 0.0)
        kpos_c = c * cw + lax.broadcasted_iota(I32, (cw, tq), 0)
        score_scr[c * cw:(c + 1) * cw, :] = jnp.where(kpos_c <= qpos, acc, -jnp.inf)

    def count(hit):
        return jnp.sum(jnp.where(hit, 1.0, 0.0), axis=0, keepdims=True)

    Wb = W // SORT_GROUP
    net = _sort_network(SORT_GROUP)

    def column(r, lg):
        return [(slice(g * Wb + r * SUBLANES, g * Wb + (r + 1) * SUBLANES), slice(lg * LANES, (lg + 1) * LANES))
                for g in range(SORT_GROUP)]

    for r in range(Wb // SUBLANES):
        for lg in range(tq // LANES):
            v = [score_scr[idx] for idx in column(r, lg)]
            for a, b in net:
                v[a], v[b] = jnp.maximum(v[a], v[b]), jnp.minimum(v[a], v[b])
            for idx, val in zip(column(r, lg), v):
                sort_scr[idx] = val

    def count_ge(cand_f):
        parts = []
        for lg in range(tq // LANES):
            c = cand_f[:, lg * LANES:(lg + 1) * LANES]
            acc = None
            for r in range(Wb // SUBLANES):
                n = _count_ge_sorted([sort_scr[idx] for idx in column(r, lg)], c)
                acc = n if acc is None else acc + n
            parts.append(jnp.sum(acc, axis=0, keepdims=True))
        return jnp.concatenate(parts, axis=1)

    def thr_step(i, t):
        cand = t + lax.shift_left(jnp.int32(1), 31 - i)
        cnt = count_ge(_key_to_f32(cand))
        return jnp.where(cnt >= float(topk), cand, t)

    t = lax.fori_loop(0, 32, thr_step, jnp.full((1, tq), INT_MIN, I32))
    few = t == INT_MIN
    thr_f = _key_to_f32(t)
    up_f = _key_to_f32(t + 1)
    need = float(topk) - count(score[...] >= up_f)

    tc = min(TIE_CHUNK, W)
    earlier = (lax.broadcasted_iota(I32, (tc, tc), 1) < lax.broadcasted_iota(I32, (tc, tc), 0)).astype(BF16)
    tied_before = jnp.zeros((1, tq), F32)
    for c in range(W // tc):
        rows = slice(c * tc, (c + 1) * tc)
        s = score_scr[rows, :]
        gt = s >= up_f
        eq = (s >= thr_f) & jnp.logical_not(gt)
        eq_f = jnp.where(eq, 1.0, 0.0)
        rank = jnp.dot(earlier, eq_f.astype(BF16), preferred_element_type=F32) + tied_before
        tied_before = tied_before + jnp.sum(eq_f, axis=0, keepdims=True)
        kpos = c * tc + lax.broadcasted_iota(I32, (tc, tq), 0)
        sel = (few | gt | (eq & (rank < need))) & (kpos <= qpos)
        bias_scr[rows, :] = jnp.where(sel, 0.0, NEG_BIG)

    R = qlt_ref.shape[2]
    ck = min(ATTN_KEY_CHUNK, W)

    def head(h, carry):
        qa = jnp.concatenate([qlt_ref[0, h], aug_ref[h]], axis=0)
        m = None
        for c in range(W // ck):
            keys = slice(c * ck, (c + 1) * ck)
            lg = jnp.dot(ckva_ref[0, keys, :], qa, preferred_element_type=F32) + bias_scr[keys, :]
            lg_scr[keys, :] = lg
            mc = jnp.max(lg, axis=0, keepdims=True)
            m = mc if m is None else jnp.maximum(m, mc)
        acc = jnp.zeros((R + ONES_ROWS, tq), F32)
        for c in range(W // ck):
            keys = slice(c * ck, (c + 1) * ck)
            p = jnp.exp2(lg_scr[keys, :] - m)
            acc = acc + jnp.dot(ckvt_ref[0, :, keys], p.astype(BF16), preferred_element_type=F32)
        o_scr[h] = (acc[0:R] / acc[R:R + 1]).astype(BF16)
        return carry

    lax.fori_loop(0, H, head, 0, unroll=ATTN_HEAD_UNROLL)
    for h in range(H):
        yt = jnp.dot(wuvt_ref[h], o_scr[h], preferred_element_type=F32)
        yt_ref[0, h * dh:(h + 1) * dh, :] = yt.astype(BF16)


def _attn_kernel(*refs, tq, cw, S, **kw):
    j = pl.program_id(1)
    nc = ((j + 1) * tq + cw - 1) // cw
    for w in range(1, S // cw + 1):
        @pl.when(nc == w)
        def _():
            _attn_body(*refs, W=w * cw, tq=tq, S=S, cw=cw, **kw)


def _attention(qit, wit, ki, qlt, ckva, ckvt, wuvt, *, tq, topk):
    B, H, R, S = qlt.shape
    dh = wuvt.shape[1]
    di = ki.shape[2]
    Hi = wit.shape[1]
    slopes = jnp.asarray([LOG2E * 2.0 ** (-8.0 * (h + 1) / H) for h in range(H)], F32)
    s_hi = slopes.astype(BF16)
    s_lo = (slopes - s_hi.astype(F32)).astype(BF16)
    aug = jnp.zeros((H, R, tq), BF16)
    aug = aug.at[:, 0:2, :].set(s_hi[:, None, None]).at[:, 2:4, :].set(s_lo[:, None, None])
    kern = functools.partial(_attn_kernel, tq=tq, S=S, H=H, dh=dh, Hi=Hi, di=di, topk=topk, cw=min(S, 512))
    col = lambda n: pl.BlockSpec((1, n, tq), lambda b, j: (b, 0, j))
    full = lambda a: pl.BlockSpec(a.shape, lambda b, j: (0,) * a.ndim)
    return pl.pallas_call(
        kern,
        grid=(B, S // tq),
        in_specs=[col(qit.shape[1]), col(Hi), pl.BlockSpec((1, S, di), lambda b, j: (b, 0, 0)),
                  pl.BlockSpec((1, H, R, tq), lambda b, j: (b, 0, 0, j)), full(aug),
                  pl.BlockSpec((1, S, 2 * R), lambda b, j: (b, 0, 0)),
                  pl.BlockSpec((1, R + ONES_ROWS, S), lambda b, j: (b, 0, 0)), full(wuvt)],
        out_specs=col(H * dh),
        out_shape=jax.ShapeDtypeStruct((B, H * dh, S), BF16),
        scratch_shapes=[pltpu.VMEM((S, tq), F32), pltpu.VMEM((S, tq), F32), pltpu.VMEM((S, tq), F32),
                        pltpu.VMEM((H, R, tq), BF16), pltpu.VMEM((S, tq), F32)],
        compiler_params=pltpu.CompilerParams(
            dimension_semantics=("parallel", "parallel"), vmem_limit_bytes=48 * 2 ** 20),
    )(qit, wit, ki, qlt, aug, ckva, ckvt, wuvt)


def _merge_kernel(x_ref, yat_ref, u_ref, halo_ref, g_ref, mod_ref, wpg_ref, psc_ref, wba_ref, wbp_ref,
                  wout_ref, ln1g_ref, ln1b_ref, wr_ref, br_ref,
                  x1_o, h2_o, te_o, tg_o, ext_scr, *, tm, D, G, alpha, E, K):
    i = pl.program_id(1)

    @pl.when(i == 0)
    def _():
        ext_scr[0:POOL_HALO, :] = jnp.zeros((POOL_HALO, ext_scr.shape[1]), F32)

    @pl.when(i > 0)
    def _():
        ext_scr[0:POOL_HALO, :] = halo_ref[0]

    ext_scr[POOL_HALO:POOL_HALO + tm, :] = u_ref[0]

    pos = i * tm + lax.broadcasted_iota(I32, (tm, 1), 0)
    ys = []
    for g, w in enumerate(POOL_WINDOWS):
        cols = slice(g * G, (g + 1) * G)
        cur = ext_scr[POOL_HALO:POOL_HALO + tm, cols]
        acc = cur
        for back in range(1, w):
            acc = acc + ext_scr[POOL_HALO - back:POOL_HALO - back + tm, cols]
        cnt = jnp.minimum(pos + 1, w).astype(F32)
        pooled = acc / cnt - cur
        ys.append(jnp.dot(pooled.astype(BF16), wpg_ref[g], preferred_element_type=F32))
    y_pool = jnp.concatenate(ys, axis=1) * psc_ref[...]

    gates = g_ref[0]
    ya = yat_ref[0].astype(F32).T.astype(BF16)
    a = jnp.dot(ya, wba_ref[...], preferred_element_type=F32)
    p = jnp.dot(y_pool.astype(BF16), wbp_ref[...], preferred_element_type=F32)
    mix = jax.nn.sigmoid(gates[:, :D].astype(F32)) * a + jax.nn.sigmoid(gates[:, D:].astype(F32)) * p
    o = jnp.dot(mix.astype(BF16), wout_ref[...], preferred_element_type=F32)
    mod = mod_ref[0]
    x1 = _layer_norm(alpha * x_ref[0] + mod[2:3] * o, ln1g_ref[...], ln1b_ref[...])
    x1_o[0] = x1
    h2 = x1 * (1.0 + mod[4:5]) + mod[3:4]
    _store_row_slabs(h2_o.at[0], h2)

    h2_hi = h2.astype(BF16)
    h2_lo = (h2 - h2_hi.astype(F32)).astype(BF16)
    logits = (jnp.dot(h2_hi, wr_ref[0], preferred_element_type=F32)
              + jnp.dot(h2_hi, wr_ref[1], preferred_element_type=F32)
              + jnp.dot(h2_lo, wr_ref[0], preferred_element_type=F32)) + br_ref[...]
    lane = lax.broadcasted_iota(I32, logits.shape, 1)
    logits = jnp.where(lane < E, logits, -jnp.inf)
    te = jnp.zeros(logits.shape, I32)
    vals = []
    for k in range(K):
        m = jnp.max(logits, axis=1, keepdims=True)
        idx = jnp.min(jnp.where(logits == m, lane, LANES), axis=1, keepdims=True)
        te = jnp.where(lane == k, idx, te)
        vals.append(m)
        logits = jnp.where(lane == idx, -jnp.inf, logits)
    ex = [jnp.exp(v - vals[0]) for v in vals]
    den = ex[0]
    for e in ex[1:]:
        den = den + e
    tg = jnp.zeros(logits.shape, F32)
    for k in range(K):
        tg = jnp.where(lane == k, ex[k] / den, tg)
    te_o[0] = te
    tg_o[0] = tg


def _merge(x, yat, u, g, mod, wpg, psc, wba, wbp, wout, ln1g, ln1b, wr, br, *, tm, alpha, E, K):
    B, S, D = x.shape
    PW = u.shape[2]
    G = wpg.shape[1]
    kern = functools.partial(_merge_kernel, tm=tm, D=D, G=G, alpha=alpha, E=E, K=K)
    row = lambda n: pl.BlockSpec((1, tm, n), lambda b, i: (b, i, 0))
    full = lambda a: pl.BlockSpec(a.shape, lambda b, i: (0,) * a.ndim)
    hb = tm // POOL_HALO
    halo = pl.BlockSpec((1, POOL_HALO, PW), lambda b, i: (b, jnp.maximum(i * hb - 1, 0), 0))
    return pl.pallas_call(
        kern,
        grid=(B, S // tm),
        in_specs=[row(D), pl.BlockSpec((1, yat.shape[1], tm), lambda b, i: (b, 0, i)), row(PW), halo,
                  row(g.shape[2]), pl.BlockSpec((1, N_MOD, D), lambda b, i: (b, 0, 0)),
                  full(wpg), full(psc), full(wba), full(wbp), full(wout), full(ln1g), full(ln1b),
                  full(wr), full(br)],
        out_specs=[row(D), pl.BlockSpec((1, tm * (D // LANES), LANES), lambda b, i: (b, i, 0)),
                   row(LANES), row(LANES)],
        out_shape=[jax.ShapeDtypeStruct((B, S, D), F32),
                   jax.ShapeDtypeStruct((B, S * (D // LANES), LANES), F32),
                   jax.ShapeDtypeStruct((B, S, LANES), I32), jax.ShapeDtypeStruct((B, S, LANES), F32)],
        scratch_shapes=[pltpu.VMEM((POOL_HALO + tm, PW), F32)],
        compiler_params=pltpu.CompilerParams(
            dimension_semantics=("parallel", "arbitrary"), vmem_limit_bytes=48 * 2 ** 20),
    )(x, yat, u, u, g, mod, wpg, psc, wba, wbp, wout, ln1g, ln1b, wr, br)


def _route_kernel(te_ref, rank_o, cnt_o, base_scr, *, tR, K):
    i = pl.program_id(0)

    @pl.when(i == 0)
    def _():
        base_scr[...] = jnp.zeros(base_scr.shape, F32)

    te = te_ref[...]
    lane = lax.broadcasted_iota(I32, te.shape, 1)
    hits = [lane == te[:, k:k + 1] for k in range(K)]
    onehot = hits[0].astype(F32)
    for k in range(1, K):
        onehot = onehot + hits[k].astype(F32)
    earlier = (lax.broadcasted_iota(I32, (tR, tR), 0) > lax.broadcasted_iota(I32, (tR, tR), 1)).astype(BF16)
    before = jnp.dot(earlier, onehot.astype(BF16), preferred_element_type=F32) + base_scr[...]
    rank = jnp.zeros(te.shape, I32)
    for k in range(K):
        rk = jnp.sum(jnp.where(hits[k], before, 0.0), axis=1, keepdims=True)
        rank = jnp.where(lane == k, rk.astype(I32), rank)
    rank_o[...] = rank
    base_scr[...] = base_scr[...] + jnp.sum(onehot, axis=0, keepdims=True)
    cnt_o[...] = base_scr[...].astype(I32)


def _route(te, *, tR, K):
    T = te.shape[0]
    return pl.pallas_call(
        functools.partial(_route_kernel, tR=tR, K=K),
        grid=(T // tR,),
        in_specs=[pl.BlockSpec((tR, LANES), lambda i: (i, 0))],
        out_specs=[pl.BlockSpec((tR, LANES), lambda i: (i, 0)), pl.BlockSpec((1, LANES), lambda i: (0, 0))],
        out_shape=[jax.ShapeDtypeStruct((T, LANES), I32), jax.ShapeDtypeStruct((1, LANES), I32)],
        scratch_shapes=[pltpu.VMEM((1, LANES), F32)],
        compiler_params=pltpu.CompilerParams(dimension_semantics=("arbitrary",)),
    )(te)


def _dispatch_kernel(pe_ref, nu_ref, pos_ref, h2_ref, xs_out, zero_scr, sem, zsem,
                     *, tD, K, sub, E, tmE, n_blocks):
    blk = tmE * sub

    @pl.when(pl.program_id(0) == 0)
    def _():
        zero_scr[...] = jnp.zeros(zero_scr.shape, zero_scr.dtype)

        def zero_block(b):
            return pltpu.make_async_copy(zero_scr, xs_out.at[pl.ds(pl.multiple_of(b * blk, blk), blk), :], zsem)

        def last_block(e):
            return jnp.maximum(pe_ref[e] // tmE - 1, 0)

        def start_last(e, carry):
            zero_block(last_block(e)).start()
            return carry

        def start_tail(b, carry):
            zero_block(b).start()
            return carry

        def wait_last(e, carry):
            zero_block(last_block(e)).wait()
            return carry

        def wait_tail(b, carry):
            zero_block(b).wait()
            return carry

        lax.fori_loop(0, E, start_last, 0)
        lax.fori_loop(nu_ref[0], n_blocks, start_tail, 0)
        lax.fori_loop(0, E, wait_last, 0)
        lax.fori_loop(nu_ref[0], n_blocks, wait_tail, 0)

    def issue(r, carry):
        src = h2_ref.at[pl.ds(pl.multiple_of(r * sub, sub), sub), :]
        for k in range(K):
            p = pl.multiple_of(pos_ref[0, 0, k * tD + r], sub)
            pltpu.async_copy(src, xs_out.at[pl.ds(p, sub), :], sem, priority=k % 2)
        return carry

    lax.fori_loop(0, tD, issue, 0, unroll=8)
    for k in range(K):
        pltpu.make_async_copy(h2_ref, xs_out.at[pl.ds(0, tD * sub), :], sem).wait()


def _dispatch(pad_end, n_used, pos, h2, n_blocks, *, tD, tmE, K, sub):
    T = h2.shape[0] // sub
    E = pad_end.shape[0]
    grid_spec = pltpu.PrefetchScalarGridSpec(
        num_scalar_prefetch=2,
        grid=(T // tD,),
        in_specs=[pl.BlockSpec((1, 1, K * tD), lambda i, pe, nu: (i, 0, 0), memory_space=pltpu.SMEM),
                  pl.BlockSpec((tD * sub, LANES), lambda i, pe, nu: (i, 0))],
        out_specs=pl.BlockSpec(memory_space=pl.ANY),
        scratch_shapes=[pltpu.VMEM((tmE * sub, LANES), h2.dtype), pltpu.SemaphoreType.DMA(()),
                        pltpu.SemaphoreType.DMA(())],
    )
    return pl.pallas_call(
        functools.partial(_dispatch_kernel, tD=tD, K=K, sub=sub, E=E, tmE=tmE, n_blocks=n_blocks),
        grid_spec=grid_spec,
        out_shape=jax.ShapeDtypeStruct((n_blocks * tmE * sub, LANES), h2.dtype),
        compiler_params=pltpu.CompilerParams(
            dimension_semantics=("arbitrary",), has_side_effects=True),
    )(pad_end, n_used, pos, h2)


def _expert_kernel(be_ref, nu_ref, nx_ref, sl_ref, x_ref, wgu_hbm, bgu_ref, wd_hbm, bd_ref, y_ref,
                   wgu_f32, wd_f32, wgu_bf, wd_bf, sem, *, F, tmE, sub):
    i = pl.program_id(0)
    used = i < nu_ref[0]
    e = be_ref[i]
    slot = sl_ref[i]
    new_expert = used & ((i == 0) | (e != be_ref[jnp.maximum(i - 1, 0)]))

    def fetch(expert, s):
        return (pltpu.make_async_copy(wgu_hbm.at[expert], wgu_f32.at[s], sem.at[s, 0]),
                pltpu.make_async_copy(wd_hbm.at[expert], wd_f32.at[s], sem.at[s, 1]))

    @pl.when(used & (i == 0))
    def _():
        for cp in fetch(e, slot):
            cp.start()

    @pl.when(new_expert)
    def _():
        for cp in fetch(e, slot):
            cp.wait()

        @pl.when(nx_ref[i] >= 0)
        def _():
            for cp in fetch(nx_ref[i], 1 - slot):
                cp.start()

        wgu_bf[...] = wgu_f32[slot].astype(BF16)
        wd_bf[...] = wd_f32[slot].astype(BF16)

    @pl.when(used)
    def _():
        x = _load_row_slabs(x_ref, tmE, sub).astype(BF16)
        gu = jnp.dot(x, wgu_bf[...], preferred_element_type=F32) + bgu_ref[0]
        glu = jnp.minimum(gu[:, :F], SWIGLU_LIMIT)
        lin = jnp.clip(gu[:, F:], -SWIGLU_LIMIT, SWIGLU_LIMIT)
        act = glu * jax.nn.sigmoid(SWIGLU_ALPHA * glu) * (lin + 1.0)
        y = jnp.dot(act.astype(BF16), wd_bf[...], preferred_element_type=F32) + bd_ref[0]
        _store_row_slabs(y_ref, y)

    @pl.when(jnp.logical_not(used))
    def _():
        y_ref[...] = jnp.zeros(y_ref.shape, F32)


def _experts(block_e, n_used, next_e, slot, xs, wgu, bgu, wd, bd, *, tmE):
    E, D, F2 = wgu.shape
    F = F2 // 2
    sub = D // LANES
    n_blocks = xs.shape[0] // (tmE * sub)
    slab_block = pl.BlockSpec((tmE * sub, LANES), lambda i, be, nu, nx, sl: (i, 0))
    grid_spec = pltpu.PrefetchScalarGridSpec(
        num_scalar_prefetch=4,
        grid=(n_blocks,),
        in_specs=[slab_block,
                  pl.BlockSpec(memory_space=pl.ANY),
                  pl.BlockSpec((1, 1, F2), lambda i, be, nu, nx, sl: (be[i], 0, 0)),
                  pl.BlockSpec(memory_space=pl.ANY),
                  pl.BlockSpec((1, 1, D), lambda i, be, nu, nx, sl: (be[i], 0, 0))],
        out_specs=slab_block,
        scratch_shapes=[pltpu.VMEM((2, D, F2), F32), pltpu.VMEM((2, F, D), F32),
                        pltpu.VMEM((D, F2), BF16), pltpu.VMEM((F, D), BF16),
                        pltpu.SemaphoreType.DMA((2, 2))],
    )
    return pl.pallas_call(
        functools.partial(_expert_kernel, F=F, tmE=tmE, sub=sub),
        grid_spec=grid_spec,
        out_shape=jax.ShapeDtypeStruct(xs.shape, F32),
        compiler_params=pltpu.CompilerParams(
            dimension_semantics=("arbitrary",), vmem_limit_bytes=56 * 2 ** 20),
    )(block_e, n_used, next_e, slot, xs, wgu, bgu, wd, bd)


def _combine_kernel(pos_ref, nxt_ref, y_hbm, x1_ref, tg_ref, mod_ref, g_ref, b_ref, o_ref, buf, sem,
                    *, tmC, K, alpha, nt, sub):
    i = pl.program_id(0)
    slot = lax.rem(i, 2)

    def fetch(p_ref, s):
        def issue(r, carry):
            dst_rows = pl.ds(pl.multiple_of(r * sub, sub), sub)
            for k in range(K):
                p = pl.multiple_of(p_ref[0, 0, k * tmC + r], sub)
                pltpu.async_copy(y_hbm.at[pl.ds(p, sub), :], buf.at[s, k, dst_rows, :], sem.at[s],
                                 priority=k % 2)
            return carry

        lax.fori_loop(0, tmC, issue, 0, unroll=8)

    @pl.when(i == 0)
    def _():
        fetch(pos_ref, slot)

    @pl.when(i + 1 < nt)
    def _():
        fetch(nxt_ref, 1 - slot)

    for k in range(K):
        pltpu.make_async_copy(y_hbm.at[pl.ds(0, tmC * sub), :], buf.at[slot, k], sem.at[slot]).wait()

    tg = tg_ref[...]
    y = tg[:, 0:1] * _load_row_slabs(buf.at[slot, 0], tmC, sub)
    for k in range(1, K):
        y = y + tg[:, k:k + 1] * _load_row_slabs(buf.at[slot, k], tmC, sub)
    mod = mod_ref[0]
    o_ref[...] = _layer_norm(alpha * x1_ref[...] + mod[5:6] * y, g_ref[...], b_ref[...])


def _combine(pos, y_sorted, x1, tg, mod, ln2g, ln2b, *, tmC, K, alpha, S):
    T, D = x1.shape
    nt = T // tmC
    per_seq = S // tmC
    sub = D // LANES
    return pl.pallas_call(
        functools.partial(_combine_kernel, tmC=tmC, K=K, alpha=alpha, nt=nt, sub=sub),
        grid=(nt,),
        in_specs=[pl.BlockSpec((1, 1, K * tmC), lambda i: (i, 0, 0), memory_space=pltpu.SMEM),
                  pl.BlockSpec((1, 1, K * tmC), lambda i: (jnp.minimum(i + 1, nt - 1), 0, 0),
                               memory_space=pltpu.SMEM),
                  pl.BlockSpec(memory_space=pl.ANY),
                  pl.BlockSpec((tmC, D), lambda i: (i, 0)),
                  pl.BlockSpec((tmC, LANES), lambda i: (i, 0)),
                  pl.BlockSpec((1, N_MOD, D), lambda i: (i // per_seq, 0, 0)),
                  pl.BlockSpec((1, D), lambda i: (0, 0)),
                  pl.BlockSpec((1, D), lambda i: (0, 0))],
        out_specs=pl.BlockSpec((tmC, D), lambda i: (i, 0)),
        out_shape=jax.ShapeDtypeStruct((T, D), F32),
        scratch_shapes=[pltpu.VMEM((2, K, tmC * sub, LANES), F32), pltpu.SemaphoreType.DMA((2,))],
        compiler_params=pltpu.CompilerParams(
            dimension_semantics=("arbitrary",), vmem_limit_bytes=48 * 2 ** 20),
    )(pos, pos, y_sorted, x1, tg, mod, ln2g, ln2b)


def _routing_tables(top_e, rank, counts, E, tmE, sub):
    T, K = top_e.shape
    padded = (counts + tmE - 1) // tmE * tmE
    pad_end = jnp.cumsum(padded)
    pad_start = pad_end - padded
    n_blocks = -(-(T * K) // tmE) + E
    block_start = jnp.arange(n_blocks, dtype=I32) * tmE
    block_e = jnp.minimum(jnp.sum((pad_end[None, :] <= block_start[:, None]).astype(I32), axis=1), E - 1)
    n_used = (pad_end[-1] // tmE).astype(I32).reshape(1)
    pos = (jnp.take(pad_start, top_e) + rank).astype(I32) * sub
    nonempty = counts > 0
    ids = jnp.arange(E, dtype=I32)
    at_or_after = jnp.flip(lax.cummin(jnp.flip(jnp.where(nonempty, ids, E))))
    after = jnp.concatenate([at_or_after[1:], jnp.full((1,), E, I32)])
    next_e = jnp.where(after < E, after, -1)[block_e].astype(I32)
    slot = ((jnp.cumsum(nonempty.astype(I32)) - 1) % 2)[block_e].astype(I32)
    return block_e.astype(I32), n_used, next_e, slot, pad_end.astype(I32), pos, n_blocks


def _tile_major(pos, tile):
    T, K = pos.shape
    return pos.reshape(T // tile, tile, K).transpose(0, 2, 1).reshape(T // tile, 1, K * tile)


def _layer(x, mod, w_in, kv_norm_g, w_uk, w_uv, w_pool_group, pool_scale, w_branch_attn, w_branch_pool,
           w_out, ln1_g, ln1_b, w_router, b_router, w_gate_up, b_gate_up, w_down, b_down, ln2_g, ln2_b,
           *, alpha):
    B, S, D = x.shape
    R, H, dh = w_uk.shape
    AW = H * dh
    PW = pool_scale.shape[0]
    E = w_router.shape[1]
    K = TOP_K_EXPERTS
    n_idx = w_in.shape[1] - (AW + R + PW + 2 * D)
    di = dh
    Hi = (n_idx - di) // (di + 1)
    sizes = (AW, R, Hi * di, di, Hi, PW, 2 * D)
    offs = [0]
    for s in sizes:
        offs.append(offs[-1] + s)
    ws = [w_in[:, offs[k]:offs[k + 1]].astype(BF16) for k in range(len(sizes))]
    ws[4] = jnp.pad(ws[4], ((0, 0), (0, LANES - Hi)))

    wukt = w_uk.transpose(1, 0, 2).astype(BF16)
    wuvt = w_uv.transpose(1, 2, 0).astype(BF16)
    qlt, ckva, ckvt, qit, ki, wit, u, g = _inproj(
        x, mod, ws, kv_norm_g.reshape(1, R), wukt, q_scale=dh ** -0.5 * LOG2E, qi_scale=di ** -0.5,
        wi_scale=Hi ** -0.5, Hi=Hi, tm=min(S, 512))

    topk = min(TOPK_MAX, S // 4)
    y_attn_t = _attention(qit, wit, ki, qlt, ckva, ckvt, wuvt, tq=min(S, 256), topk=topk)

    wr = jnp.pad(w_router, ((0, 0), (0, LANES - E)))
    wr_hi = wr.astype(BF16)
    wr = jnp.stack([wr_hi, (wr - wr_hi.astype(F32)).astype(BF16)])
    br = jnp.pad(b_router, (0, LANES - E)).reshape(1, LANES)
    x1, h2, te, tg = _merge(x, y_attn_t, u, g, mod, w_pool_group.astype(BF16), pool_scale.reshape(1, PW),
                            w_branch_attn.astype(BF16), w_branch_pool.astype(BF16), w_out.astype(BF16),
                            ln1_g.reshape(1, D), ln1_b.reshape(1, D), wr, br,
                            tm=min(S, 256), alpha=alpha, E=E, K=K)

    T = B * S
    tmE = 512
    tmC = min(S, 256)
    tD = min(T, 1024)
    sub = D // LANES
    te = te.reshape(T, LANES)
    rank, counts = _route(te, tR=min(T, 512), K=K)
    block_e, n_used, next_e, slot, pad_end, pos, n_blocks = _routing_tables(
        te[:, :K], rank[:, :K], counts[0, :E], E, tmE, sub)
    xs = _dispatch(pad_end, n_used, _tile_major(pos, tD), h2.reshape(T * sub, LANES), n_blocks,
                   tD=tD, tmE=tmE, K=K, sub=sub)
    y_sorted = _experts(block_e, n_used, next_e, slot, xs, w_gate_up, b_gate_up.reshape(E, 1, -1),
                        w_down, b_down.reshape(E, 1, D), tmE=tmE)
    out = _combine(_tile_major(pos, tmC), y_sorted, x1.reshape(T, D), tg.reshape(T, LANES), mod,
                   ln2_g.reshape(1, D), ln2_b.reshape(1, D), tmC=tmC, K=K, alpha=alpha, S=S)
    return out.reshape(B, S, D)


def kernel(x, c, w_ada, b_ada, w_in, kv_norm_g, w_uk, w_uv, w_pool_group, pool_scale, w_branch_attn,
           w_branch_pool, w_out, ln1_g, ln1_b, w_router, b_router, w_gate_up, b_gate_up, w_down, b_down,
           ln2_g, ln2_b):
    B, S, D = x.shape
    depth = w_ada.shape[0]
    alpha = (2.0 * depth) ** 0.25
    for l in range(depth):
        mod = _ada(c, w_ada[l], b_ada[l]).reshape(B, N_MOD, D)
        x = _layer(x, mod, w_in[l], kv_norm_g[l], w_uk[l], w_uv[l], w_pool_group[l], pool_scale[l],
                   w_branch_attn[l], w_branch_pool[l], w_out[l], ln1_g[l], ln1_b[l], w_router[l],
                   b_router[l], w_gate_up[l], b_gate_up[l], w_down[l], b_down[l], ln2_g[l], ln2_b[l],
                   alpha=alpha)
    return x
```

```python
import functools

import jax
import jax.numpy as jnp
from jax import lax
from jax.experimental import pallas as pl
from jax.experimental.pallas import tpu as pltpu

F32 = jnp.float32
BF16 = jnp.bfloat16
I32 = jnp.int32
HIGHEST = lax.Precision.HIGHEST

LN_EPS = 1e-5
TOPK_MAX = 256
TOP_K_EXPERTS = 4
POOL_WINDOWS = (2, 4, 8, 16)
SWIGLU_LIMIT = 7.0
SWIGLU_ALPHA = 1.702
N_MOD = 6
INT_MIN = -(2 ** 31)
NEG_BIG = -1e30
LANES = 128
SUBLANES = 8
SORT_GROUP = 16
TIE_CHUNK = 256
POOL_HALO = 16
ONES_ROWS = 16
LOG2E = 1.4426950408889634
ATTN_KEY_CHUNK = 256
ATTN_HEAD_UNROLL = 4


def _store_row_slabs(ref, x):
    n, D = x.shape
    sub = D // LANES
    for s in range(sub):
        ref[pl.ds(s, n, stride=sub), :] = x[:, s * LANES:(s + 1) * LANES]


def _load_row_slabs(ref, n, sub):
    return jnp.concatenate([ref[pl.ds(s, n, stride=sub), :] for s in range(sub)], axis=1)


def _sigmoid(x):
    return 0.5 * jnp.tanh(0.5 * x) + 0.5


def _layer_norm(z, g, b):
    mu = jnp.mean(z, axis=-1, keepdims=True)
    zc = z - mu
    var = jnp.mean(zc * zc, axis=-1, keepdims=True)
    return zc * lax.rsqrt(var + LN_EPS) * g + b


def _ada_kernel(c_ref, w_ref, b_ref, o_ref):
    c = c_ref[...]
    cond = c * jax.nn.sigmoid(c)
    o_ref[...] = jnp.dot(cond, w_ref[...], precision=HIGHEST, preferred_element_type=F32) + b_ref[...]


def _ada(c, w_ada, b_ada):
    B, D = c.shape
    N = w_ada.shape[1]
    tn = D
    return pl.pallas_call(
        _ada_kernel,
        grid=(N // tn,),
        in_specs=[pl.BlockSpec((B, D), lambda j: (0, 0)),
                  pl.BlockSpec((D, tn), lambda j: (0, j)),
                  pl.BlockSpec((1, tn), lambda j: (0, j))],
        out_specs=pl.BlockSpec((B, tn), lambda j: (0, j)),
        out_shape=jax.ShapeDtypeStruct((B, N), F32),
    )(c, w_ada, b_ada.reshape(1, N))


def _inproj_kernel(x_ref, mod_ref, wq, wckv, wqi, wki, wwi, wu, wg, kvg_ref, wukt_ref,
                   qlt_o, ckva_o, ckvt_o, qit_o, ki_o, wit_o, u_o, g_o,
                   *, tm, H, dh, Hi, q_scale, qi_scale, wi_scale):
    i = pl.program_id(1)
    x = x_ref[0]
    mod = mod_ref[0]
    h = (x * (1.0 + mod[1:2]) + mod[0:1]).astype(BF16)

    def proj(w):
        return jnp.dot(h, w[...], preferred_element_type=F32)

    qt = proj(wq).T.astype(BF16)
    for hh in range(H):
        qlt = jnp.dot(wukt_ref[hh], qt[hh * dh:(hh + 1) * dh, :], preferred_element_type=F32) * q_scale
        qlt_o[0, hh] = qlt.astype(BF16)

    ckv = proj(wckv)
    ckv = ckv * lax.rsqrt(jnp.mean(ckv * ckv, axis=-1, keepdims=True) + LN_EPS) * kvg_ref[...]
    ckvt_o[0, 0:ckv.shape[1], :] = ckv.T.astype(BF16)
    ckvt_o[0, ckv.shape[1]:, :] = jnp.ones((ONES_ROWS, tm), BF16)
    R = ckv.shape[1]
    pos = i * tm + lax.broadcasted_iota(I32, (tm, R), 0)
    lane = lax.broadcasted_iota(I32, (tm, R), 1)
    pos_hi = ((pos >> 8) << 8).astype(F32)
    pos_lo = (pos & 255).astype(F32)
    extra = jnp.where((lane == 0) | (lane == 2), pos_hi, jnp.where((lane == 1) | (lane == 3), pos_lo, 0.0))
    ckva_o[0, :, 0:R] = ckv.astype(BF16)
    ckva_o[0, :, R:2 * R] = extra.astype(BF16)

    qit_o[0] = (proj(wqi) * qi_scale).T.astype(BF16)
    ki_o[0] = proj(wki).astype(BF16)
    wit_o[0] = (proj(wwi) * wi_scale).T[0:Hi, :]
    u_o[0] = proj(wu)
    g_o[0] = proj(wg).astype(BF16)


def _inproj(x, mod, ws, kv_norm_g, wukt, *, q_scale, qi_scale, wi_scale, Hi, tm):
    B, S, D = x.shape
    wq, wckv, wqi, wki, wwi, wu, wg = ws
    H, R, dh = wukt.shape
    row = lambda n: pl.BlockSpec((1, tm, n), lambda b, i: (b, i, 0))
    col = lambda n: pl.BlockSpec((1, n, tm), lambda b, i: (b, 0, i))
    full = lambda a: pl.BlockSpec(a.shape, lambda b, i: (0,) * a.ndim)
    out_specs = [pl.BlockSpec((1, H, R, tm), lambda b, i: (b, 0, 0, i)), row(2 * R), col(R + ONES_ROWS),
                 col(wqi.shape[1]),
                 row(wki.shape[1]), col(Hi), row(wu.shape[1]), row(wg.shape[1])]
    out_shape = [jax.ShapeDtypeStruct((B, H, R, S), BF16), jax.ShapeDtypeStruct((B, S, 2 * R), BF16),
                 jax.ShapeDtypeStruct((B, R + ONES_ROWS, S), BF16),
                 jax.ShapeDtypeStruct((B, wqi.shape[1], S), BF16),
                 jax.ShapeDtypeStruct((B, S, wki.shape[1]), BF16), jax.ShapeDtypeStruct((B, Hi, S), F32),
                 jax.ShapeDtypeStruct((B, S, wu.shape[1]), F32), jax.ShapeDtypeStruct((B, S, wg.shape[1]), BF16)]
    return pl.pallas_call(
        functools.partial(_inproj_kernel, tm=tm, H=H, dh=dh, Hi=Hi, q_scale=q_scale, qi_scale=qi_scale,
                          wi_scale=wi_scale),
        grid=(B, S // tm),
        in_specs=[row(D), pl.BlockSpec((1, N_MOD, D), lambda b, i: (b, 0, 0))]
                 + [full(w) for w in ws] + [full(kv_norm_g), full(wukt)],
        out_specs=out_specs,
        out_shape=out_shape,
        compiler_params=pltpu.CompilerParams(
            dimension_semantics=("parallel", "parallel"), vmem_limit_bytes=48 * 2 ** 20),
    )(x, mod, *ws, kv_norm_g, wukt)


def _key_to_f32(t):
    return pltpu.bitcast(jnp.where(t >= 0, t, t ^ 0x7FFFFFFF), F32)


def _sort_network(n):
    pairs, p = [], 1
    while p < n:
        k = p
        while k >= 1:
            for j in range(k % p, n - k, 2 * k):
                for i in range(min(k, n - j - k)):
                    if (i + j) // (2 * p) == (i + j + k) // (2 * p):
                        pairs.append((i + j, i + j + k))
            k //= 2
        p *= 2
    return pairs


def _count_ge_sorted(blocks, cand):
    n = len(blocks)

    def pick(level, bits):
        if len(level) == 1:
            return level[0]
        half = len(level) // 2
        return jnp.where(bits[0], pick(level[half:], bits[1:]), pick(level[:half], bits[1:]))

    total = jnp.where(blocks[n - 1] >= cand, 1.0, 0.0)
    bits = []
    step = n // 2
    while step >= 1:
        level = [blocks[c + step - 1] for c in range(0, n, 2 * step)]
        b = pick(level, bits) >= cand
        bits.append(b)
        total = total + jnp.where(b, float(step), 0.0)
        step //= 2
    return total


def _attn_body(qit_ref, wit_ref, ki_ref, qlt_ref, aug_ref, ckva_ref, ckvt_ref, wuvt_ref, yt_ref,
               score_scr, bias_scr, lg_scr, o_scr, sort_scr, *, W, tq, S, H, dh, Hi, di, topk, cw):
    j = pl.program_id(1)
    qpos = j * tq + lax.broadcasted_iota(I32, (1, tq), 1)
    score = score_scr.at[0:W, :]

    wit = wit_ref[0]
    for c in range(W // cw):
        kic = ki_ref[0, c * cw:(c + 1) * cw, :]
        acc = jnp.zeros((cw, tq), F32)
        for h in range(Hi):
            d = jnp.dot(kic, qit_ref[0, h * di:(h + 1) * di, :], preferred_element_type=F32)
            acc = acc + wit[h:h + 1, :] * jnp.maximum(d, 0.0)
        kpos_c = c * cw + lax.broadcasted_iota(I32, (cw, tq), 0)
        score_scr[c * cw:(c + 1) * cw, :] = jnp.where(kpos_c <= qpos, acc, -jnp.inf)

    def count(hit):
        return jnp.sum(jnp.where(hit, 1.0, 0.0), axis=0, keepdims=True)

    Wb = W // SORT_GROUP
    net = _sort_network(SORT_GROUP)

    def column(r, lg):
        return [(slice(g * Wb + r * SUBLANES, g * Wb + (r + 1) * SUBLANES), slice(lg * LANES, (lg + 1) * LANES))
                for g in range(SORT_GROUP)]

    for r in range(Wb // SUBLANES):
        for lg in range(tq // LANES):
            v = [score_scr[idx] for idx in column(r, lg)]
            for a, b in net:
                v[a], v[b] = jnp.maximum(v[a], v[b]), jnp.minimum(v[a], v[b])
            for idx, val in zip(column(r, lg), v):
                sort_scr[idx] = val

    def count_ge(cand_f):
        parts = []
        for lg in range(tq // LANES):
            c = cand_f[:, lg * LANES:(lg + 1) * LANES]
            acc = None
            for r in range(Wb // SUBLANES):
                n = _count_ge_sorted([sort_scr[idx] for idx in column(r, lg)], c)
                acc = n if acc is None else acc + n
            parts.append(jnp.sum(acc, axis=0, keepdims=True))
        return jnp.concatenate(parts, axis=1)

    def thr_step(i, t):
        cand = t + lax.shift_left(jnp.int32(1), 31 - i)
        cnt = count_ge(_key_to_f32(cand))
        return jnp.where(cnt >= float(topk), cand, t)

    t = lax.fori_loop(0, 32, thr_step, jnp.full((1, tq), INT_MIN, I32))
    few = t == INT_MIN
    thr_f = _key_to_f32(t)
    up_f = _key_to_f32(t + 1)
    need = float(topk) - count(score[...] >= up_f)

    tc = min(TIE_CHUNK, W)
    earlier = (lax.broadcasted_iota(I32, (tc, tc), 1) < lax.broadcasted_iota(I32, (tc, tc), 0)).astype(BF16)
    tied_before = jnp.zeros((1, tq), F32)
    for c in range(W // tc):
        rows = slice(c * tc, (c + 1) * tc)
        s = score_scr[rows, :]
        gt = s >= up_f
        eq = (s >= thr_f) & jnp.logical_not(gt)
        eq_f = jnp.where(eq, 1.0, 0.0)
        rank = jnp.dot(earlier, eq_f.astype(BF16), preferred_element_type=F32) + tied_before
        tied_before = tied_before + jnp.sum(eq_f, axis=0, keepdims=True)
        kpos = c * tc + lax.broadcasted_iota(I32, (tc, tq), 0)
        sel = (few | gt | (eq & (rank < need))) & (kpos <= qpos)
        bias_scr[rows, :] = jnp.where(sel, 0.0, NEG_BIG)

    R = qlt_ref.shape[2]
    ck = min(ATTN_KEY_CHUNK, W)

    def head(h, carry):
        qa = jnp.concatenate([qlt_ref[0, h], aug_ref[h]], axis=0)
        m = None
        for c in range(W // ck):
            keys = slice(c * ck, (c + 1) * ck)
            lg = jnp.dot(ckva_ref[0, keys, :], qa, preferred_element_type=F32) + bias_scr[keys, :]
            lg_scr[keys, :] = lg
            mc = jnp.max(lg, axis=0, keepdims=True)
            m = mc if m is None else jnp.maximum(m, mc)
        acc = jnp.zeros((R + ONES_ROWS, tq), F32)
        for c in range(W // ck):
            keys = slice(c * ck, (c + 1) * ck)
            p = jnp.exp2(lg_scr[keys, :] - m)
            acc = acc + jnp.dot(ckvt_ref[0, :, keys], p.astype(BF16), preferred_element_type=F32)
        o_scr[h] = (acc[0:R] / acc[R:R + 1]).astype(BF16)
        return carry

    lax.fori_loop(0, H, head, 0, unroll=ATTN_HEAD_UNROLL)
    for h in range(H):
        yt = jnp.dot(wuvt_ref[h], o_scr[h], preferred_element_type=F32)
        yt_ref[0, h * dh:(h + 1) * dh, :] = yt.astype(BF16)


def _attn_kernel(*refs, tq, cw, S, **kw):
    j = pl.program_id(1)
    nc = ((j + 1) * tq + cw - 1) // cw
    for w in range(1, S // cw + 1):
        @pl.when(nc == w)
        def _():
            _attn_body(*refs, W=w * cw, tq=tq, S=S, cw=cw, **kw)


def _attention(qit, wit, ki, qlt, ckva, ckvt, wuvt, *, tq, topk):
    B, H, R, S = qlt.shape
    dh = wuvt.shape[1]
    di = ki.shape[2]
    Hi = wit.shape[1]
    slopes = jnp.asarray([LOG2E * 2.0 ** (-8.0 * (h + 1) / H) for h in range(H)], F32)
    s_hi = slopes.astype(BF16)
    s_lo = (slopes - s_hi.astype(F32)).astype(BF16)
    aug = jnp.zeros((H, R, tq), BF16)
    aug = aug.at[:, 0:2, :].set(s_hi[:, None, None]).at[:, 2:4, :].set(s_lo[:, None, None])
    kern = functools.partial(_attn_kernel, tq=tq, S=S, H=H, dh=dh, Hi=Hi, di=di, topk=topk, cw=min(S, 512))
    col = lambda n: pl.BlockSpec((1, n, tq), lambda b, j: (b, 0, j))
    full = lambda a: pl.BlockSpec(a.shape, lambda b, j: (0,) * a.ndim)
    return pl.pallas_call(
        kern,
        grid=(B, S // tq),
        in_specs=[col(qit.shape[1]), col(Hi), pl.BlockSpec((1, S, di), lambda b, j: (b, 0, 0)),
                  pl.BlockSpec((1, H, R, tq), lambda b, j: (b, 0, 0, j)), full(aug),
                  pl.BlockSpec((1, S, 2 * R), lambda b, j: (b, 0, 0)),
                  pl.BlockSpec((1, R + ONES_ROWS, S), lambda b, j: (b, 0, 0)), full(wuvt)],
        out_specs=col(H * dh),
        out_shape=jax.ShapeDtypeStruct((B, H * dh, S), BF16),
        scratch_shapes=[pltpu.VMEM((S, tq), F32), pltpu.VMEM((S, tq), F32), pltpu.VMEM((S, tq), F32),
                        pltpu.VMEM((H, R, tq), BF16), pltpu.VMEM((S, tq), F32)],
        compiler_params=pltpu.CompilerParams(
            dimension_semantics=("parallel", "parallel"), vmem_limit_bytes=48 * 2 ** 20),
    )(qit, wit, ki, qlt, aug, ckva, ckvt, wuvt)


def _merge_kernel(x_ref, yat_ref, u_ref, halo_ref, g_ref, mod_ref, wpg_ref, psc_ref, wba_ref, wbp_ref,
                  wout_ref, ln1g_ref, ln1b_ref, wr_ref, br_ref,
                  x1_o, h2_o, te_o, tg_o, ext_scr, *, tm, D, G, alpha, E, K):
    i = pl.program_id(1)

    @pl.when(i == 0)
    def _():
        ext_scr[0:POOL_HALO, :] = jnp.zeros((POOL_HALO, ext_scr.shape[1]), F32)

    @pl.when(i > 0)
    def _():
        ext_scr[0:POOL_HALO, :] = halo_ref[0]

    ext_scr[POOL_HALO:POOL_HALO + tm, :] = u_ref[0]

    pos = i * tm + lax.broadcasted_iota(I32, (tm, 1), 0)
    ys = []
    for g, w in enumerate(POOL_WINDOWS):
        cols = slice(g * G, (g + 1) * G)
        cur = ext_scr[POOL_HALO:POOL_HALO + tm, cols]
        acc = cur
        for back in range(1, w):
            acc = acc + ext_scr[POOL_HALO - back:POOL_HALO - back + tm, cols]
        cnt = jnp.minimum(pos + 1, w).astype(F32)
        pooled = acc / cnt - cur
        ys.append(jnp.dot(pooled.astype(BF16), wpg_ref[g], preferred_element_type=F32))
    y_pool = jnp.concatenate(ys, axis=1) * psc_ref[...]

    gates = g_ref[0]
    ya = yat_ref[0].astype(F32).T.astype(BF16)
    a = jnp.dot(ya, wba_ref[...], preferred_element_type=F32)
    p = jnp.dot(y_pool.astype(BF16), wbp_ref[...], preferred_element_type=F32)
    mix = _sigmoid(gates[:, :D].astype(F32)) * a + _sigmoid(gates[:, D:].astype(F32)) * p
    o = jnp.dot(mix.astype(BF16), wout_ref[...], preferred_element_type=F32)
    mod = mod_ref[0]
    x1 = _layer_norm(alpha * x_ref[0] + mod[2:3] * o, ln1g_ref[...], ln1b_ref[...])
    x1_o[0] = x1
    h2 = x1 * (1.0 + mod[4:5]) + mod[3:4]
    _store_row_slabs(h2_o.at[0], h2)

    h2_hi = h2.astype(BF16)
    h2_lo = (h2 - h2_hi.astype(F32)).astype(BF16)
    logits = (jnp.dot(h2_hi, wr_ref[0], preferred_element_type=F32)
              + jnp.dot(h2_hi, wr_ref[1], preferred_element_type=F32)
              + jnp.dot(h2_lo, wr_ref[0], preferred_element_type=F32)) + br_ref[...]
    lane = lax.broadcasted_iota(I32, logits.shape, 1)
    logits = jnp.where(lane < E, logits, -jnp.inf)
    lane_f = lane.astype(F32)
    te_f = jnp.zeros(logits.shape, F32)
    vals = []
    for k in range(K):
        m = jnp.max(logits, axis=1, keepdims=True)
        idx = jnp.min(jnp.where(logits == m, lane_f, float(LANES)), axis=1, keepdims=True)
        te_f = jnp.where(lane == k, idx, te_f)
        vals.append(m)
        logits = jnp.where(lane_f == idx, -jnp.inf, logits)
    te = te_f.astype(I32)
    ex = [jnp.exp(v - vals[0]) for v in vals]
    den = ex[0]
    for e in ex[1:]:
        den = den + e
    tg = jnp.zeros(logits.shape, F32)
    for k in range(K):
        tg = jnp.where(lane == k, ex[k] / den, tg)
    te_o[0] = te
    tg_o[0] = tg


def _merge(x, yat, u, g, mod, wpg, psc, wba, wbp, wout, ln1g, ln1b, wr, br, *, tm, alpha, E, K):
    B, S, D = x.shape
    PW = u.shape[2]
    G = wpg.shape[1]
    kern = functools.partial(_merge_kernel, tm=tm, D=D, G=G, alpha=alpha, E=E, K=K)
    row = lambda n: pl.BlockSpec((1, tm, n), lambda b, i: (b, i, 0))
    full = lambda a: pl.BlockSpec(a.shape, lambda b, i: (0,) * a.ndim)
    hb = tm // POOL_HALO
    halo = pl.BlockSpec((1, POOL_HALO, PW), lambda b, i: (b, jnp.maximum(i * hb - 1, 0), 0))
    return pl.pallas_call(
        kern,
        grid=(B, S // tm),
        in_specs=[row(D), pl.BlockSpec((1, yat.shape[1], tm), lambda b, i: (b, 0, i)), row(PW), halo,
                  row(g.shape[2]), pl.BlockSpec((1, N_MOD, D), lambda b, i: (b, 0, 0)),
                  full(wpg), full(psc), full(wba), full(wbp), full(wout), full(ln1g), full(ln1b),
                  full(wr), full(br)],
        out_specs=[row(D), pl.BlockSpec((1, tm * (D // LANES), LANES), lambda b, i: (b, i, 0)),
                   row(LANES), row(LANES)],
        out_shape=[jax.ShapeDtypeStruct((B, S, D), F32),
                   jax.ShapeDtypeStruct((B, S * (D // LANES), LANES), F32),
                   jax.ShapeDtypeStruct((B, S, LANES), I32), jax.ShapeDtypeStruct((B, S, LANES), F32)],
        scratch_shapes=[pltpu.VMEM((POOL_HALO + tm, PW), F32)],
        compiler_params=pltpu.CompilerParams(
            dimension_semantics=("parallel", "arbitrary"), vmem_limit_bytes=48 * 2 ** 20),
    )(x, yat, u, u, g, mod, wpg, psc, wba, wbp, wout, ln1g, ln1b, wr, br)


def _route_kernel(te_ref, rank_o, cnt_o, base_scr, *, tR, K):
    i = pl.program_id(0)

    @pl.when(i == 0)
    def _():
        base_scr[...] = jnp.zeros(base_scr.shape, F32)

    te = te_ref[...]
    lane = lax.broadcasted_iota(I32, te.shape, 1)
    hits = [lane == te[:, k:k + 1] for k in range(K)]
    onehot = hits[0].astype(F32)
    for k in range(1, K):
        onehot = onehot + hits[k].astype(F32)
    earlier = (lax.broadcasted_iota(I32, (tR, tR), 0) > lax.broadcasted_iota(I32, (tR, tR), 1)).astype(BF16)
    before = jnp.dot(earlier, onehot.astype(BF16), preferred_element_type=F32) + base_scr[...]
    rank = jnp.zeros(te.shape, I32)
    for k in range(K):
        rk = jnp.sum(jnp.where(hits[k], before, 0.0), axis=1, keepdims=True)
        rank = jnp.where(lane == k, rk.astype(I32), rank)
    rank_o[...] = rank
    base_scr[...] = base_scr[...] + jnp.sum(onehot, axis=0, keepdims=True)
    cnt_o[...] = base_scr[...].astype(I32)


def _route(te, *, tR, K):
    T = te.shape[0]
    return pl.pallas_call(
        functools.partial(_route_kernel, tR=tR, K=K),
        grid=(T // tR,),
        in_specs=[pl.BlockSpec((tR, LANES), lambda i: (i, 0))],
        out_specs=[pl.BlockSpec((tR, LANES), lambda i: (i, 0)), pl.BlockSpec((1, LANES), lambda i: (0, 0))],
        out_shape=[jax.ShapeDtypeStruct((T, LANES), I32), jax.ShapeDtypeStruct((1, LANES), I32)],
        scratch_shapes=[pltpu.VMEM((1, LANES), F32)],
        compiler_params=pltpu.CompilerParams(dimension_semantics=("arbitrary",)),
    )(te)


def _dispatch_kernel(pe_ref, nu_ref, pos_ref, h2_ref, xs_out, zero_scr, sem, zsem,
                     *, tD, K, sub, E, tmE, n_blocks):
    blk = tmE * sub

    @pl.when(pl.program_id(0) == 0)
    def _():
        zero_scr[...] = jnp.zeros(zero_scr.shape, zero_scr.dtype)

        def zero_block(b):
            return pltpu.make_async_copy(zero_scr, xs_out.at[pl.ds(pl.multiple_of(b * blk, blk), blk), :], zsem)

        def last_block(e):
            return jnp.maximum(pe_ref[e] // tmE - 1, 0)

        def start_last(e, carry):
            zero_block(last_block(e)).start()
            return carry

        def start_tail(b, carry):
            zero_block(b).start()
            return carry

        def wait_last(e, carry):
            zero_block(last_block(e)).wait()
            return carry

        def wait_tail(b, carry):
            zero_block(b).wait()
            return carry

        lax.fori_loop(0, E, start_last, 0)
        lax.fori_loop(nu_ref[0], n_blocks, start_tail, 0)
        lax.fori_loop(0, E, wait_last, 0)
        lax.fori_loop(nu_ref[0], n_blocks, wait_tail, 0)

    def issue(r, carry):
        src = h2_ref.at[pl.ds(pl.multiple_of(r * sub, sub), sub), :]
        for k in range(K):
            p = pl.multiple_of(pos_ref[0, 0, k * tD + r], sub)
            pltpu.async_copy(src, xs_out.at[pl.ds(p, sub), :], sem, priority=k % 2)
        return carry

    lax.fori_loop(0, tD, issue, 0, unroll=8)
    for k in range(K):
        pltpu.make_async_copy(h2_ref, xs_out.at[pl.ds(0, tD * sub), :], sem).wait()


def _dispatch(pad_end, n_used, pos, h2, n_blocks, *, tD, tmE, K, sub):
    T = h2.shape[0] // sub
    E = pad_end.shape[0]
    grid_spec = pltpu.PrefetchScalarGridSpec(
        num_scalar_prefetch=2,
        grid=(T // tD,),
        in_specs=[pl.BlockSpec((1, 1, K * tD), lambda i, pe, nu: (i, 0, 0), memory_space=pltpu.SMEM),
                  pl.BlockSpec((tD * sub, LANES), lambda i, pe, nu: (i, 0))],
        out_specs=pl.BlockSpec(memory_space=pl.ANY),
        scratch_shapes=[pltpu.VMEM((tmE * sub, LANES), h2.dtype), pltpu.SemaphoreType.DMA(()),
                        pltpu.SemaphoreType.DMA(())],
    )
    return pl.pallas_call(
        functools.partial(_dispatch_kernel, tD=tD, K=K, sub=sub, E=E, tmE=tmE, n_blocks=n_blocks),
        grid_spec=grid_spec,
        out_shape=jax.ShapeDtypeStruct((n_blocks * tmE * sub, LANES), h2.dtype),
        compiler_params=pltpu.CompilerParams(
            dimension_semantics=("arbitrary",), has_side_effects=True),
    )(pad_end, n_used, pos, h2)


def _expert_kernel(be_ref, nu_ref, nx_ref, sl_ref, x_ref, wgu_hbm, bgu_ref, wd_hbm, bd_ref, y_ref,
                   wgu_f32, wd_f32, wgu_bf, wd_bf, sem, *, F, tmE, sub):
    i = pl.program_id(0)
    used = i < nu_ref[0]
    e = be_ref[i]
    slot = sl_ref[i]
    new_expert = used & ((i == 0) | (e != be_ref[jnp.maximum(i - 1, 0)]))

    def fetch(expert, s):
        return (pltpu.make_async_copy(wgu_hbm.at[expert], wgu_f32.at[s], sem.at[s, 0]),
                pltpu.make_async_copy(wd_hbm.at[expert], wd_f32.at[s], sem.at[s, 1]))

    @pl.when(used & (i == 0))
    def _():
        for cp in fetch(e, slot):
            cp.start()

    @pl.when(new_expert)
    def _():
        for cp in fetch(e, slot):
            cp.wait()

        @pl.when(nx_ref[i] >= 0)
        def _():
            for cp in fetch(nx_ref[i], 1 - slot):
                cp.start()

        wgu_bf[...] = wgu_f32[slot].astype(BF16)
        wd_bf[...] = wd_f32[slot].astype(BF16)

    @pl.when(used)
    def _():
        x = _load_row_slabs(x_ref, tmE, sub).astype(BF16)
        gu = jnp.dot(x, wgu_bf[...], preferred_element_type=F32) + bgu_ref[0]
        glu = jnp.minimum(gu[:, :F], SWIGLU_LIMIT)
        lin = jnp.clip(gu[:, F:], -SWIGLU_LIMIT, SWIGLU_LIMIT)
        act = glu * jax.nn.sigmoid(SWIGLU_ALPHA * glu) * (lin + 1.0)
        y = jnp.dot(act.astype(BF16), wd_bf[...], preferred_element_type=F32) + bd_ref[0]
        _store_row_slabs(y_ref, y)

    @pl.when(jnp.logical_not(used))
    def _():
        y_ref[...] = jnp.zeros(y_ref.shape, F32)


def _experts(block_e, n_used, next_e, slot, xs, wgu, bgu, wd, bd, *, tmE):
    E, D, F2 = wgu.shape
    F = F2 // 2
    sub = D // LANES
    n_blocks = xs.shape[0] // (tmE * sub)
    slab_block = pl.BlockSpec((tmE * sub, LANES), lambda i, be, nu, nx, sl: (i, 0))
    grid_spec = pltpu.PrefetchScalarGridSpec(
        num_scalar_prefetch=4,
        grid=(n_blocks,),
        in_specs=[slab_block,
                  pl.BlockSpec(memory_space=pl.ANY),
                  pl.BlockSpec((1, 1, F2), lambda i, be, nu, nx, sl: (be[i], 0, 0)),
                  pl.BlockSpec(memory_space=pl.ANY),
                  pl.BlockSpec((1, 1, D), lambda i, be, nu, nx, sl: (be[i], 0, 0))],
        out_specs=slab_block,
        scratch_shapes=[pltpu.VMEM((2, D, F2), F32), pltpu.VMEM((2, F, D), F32),
                        pltpu.VMEM((D, F2), BF16), pltpu.VMEM((F, D), BF16),
                        pltpu.SemaphoreType.DMA((2, 2))],
    )
    return pl.pallas_call(
        functools.partial(_expert_kernel, F=F, tmE=tmE, sub=sub),
        grid_spec=grid_spec,
        out_shape=jax.ShapeDtypeStruct(xs.shape, F32),
        compiler_params=pltpu.CompilerParams(
            dimension_semantics=("arbitrary",), vmem_limit_bytes=56 * 2 ** 20),
    )(block_e, n_used, next_e, slot, xs, wgu, bgu, wd, bd)


def _combine_kernel(pos_ref, nxt_ref, y_hbm, x1_ref, tg_ref, mod_ref, g_ref, b_ref, o_ref, buf, sem,
                    *, tmC, K, alpha, nt, sub):
    i = pl.program_id(0)
    slot = lax.rem(i, 2)

    def fetch(p_ref, s):
        def issue(r, carry):
            dst_rows = pl.ds(pl.multiple_of(r * sub, sub), sub)
            for k in range(K):
                p = pl.multiple_of(p_ref[0, 0, k * tmC + r], sub)
                pltpu.async_copy(y_hbm.at[pl.ds(p, sub), :], buf.at[s, k, dst_rows, :], sem.at[s],
                                 priority=k % 2)
            return carry

        lax.fori_loop(0, tmC, issue, 0, unroll=8)

    @pl.when(i == 0)
    def _():
        fetch(pos_ref, slot)

    @pl.when(i + 1 < nt)
    def _():
        fetch(nxt_ref, 1 - slot)

    for k in range(K):
        pltpu.make_async_copy(y_hbm.at[pl.ds(0, tmC * sub), :], buf.at[slot, k], sem.at[slot]).wait()

    tg = tg_ref[...]
    y = tg[:, 0:1] * _load_row_slabs(buf.at[slot, 0], tmC, sub)
    for k in range(1, K):
        y = y + tg[:, k:k + 1] * _load_row_slabs(buf.at[slot, k], tmC, sub)
    mod = mod_ref[0]
    o_ref[...] = _layer_norm(alpha * x1_ref[...] + mod[5:6] * y, g_ref[...], b_ref[...])


def _combine(pos, y_sorted, x1, tg, mod, ln2g, ln2b, *, tmC, K, alpha, S):
    T, D = x1.shape
    nt = T // tmC
    per_seq = S // tmC
    sub = D // LANES
    return pl.pallas_call(
        functools.partial(_combine_kernel, tmC=tmC, K=K, alpha=alpha, nt=nt, sub=sub),
        grid=(nt,),
        in_specs=[pl.BlockSpec((1, 1, K * tmC), lambda i: (i, 0, 0), memory_space=pltpu.SMEM),
                  pl.BlockSpec((1, 1, K * tmC), lambda i: (jnp.minimum(i + 1, nt - 1), 0, 0),
                               memory_space=pltpu.SMEM),
                  pl.BlockSpec(memory_space=pl.ANY),
                  pl.BlockSpec((tmC, D), lambda i: (i, 0)),
                  pl.BlockSpec((tmC, LANES), lambda i: (i, 0)),
                  pl.BlockSpec((1, N_MOD, D), lambda i: (i // per_seq, 0, 0)),
                  pl.BlockSpec((1, D), lambda i: (0, 0)),
                  pl.BlockSpec((1, D), lambda i: (0, 0))],
        out_specs=pl.BlockSpec((tmC, D), lambda i: (i, 0)),
        out_shape=jax.ShapeDtypeStruct((T, D), F32),
        scratch_shapes=[pltpu.VMEM((2, K, tmC * sub, LANES), F32), pltpu.SemaphoreType.DMA((2,))],
        compiler_params=pltpu.CompilerParams(
            dimension_semantics=("arbitrary",), vmem_limit_bytes=48 * 2 ** 20),
    )(pos, pos, y_sorted, x1, tg, mod, ln2g, ln2b)


def _routing_tables(top_e, rank, counts, E, tmE, sub):
    T, K = top_e.shape
    padded = (counts + tmE - 1) // tmE * tmE
    pad_end = jnp.cumsum(padded)
    pad_start = pad_end - padded
    n_blocks = -(-(T * K) // tmE) + E
    block_start = jnp.arange(n_blocks, dtype=I32) * tmE
    block_e = jnp.minimum(jnp.sum((pad_end[None, :] <= block_start[:, None]).astype(I32), axis=1), E - 1)
    n_used = (pad_end[-1] // tmE).astype(I32).reshape(1)
    ids = jnp.arange(E, dtype=I32)

    def lookup(table, idx):
        return jnp.sum(jnp.where(idx[..., None] == ids, table.astype(I32), 0), axis=-1)

    pos = (lookup(pad_start, top_e) + rank).astype(I32) * sub
    nonempty = counts > 0
    at_or_after = jnp.flip(lax.cummin(jnp.flip(jnp.where(nonempty, ids, E))))
    after = jnp.concatenate([at_or_after[1:], jnp.full((1,), E, I32)])
    next_e = lookup(jnp.where(after < E, after, -1), block_e)
    slot = lookup((jnp.cumsum(nonempty.astype(I32)) - 1) % 2, block_e)
    return block_e.astype(I32), n_used, next_e, slot, pad_end.astype(I32), pos, n_blocks


def _tile_major(pos, tile):
    T, K = pos.shape
    return pos.reshape(T // tile, tile, K).transpose(0, 2, 1).reshape(T // tile, 1, K * tile)


def _layer(x, mod, w_in, kv_norm_g, w_uk, w_uv, w_pool_group, pool_scale, w_branch_attn, w_branch_pool,
           w_out, ln1_g, ln1_b, w_router, b_router, w_gate_up, b_gate_up, w_down, b_down, ln2_g, ln2_b,
           *, alpha):
    B, S, D = x.shape
    R, H, dh = w_uk.shape
    AW = H * dh
    PW = pool_scale.shape[0]
    E = w_router.shape[1]
    K = TOP_K_EXPERTS
    n_idx = w_in.shape[1] - (AW + R + PW + 2 * D)
    di = dh
    Hi = (n_idx - di) // (di + 1)
    sizes = (AW, R, Hi * di, di, Hi, PW, 2 * D)
    offs = [0]
    for s in sizes:
        offs.append(offs[-1] + s)
    ws = [w_in[:, offs[k]:offs[k + 1]].astype(BF16) for k in range(len(sizes))]
    ws[4] = jnp.pad(ws[4], ((0, 0), (0, LANES - Hi)))

    wukt = w_uk.transpose(1, 0, 2).astype(BF16)
    wuvt = w_uv.transpose(1, 2, 0).astype(BF16)
    qlt, ckva, ckvt, qit, ki, wit, u, g = _inproj(
        x, mod, ws, kv_norm_g.reshape(1, R), wukt, q_scale=dh ** -0.5 * LOG2E, qi_scale=di ** -0.5,
        wi_scale=Hi ** -0.5, Hi=Hi, tm=min(S, 512))

    topk = min(TOPK_MAX, S // 4)
    y_attn_t = _attention(qit, wit, ki, qlt, ckva, ckvt, wuvt, tq=min(S, 256), topk=topk)

    wr = jnp.pad(w_router, ((0, 0), (0, LANES - E)))
    wr_hi = wr.astype(BF16)
    wr = jnp.stack([wr_hi, (wr - wr_hi.astype(F32)).astype(BF16)])
    br = jnp.pad(b_router, (0, LANES - E)).reshape(1, LANES)
    x1, h2, te, tg = _merge(x, y_attn_t, u, g, mod, w_pool_group.astype(BF16), pool_scale.reshape(1, PW),
                            w_branch_attn.astype(BF16), w_branch_pool.astype(BF16), w_out.astype(BF16),
                            ln1_g.reshape(1, D), ln1_b.reshape(1, D), wr, br,
                            tm=min(S, 256), alpha=alpha, E=E, K=K)

    T = B * S
    tmE = 512
    tmC = min(S, 256)
    tD = min(T, 1024)
    sub = D // LANES
    te = te.reshape(T, LANES)
    rank, counts = _route(te, tR=min(T, 512), K=K)
    block_e, n_used, next_e, slot, pad_end, pos, n_blocks = _routing_tables(
        te[:, :K], rank[:, :K], counts[0, :E], E, tmE, sub)
    xs = _dispatch(pad_end, n_used, _tile_major(pos, tD), h2.reshape(T * sub, LANES), n_blocks,
                   tD=tD, tmE=tmE, K=K, sub=sub)
    y_sorted = _experts(block_e, n_used, next_e, slot, xs, w_gate_up, b_gate_up.reshape(E, 1, -1),
                        w_down, b_down.reshape(E, 1, D), tmE=tmE)
    out = _combine(_tile_major(pos, tmC), y_sorted, x1.reshape(T, D), tg.reshape(T, LANES), mod,
                   ln2_g.reshape(1, D), ln2_b.reshape(1, D), tmC=tmC, K=K, alpha=alpha, S=S)
    return out.reshape(B, S, D)


def kernel(x, c, w_ada, b_ada, w_in, kv_norm_g, w_uk, w_uv, w_pool_group, pool_scale, w_branch_attn,
           w_branch_pool, w_out, ln1_g, ln1_b, w_router, b_router, w_gate_up, b_gate_up, w_down, b_down,
           ln2_g, ln2_b):
    B, S, D = x.shape
    depth = w_ada.shape[0]
    alpha = (2.0 * depth) ** 0.25
    for l in range(depth):
        mod = _ada(c, w_ada[l], b_ada[l]).reshape(B, N_MOD, D)
        x = _layer(x, mod, w_in[l], kv_norm_g[l], w_uk[l], w_uv[l], w_pool_group[l], pool_scale[l],
                   w_branch_attn[l], w_branch_pool[l], w_out[l], ln1_g[l], ln1_b[l], w_router[l],
                   b_router[l], w_gate_up[l], b_gate_up[l], w_down[l], b_down[l], ln2_g[l], ln2_b[l],
                   alpha=alpha)
    return x
```

```python
import functools

import jax
import jax.numpy as jnp
from jax import lax
from jax.experimental import pallas as pl
from jax.experimental.pallas import tpu as pltpu

F32 = jnp.float32
BF16 = jnp.bfloat16
I32 = jnp.int32
HIGHEST = lax.Precision.HIGHEST

LN_EPS = 1e-5
TOPK_MAX = 256
TOP_K_EXPERTS = 4
POOL_WINDOWS = (2, 4, 8, 16)
SWIGLU_LIMIT = 7.0
SWIGLU_ALPHA = 1.702
N_MOD = 6
INT_MIN = -(2 ** 31)
NEG_BIG = -1e30
LANES = 128
SUBLANES = 8
SORT_GROUP = 16
TIE_CHUNK = 256
COMBINE_GROUPS = 8
POOL_HALO = 16
ONES_ROWS = 16
LOG2E = 1.4426950408889634
ATTN_KEY_CHUNK = 256
ATTN_HEAD_UNROLL = 4


def _store_row_slabs(ref, x):
    n, D = x.shape
    sub = D // LANES
    for s in range(sub):
        ref[pl.ds(s, n, stride=sub), :] = x[:, s * LANES:(s + 1) * LANES]


def _load_row_slabs(ref, n, sub):
    return jnp.concatenate([ref[pl.ds(s, n, stride=sub), :] for s in range(sub)], axis=1)


def _sigmoid(x):
    return 0.5 * jnp.tanh(0.5 * x) + 0.5


def _layer_norm(z, g, b):
    mu = jnp.mean(z, axis=-1, keepdims=True)
    zc = z - mu
    var = jnp.mean(zc * zc, axis=-1, keepdims=True)
    return zc * lax.rsqrt(var + LN_EPS) * g + b


def _ada_kernel(c_ref, w_ref, b_ref, o_ref):
    c = c_ref[...]
    cond = c * jax.nn.sigmoid(c)
    o_ref[...] = jnp.dot(cond, w_ref[...], precision=HIGHEST, preferred_element_type=F32) + b_ref[...]


def _ada(c, w_ada, b_ada):
    B, D = c.shape
    N = w_ada.shape[1]
    tn = D
    return pl.pallas_call(
        _ada_kernel,
        grid=(N // tn,),
        in_specs=[pl.BlockSpec((B, D), lambda j: (0, 0)),
                  pl.BlockSpec((D, tn), lambda j: (0, j)),
                  pl.BlockSpec((1, tn), lambda j: (0, j))],
        out_specs=pl.BlockSpec((B, tn), lambda j: (0, j)),
        out_shape=jax.ShapeDtypeStruct((B, N), F32),
    )(c, w_ada, b_ada.reshape(1, N))


def _inproj_kernel(x_ref, mod_ref, wq, wckv, wqi, wki, wwi, wu, wg, kvg_ref, wukt_ref,
                   qlt_o, ckva_o, ckvt_o, qit_o, ki_o, wit_o, u_o, g_o,
                   *, tm, H, dh, Hi, q_scale, qi_scale, wi_scale):
    i = pl.program_id(1)
    x = x_ref[0]
    mod = mod_ref[0]
    h = (x * (1.0 + mod[1:2]) + mod[0:1]).astype(BF16)

    def proj(w):
        return jnp.dot(h, w[...], preferred_element_type=F32)

    qt = proj(wq).T.astype(BF16)
    for hh in range(H):
        qlt = jnp.dot(wukt_ref[hh], qt[hh * dh:(hh + 1) * dh, :], preferred_element_type=F32) * q_scale
        qlt_o[0, hh] = qlt.astype(BF16)

    ckv = proj(wckv)
    ckv = ckv * lax.rsqrt(jnp.mean(ckv * ckv, axis=-1, keepdims=True) + LN_EPS) * kvg_ref[...]
    ckvt_o[0, 0:ckv.shape[1], :] = ckv.T.astype(BF16)
    ckvt_o[0, ckv.shape[1]:, :] = jnp.ones((ONES_ROWS, tm), BF16)
    R = ckv.shape[1]
    pos = i * tm + lax.broadcasted_iota(I32, (tm, R), 0)
    lane = lax.broadcasted_iota(I32, (tm, R), 1)
    pos_hi = ((pos >> 8) << 8).astype(F32)
    pos_lo = (pos & 255).astype(F32)
    extra = jnp.where((lane == 0) | (lane == 2), pos_hi, jnp.where((lane == 1) | (lane == 3), pos_lo, 0.0))
    ckva_o[0, :, 0:R] = ckv.astype(BF16)
    ckva_o[0, :, R:2 * R] = extra.astype(BF16)

    qit_o[0] = (proj(wqi) * qi_scale).T.astype(BF16)
    ki_o[0] = proj(wki).astype(BF16)
    wit_o[0] = (proj(wwi) * wi_scale).T[0:Hi, :]
    u_o[0] = proj(wu)
    g_o[0] = proj(wg).astype(BF16)


def _inproj(x, mod, ws, kv_norm_g, wukt, *, q_scale, qi_scale, wi_scale, Hi, tm):
    B, S, D = x.shape
    wq, wckv, wqi, wki, wwi, wu, wg = ws
    H, R, dh = wukt.shape
    row = lambda n: pl.BlockSpec((1, tm, n), lambda b, i: (b, i, 0))
    col = lambda n: pl.BlockSpec((1, n, tm), lambda b, i: (b, 0, i))
    full = lambda a: pl.BlockSpec(a.shape, lambda b, i: (0,) * a.ndim)
    out_specs = [pl.BlockSpec((1, H, R, tm), lambda b, i: (b, 0, 0, i)), row(2 * R), col(R + ONES_ROWS),
                 col(wqi.shape[1]),
                 row(wki.shape[1]), col(Hi), row(wu.shape[1]), row(wg.shape[1])]
    out_shape = [jax.ShapeDtypeStruct((B, H, R, S), BF16), jax.ShapeDtypeStruct((B, S, 2 * R), BF16),
                 jax.ShapeDtypeStruct((B, R + ONES_ROWS, S), BF16),
                 jax.ShapeDtypeStruct((B, wqi.shape[1], S), BF16),
                 jax.ShapeDtypeStruct((B, S, wki.shape[1]), BF16), jax.ShapeDtypeStruct((B, Hi, S), F32),
                 jax.ShapeDtypeStruct((B, S, wu.shape[1]), F32), jax.ShapeDtypeStruct((B, S, wg.shape[1]), BF16)]
    return pl.pallas_call(
        functools.partial(_inproj_kernel, tm=tm, H=H, dh=dh, Hi=Hi, q_scale=q_scale, qi_scale=qi_scale,
                          wi_scale=wi_scale),
        grid=(B, S // tm),
        in_specs=[row(D), pl.BlockSpec((1, N_MOD, D), lambda b, i: (b, 0, 0))]
                 + [full(w) for w in ws] + [full(kv_norm_g), full(wukt)],
        out_specs=out_specs,
        out_shape=out_shape,
        compiler_params=pltpu.CompilerParams(
            dimension_semantics=("parallel", "parallel"), vmem_limit_bytes=48 * 2 ** 20),
    )(x, mod, *ws, kv_norm_g, wukt)


def _key_to_f32(t):
    return pltpu.bitcast(jnp.where(t >= 0, t, t ^ 0x7FFFFFFF), F32)


def _sort_network(n):
    pairs, p = [], 1
    while p < n:
        k = p
        while k >= 1:
            for j in range(k % p, n - k, 2 * k):
                for i in range(min(k, n - j - k)):
                    if (i + j) // (2 * p) == (i + j + k) // (2 * p):
                        pairs.append((i + j, i + j + k))
            k //= 2
        p *= 2
    return pairs


def _count_ge_sorted(blocks, cand):
    n = len(blocks)

    def pick(level, bits):
        if len(level) == 1:
            return level[0]
        half = len(level) // 2
        return jnp.where(bits[0], pick(level[half:], bits[1:]), pick(level[:half], bits[1:]))

    total = jnp.where(blocks[n - 1] >= cand, 1.0, 0.0)
    bits = []
    step = n // 2
    while step >= 1:
        level = [blocks[c + step - 1] for c in range(0, n, 2 * step)]
        b = pick(level, bits) >= cand
        bits.append(b)
        total = total + jnp.where(b, float(step), 0.0)
        step //= 2
    return total


def _attn_body(qit_ref, wit_ref, ki_ref, qlt_ref, aug_ref, ckva_ref, ckvt_ref, wuvt_ref, yt_ref,
               score_scr, bias_scr, lg_scr, o_scr, sort_scr, *, W, tq, S, H, dh, Hi, di, topk, cw):
    j = pl.program_id(1)
    qpos = j * tq + lax.broadcasted_iota(I32, (1, tq), 1)
    score = score_scr.at[0:W, :]

    wit = wit_ref[0]
    for c in range(W // cw):
        kic = ki_ref[0, c * cw:(c + 1) * cw, :]
        acc = jnp.zeros((cw, tq), F32)
        for h in range(Hi):
            d = jnp.dot(kic, qit_ref[0, h * di:(h + 1) * di, :], preferred_element_type=F32)
            acc = acc + wit[h:h + 1, :] * jnp.maximum(d, 0.0)
        kpos_c = c * cw + lax.broadcasted_iota(I32, (cw, tq), 0)
        score_scr[c * cw:(c + 1) * cw, :] = jnp.where(kpos_c <= qpos, acc, -jnp.inf)

    def count(hit):
        return jnp.sum(jnp.where(hit, 1.0, 0.0), axis=0, keepdims=True)

    Wb = W // SORT_GROUP
    net = _sort_network(SORT_GROUP)

    def column(r, lg):
        return [(slice(g * Wb + r * SUBLANES, g * Wb + (r + 1) * SUBLANES), slice(lg * LANES, (lg + 1) * LANES))
                for g in range(SORT_GROUP)]

    for r in range(Wb // SUBLANES):
        for lg in range(tq // LANES):
            v = [score_scr[idx] for idx in column(r, lg)]
            for a, b in net:
                v[a], v[b] = jnp.maximum(v[a], v[b]), jnp.minimum(v[a], v[b])
            for idx, val in zip(column(r, lg), v):
                sort_scr[idx] = val

    def count_ge(cand_f):
        parts = []
        for lg in range(tq // LANES):
            c = cand_f[:, lg * LANES:(lg + 1) * LANES]
            acc = None
            for r in range(Wb // SUBLANES):
                n = _count_ge_sorted([sort_scr[idx] for idx in column(r, lg)], c)
                acc = n if acc is None else acc + n
            parts.append(jnp.sum(acc, axis=0, keepdims=True))
        return jnp.concatenate(parts, axis=1)

    def thr_step(i, t):
        cand = t + lax.shift_left(jnp.int32(1), 31 - i)
        cnt = count_ge(_key_to_f32(cand))
        return jnp.where(cnt >= float(topk), cand, t)

    t = lax.fori_loop(0, 32, thr_step, jnp.full((1, tq), INT_MIN, I32))
    few = t == INT_MIN
    thr_f = _key_to_f32(t)
    up_f = _key_to_f32(t + 1)
    need = float(topk) - count(score[...] >= up_f)

    tc = min(TIE_CHUNK, W)
    earlier = (lax.broadcasted_iota(I32, (tc, tc), 1) < lax.broadcasted_iota(I32, (tc, tc), 0)).astype(BF16)
    tied_before = jnp.zeros((1, tq), F32)
    for c in range(W // tc):
        rows = slice(c * tc, (c + 1) * tc)
        s = score_scr[rows, :]
        gt = s >= up_f
        eq = (s >= thr_f) & jnp.logical_not(gt)
        eq_f = jnp.where(eq, 1.0, 0.0)
        rank = jnp.dot(earlier, eq_f.astype(BF16), preferred_element_type=F32) + tied_before
        tied_before = tied_before + jnp.sum(eq_f, axis=0, keepdims=True)
        kpos = c * tc + lax.broadcasted_iota(I32, (tc, tq), 0)
        sel = (few | gt | (eq & (rank < need))) & (kpos <= qpos)
        bias_scr[rows, :] = jnp.where(sel, 0.0, NEG_BIG)

    R = qlt_ref.shape[2]
    ck = min(ATTN_KEY_CHUNK, W)

    def head(h, carry):
        qa = jnp.concatenate([qlt_ref[0, h], aug_ref[h]], axis=0)
        m = None
        for c in range(W // ck):
            keys = slice(c * ck, (c + 1) * ck)
            lg = jnp.dot(ckva_ref[0, keys, :], qa, preferred_element_type=F32) + bias_scr[keys, :]
            lg_scr[keys, :] = lg
            mc = jnp.max(lg, axis=0, keepdims=True)
            m = mc if m is None else jnp.maximum(m, mc)
        acc = jnp.zeros((R + ONES_ROWS, tq), F32)
        for c in range(W // ck):
            keys = slice(c * ck, (c + 1) * ck)
            p = jnp.exp2(lg_scr[keys, :] - m)
            acc = acc + jnp.dot(ckvt_ref[0, :, keys], p.astype(BF16), preferred_element_type=F32)
        o_scr[h] = (acc[0:R] / acc[R:R + 1]).astype(BF16)
        return carry

    lax.fori_loop(0, H, head, 0, unroll=ATTN_HEAD_UNROLL)
    for h in range(H):
        yt = jnp.dot(wuvt_ref[h], o_scr[h], preferred_element_type=F32)
        yt_ref[0, h * dh:(h + 1) * dh, :] = yt.astype(BF16)


def _attn_kernel(*refs, tq, cw, S, **kw):
    j = pl.program_id(1)
    nc = ((j + 1) * tq + cw - 1) // cw
    for w in range(1, S // cw + 1):
        @pl.when(nc == w)
        def _():
            _attn_body(*refs, W=w * cw, tq=tq, S=S, cw=cw, **kw)


def _attention(qit, wit, ki, qlt, ckva, ckvt, wuvt, *, tq, topk):
    B, H, R, S = qlt.shape
    dh = wuvt.shape[1]
    di = ki.shape[2]
    Hi = wit.shape[1]
    slopes = jnp.asarray([LOG2E * 2.0 ** (-8.0 * (h + 1) / H) for h in range(H)], F32)
    s_hi = slopes.astype(BF16)
    s_lo = (slopes - s_hi.astype(F32)).astype(BF16)
    aug = jnp.zeros((H, R, tq), BF16)
    aug = aug.at[:, 0:2, :].set(s_hi[:, None, None]).at[:, 2:4, :].set(s_lo[:, None, None])
    kern = functools.partial(_attn_kernel, tq=tq, S=S, H=H, dh=dh, Hi=Hi, di=di, topk=topk, cw=min(S, 512))
    col = lambda n: pl.BlockSpec((1, n, tq), lambda b, j: (b, 0, j))
    full = lambda a: pl.BlockSpec(a.shape, lambda b, j: (0,) * a.ndim)
    return pl.pallas_call(
        kern,
        grid=(B, S // tq),
        in_specs=[col(qit.shape[1]), col(Hi), pl.BlockSpec((1, S, di), lambda b, j: (b, 0, 0)),
                  pl.BlockSpec((1, H, R, tq), lambda b, j: (b, 0, 0, j)), full(aug),
                  pl.BlockSpec((1, S, 2 * R), lambda b, j: (b, 0, 0)),
                  pl.BlockSpec((1, R + ONES_ROWS, S), lambda b, j: (b, 0, 0)), full(wuvt)],
        out_specs=col(H * dh),
        out_shape=jax.ShapeDtypeStruct((B, H * dh, S), BF16),
        scratch_shapes=[pltpu.VMEM((S, tq), F32), pltpu.VMEM((S, tq), F32), pltpu.VMEM((S, tq), F32),
                        pltpu.VMEM((H, R, tq), BF16), pltpu.VMEM((S, tq), F32)],
        compiler_params=pltpu.CompilerParams(
            dimension_semantics=("parallel", "parallel"), vmem_limit_bytes=48 * 2 ** 20),
    )(qit, wit, ki, qlt, aug, ckva, ckvt, wuvt)


def _merge_kernel(x_ref, yat_ref, u_ref, halo_ref, g_ref, mod_ref, wpg_ref, psc_ref, wba_ref, wbp_ref,
                  wout_ref, ln1g_ref, ln1b_ref, wr_ref, br_ref,
                  x1_o, h2_o, te_o, tg_o, ext_scr, *, tm, D, G, alpha, E, K):
    i = pl.program_id(1)

    @pl.when(i == 0)
    def _():
        ext_scr[0:POOL_HALO, :] = jnp.zeros((POOL_HALO, ext_scr.shape[1]), F32)

    @pl.when(i > 0)
    def _():
        ext_scr[0:POOL_HALO, :] = halo_ref[0]

    ext_scr[POOL_HALO:POOL_HALO + tm, :] = u_ref[0]

    pos = i * tm + lax.broadcasted_iota(I32, (tm, 1), 0)
    ys = []
    for g, w in enumerate(POOL_WINDOWS):
        cols = slice(g * G, (g + 1) * G)
        cur = ext_scr[POOL_HALO:POOL_HALO + tm, cols]
        acc = cur
        for back in range(1, w):
            acc = acc + ext_scr[POOL_HALO - back:POOL_HALO - back + tm, cols]
        cnt = jnp.minimum(pos + 1, w).astype(F32)
        pooled = acc / cnt - cur
        ys.append(jnp.dot(pooled.astype(BF16), wpg_ref[g], preferred_element_type=F32))
    y_pool = jnp.concatenate(ys, axis=1) * psc_ref[...]

    gates = g_ref[0]
    ya = yat_ref[0].astype(F32).T.astype(BF16)
    a = jnp.dot(ya, wba_ref[...], preferred_element_type=F32)
    p = jnp.dot(y_pool.astype(BF16), wbp_ref[...], preferred_element_type=F32)
    mix = _sigmoid(gates[:, :D].astype(F32)) * a + _sigmoid(gates[:, D:].astype(F32)) * p
    o = jnp.dot(mix.astype(BF16), wout_ref[...], preferred_element_type=F32)
    mod = mod_ref[0]
    x1 = _layer_norm(alpha * x_ref[0] + mod[2:3] * o, ln1g_ref[...], ln1b_ref[...])
    x1_o[0] = x1
    h2 = x1 * (1.0 + mod[4:5]) + mod[3:4]
    _store_row_slabs(h2_o.at[0], h2)

    h2_hi = h2.astype(BF16)
    h2_lo = (h2 - h2_hi.astype(F32)).astype(BF16)
    logits = (jnp.dot(h2_hi, wr_ref[0], preferred_element_type=F32)
              + jnp.dot(h2_hi, wr_ref[1], preferred_element_type=F32)
              + jnp.dot(h2_lo, wr_ref[0], preferred_element_type=F32)) + br_ref[...]
    lane = lax.broadcasted_iota(I32, logits.shape, 1)
    logits = jnp.where(lane < E, logits, -jnp.inf)
    lane_f = lane.astype(F32)
    te_f = jnp.zeros(logits.shape, F32)
    vals = []
    for k in range(K):
        m = jnp.max(logits, axis=1, keepdims=True)
        idx = jnp.min(jnp.where(logits == m, lane_f, float(LANES)), axis=1, keepdims=True)
        te_f = jnp.where(lane == k, idx, te_f)
        vals.append(m)
        logits = jnp.where(lane_f == idx, -jnp.inf, logits)
    te = te_f.astype(I32)
    ex = [jnp.exp(v - vals[0]) for v in vals]
    den = ex[0]
    for e in ex[1:]:
        den = den + e
    tg = jnp.zeros(logits.shape, F32)
    for k in range(K):
        tg = jnp.where(lane == k, ex[k] / den, tg)
    te_o[0] = te
    tg_o[0] = tg


def _merge(x, yat, u, g, mod, wpg, psc, wba, wbp, wout, ln1g, ln1b, wr, br, *, tm, alpha, E, K):
    B, S, D = x.shape
    PW = u.shape[2]
    G = wpg.shape[1]
    kern = functools.partial(_merge_kernel, tm=tm, D=D, G=G, alpha=alpha, E=E, K=K)
    row = lambda n: pl.BlockSpec((1, tm, n), lambda b, i: (b, i, 0))
    full = lambda a: pl.BlockSpec(a.shape, lambda b, i: (0,) * a.ndim)
    hb = tm // POOL_HALO
    halo = pl.BlockSpec((1, POOL_HALO, PW), lambda b, i: (b, jnp.maximum(i * hb - 1, 0), 0))
    return pl.pallas_call(
        kern,
        grid=(B, S // tm),
        in_specs=[row(D), pl.BlockSpec((1, yat.shape[1], tm), lambda b, i: (b, 0, i)), row(PW), halo,
                  row(g.shape[2]), pl.BlockSpec((1, N_MOD, D), lambda b, i: (b, 0, 0)),
                  full(wpg), full(psc), full(wba), full(wbp), full(wout), full(ln1g), full(ln1b),
                  full(wr), full(br)],
        out_specs=[row(D), pl.BlockSpec((1, tm * (D // LANES), LANES), lambda b, i: (b, i, 0)),
                   row(LANES), row(LANES)],
        out_shape=[jax.ShapeDtypeStruct((B, S, D), F32),
                   jax.ShapeDtypeStruct((B, S * (D // LANES), LANES), F32),
                   jax.ShapeDtypeStruct((B, S, LANES), I32), jax.ShapeDtypeStruct((B, S, LANES), F32)],
        scratch_shapes=[pltpu.VMEM((POOL_HALO + tm, PW), F32)],
        compiler_params=pltpu.CompilerParams(
            dimension_semantics=("parallel", "arbitrary"), vmem_limit_bytes=48 * 2 ** 20),
    )(x, yat, u, u, g, mod, wpg, psc, wba, wbp, wout, ln1g, ln1b, wr, br)


def _route_kernel(te_ref, rank_o, cnt_o, base_scr, *, tR, K):
    i = pl.program_id(0)

    @pl.when(i == 0)
    def _():
        base_scr[...] = jnp.zeros(base_scr.shape, F32)

    te = te_ref[...]
    lane = lax.broadcasted_iota(I32, te.shape, 1)
    hits = [lane == te[:, k:k + 1] for k in range(K)]
    onehot = hits[0].astype(F32)
    for k in range(1, K):
        onehot = onehot + hits[k].astype(F32)
    earlier = (lax.broadcasted_iota(I32, (tR, tR), 0) > lax.broadcasted_iota(I32, (tR, tR), 1)).astype(BF16)
    before = jnp.dot(earlier, onehot.astype(BF16), preferred_element_type=F32) + base_scr[...]
    rank = jnp.zeros(te.shape, I32)
    for k in range(K):
        rk = jnp.sum(jnp.where(hits[k], before, 0.0), axis=1, keepdims=True)
        rank = jnp.where(lane == k, rk.astype(I32), rank)
    rank_o[...] = rank
    base_scr[...] = base_scr[...] + jnp.sum(onehot, axis=0, keepdims=True)
    cnt_o[...] = base_scr[...].astype(I32)


def _route(te, *, tR, K):
    T = te.shape[0]
    return pl.pallas_call(
        functools.partial(_route_kernel, tR=tR, K=K),
        grid=(T // tR,),
        in_specs=[pl.BlockSpec((tR, LANES), lambda i: (i, 0))],
        out_specs=[pl.BlockSpec((tR, LANES), lambda i: (i, 0)), pl.BlockSpec((1, LANES), lambda i: (0, 0))],
        out_shape=[jax.ShapeDtypeStruct((T, LANES), I32), jax.ShapeDtypeStruct((1, LANES), I32)],
        scratch_shapes=[pltpu.VMEM((1, LANES), F32)],
        compiler_params=pltpu.CompilerParams(dimension_semantics=("arbitrary",)),
    )(te)


def _dispatch_kernel(pe_ref, nu_ref, pos_ref, h2_ref, xs_out, zero_scr, sem, zsem,
                     *, tD, K, sub, E, tmE, n_blocks):
    blk = tmE * sub

    @pl.when(pl.program_id(0) == 0)
    def _():
        zero_scr[...] = jnp.zeros(zero_scr.shape, zero_scr.dtype)

        def zero_block(b):
            return pltpu.make_async_copy(zero_scr, xs_out.at[pl.ds(pl.multiple_of(b * blk, blk), blk), :], zsem)

        def last_block(e):
            return jnp.maximum(pe_ref[e] // tmE - 1, 0)

        def start_last(e, carry):
            zero_block(last_block(e)).start()
            return carry

        def start_tail(b, carry):
            zero_block(b).start()
            return carry

        def wait_last(e, carry):
            zero_block(last_block(e)).wait()
            return carry

        def wait_tail(b, carry):
            zero_block(b).wait()
            return carry

        lax.fori_loop(0, E, start_last, 0)
        lax.fori_loop(nu_ref[0], n_blocks, start_tail, 0)
        lax.fori_loop(0, E, wait_last, 0)
        lax.fori_loop(nu_ref[0], n_blocks, wait_tail, 0)

    def issue(r, carry):
        src = h2_ref.at[pl.ds(pl.multiple_of(r * sub, sub), sub), :]
        for k in range(K):
            p = pl.multiple_of(pos_ref[0, 0, k * tD + r], sub)
            pltpu.async_copy(src, xs_out.at[pl.ds(p, sub), :], sem, priority=k % 2)
        return carry

    lax.fori_loop(0, tD, issue, 0, unroll=8)
    for k in range(K):
        pltpu.make_async_copy(h2_ref, xs_out.at[pl.ds(0, tD * sub), :], sem).wait()


def _dispatch(pad_end, n_used, pos, h2, n_blocks, *, tD, tmE, K, sub):
    T = h2.shape[0] // sub
    E = pad_end.shape[0]
    grid_spec = pltpu.PrefetchScalarGridSpec(
        num_scalar_prefetch=2,
        grid=(T // tD,),
        in_specs=[pl.BlockSpec((1, 1, K * tD), lambda i, pe, nu: (i, 0, 0), memory_space=pltpu.SMEM),
                  pl.BlockSpec((tD * sub, LANES), lambda i, pe, nu: (i, 0))],
        out_specs=pl.BlockSpec(memory_space=pl.ANY),
        scratch_shapes=[pltpu.VMEM((tmE * sub, LANES), h2.dtype), pltpu.SemaphoreType.DMA(()),
                        pltpu.SemaphoreType.DMA(())],
    )
    return pl.pallas_call(
        functools.partial(_dispatch_kernel, tD=tD, K=K, sub=sub, E=E, tmE=tmE, n_blocks=n_blocks),
        grid_spec=grid_spec,
        out_shape=jax.ShapeDtypeStruct((n_blocks * tmE * sub, LANES), h2.dtype),
        compiler_params=pltpu.CompilerParams(
            dimension_semantics=("arbitrary",), has_side_effects=True),
    )(pad_end, n_used, pos, h2)


def _expert_kernel(be_ref, nu_ref, nx_ref, sl_ref, x_ref, wgu_hbm, bgu_ref, wd_hbm, bd_ref, y_ref,
                   wgu_f32, wd_f32, wgu_bf, wd_bf, sem, *, F, tmE, sub):
    i = pl.program_id(0)
    used = i < nu_ref[0]
    e = be_ref[i]
    slot = sl_ref[i]
    new_expert = used & ((i == 0) | (e != be_ref[jnp.maximum(i - 1, 0)]))

    def fetch(expert, s):
        return (pltpu.make_async_copy(wgu_hbm.at[expert], wgu_f32.at[s], sem.at[s, 0]),
                pltpu.make_async_copy(wd_hbm.at[expert], wd_f32.at[s], sem.at[s, 1]))

    @pl.when(used & (i == 0))
    def _():
        for cp in fetch(e, slot):
            cp.start()

    @pl.when(new_expert)
    def _():
        for cp in fetch(e, slot):
            cp.wait()

        @pl.when(nx_ref[i] >= 0)
        def _():
            for cp in fetch(nx_ref[i], 1 - slot):
                cp.start()

        wgu_bf[...] = wgu_f32[slot].astype(BF16)
        wd_bf[...] = wd_f32[slot].astype(BF16)

    @pl.when(used)
    def _():
        x = _load_row_slabs(x_ref, tmE, sub).astype(BF16)
        gu = jnp.dot(x, wgu_bf[...], preferred_element_type=F32) + bgu_ref[0]
        glu = jnp.minimum(gu[:, :F], SWIGLU_LIMIT)
        lin = jnp.clip(gu[:, F:], -SWIGLU_LIMIT, SWIGLU_LIMIT)
        act = glu * jax.nn.sigmoid(SWIGLU_ALPHA * glu) * (lin + 1.0)
        y = jnp.dot(act.astype(BF16), wd_bf[...], preferred_element_type=F32) + bd_ref[0]
        _store_row_slabs(y_ref, y)

    @pl.when(jnp.logical_not(used))
    def _():
        y_ref[...] = jnp.zeros(y_ref.shape, F32)


def _experts(block_e, n_used, next_e, slot, xs, wgu, bgu, wd, bd, *, tmE):
    E, D, F2 = wgu.shape
    F = F2 // 2
    sub = D // LANES
    n_blocks = xs.shape[0] // (tmE * sub)
    slab_block = pl.BlockSpec((tmE * sub, LANES), lambda i, be, nu, nx, sl: (i, 0))
    grid_spec = pltpu.PrefetchScalarGridSpec(
        num_scalar_prefetch=4,
        grid=(n_blocks,),
        in_specs=[slab_block,
                  pl.BlockSpec(memory_space=pl.ANY),
                  pl.BlockSpec((1, 1, F2), lambda i, be, nu, nx, sl: (be[i], 0, 0)),
                  pl.BlockSpec(memory_space=pl.ANY),
                  pl.BlockSpec((1, 1, D), lambda i, be, nu, nx, sl: (be[i], 0, 0))],
        out_specs=slab_block,
        scratch_shapes=[pltpu.VMEM((2, D, F2), F32), pltpu.VMEM((2, F, D), F32),
                        pltpu.VMEM((D, F2), BF16), pltpu.VMEM((F, D), BF16),
                        pltpu.SemaphoreType.DMA((2, 2))],
    )
    return pl.pallas_call(
        functools.partial(_expert_kernel, F=F, tmE=tmE, sub=sub),
        grid_spec=grid_spec,
        out_shape=jax.ShapeDtypeStruct(xs.shape, F32),
        compiler_params=pltpu.CompilerParams(
            dimension_semantics=("arbitrary",), vmem_limit_bytes=56 * 2 ** 20),
    )(block_e, n_used, next_e, slot, xs, wgu, bgu, wd, bd)


def _combine_kernel(pos_ref, nxt_ref, y_hbm, x1_ref, tg_ref, mod_ref, g_ref, b_ref, o_ref, buf0, buf1, sem,
                    *, tmC, K, alpha, nt, sub):
    i = pl.program_id(0)
    rows_per_group = tmC // COMBINE_GROUPS

    def start_row(p_ref, dst, s, r):
        row0 = r * sub if isinstance(r, int) else pl.multiple_of(r * sub, sub)
        for k in range(K):
            p = pl.multiple_of(p_ref[0, 0, k * tmC + r], sub)
            pltpu.async_copy(y_hbm.at[pl.ds(p, sub), :], dst.at[k, pl.ds(row0, sub), :], sem.at[s],
                             priority=k % 2)

    def reduce_rows(cur, g):
        rows = slice(g * rows_per_group, (g + 1) * rows_per_group)
        tg = tg_ref[rows, :]
        y = None
        for k in range(K):
            part = _load_row_slabs(cur.at[k, pl.ds(g * rows_per_group * sub, rows_per_group * sub), :],
                                   rows_per_group, sub)
            y = tg[:, k:k + 1] * part if y is None else y + tg[:, k:k + 1] * part
        mod = mod_ref[0]
        o_ref[rows, :] = _layer_norm(alpha * x1_ref[rows, :] + mod[5:6] * y, g_ref[...], b_ref[...])

    def step(cur, cur_s, nxt, nxt_s, prefetch):
        for k in range(K):
            pltpu.make_async_copy(y_hbm.at[pl.ds(0, tmC * sub), :], cur.at[k], sem.at[cur_s]).wait()
        for g in range(COMBINE_GROUPS):
            if prefetch:
                for r in range(g * rows_per_group, (g + 1) * rows_per_group):
                    start_row(nxt_ref, nxt, nxt_s, r)
            reduce_rows(cur, g)

    @pl.when(i == 0)
    def _():
        def issue(r, carry):
            start_row(pos_ref, buf0, 0, r)
            return carry
        lax.fori_loop(0, tmC, issue, 0, unroll=8)

    even = lax.rem(i, 2) == 0
    last = i + 1 == nt
    for is_even, (cur, cur_s, nxt, nxt_s) in ((True, (buf0, 0, buf1, 1)), (False, (buf1, 1, buf0, 0))):
        for is_last in (False, True):
            @pl.when((even == is_even) & (last == is_last))
            def _():
                step(cur, cur_s, nxt, nxt_s, prefetch=not is_last)


def _combine(pos, y_sorted, x1, tg, mod, ln2g, ln2b, *, tmC, K, alpha, S):
    T, D = x1.shape
    nt = T // tmC
    per_seq = S // tmC
    sub = D // LANES
    return pl.pallas_call(
        functools.partial(_combine_kernel, tmC=tmC, K=K, alpha=alpha, nt=nt, sub=sub),
        grid=(nt,),
        in_specs=[pl.BlockSpec((1, 1, K * tmC), lambda i: (i, 0, 0), memory_space=pltpu.SMEM),
                  pl.BlockSpec((1, 1, K * tmC), lambda i: (jnp.minimum(i + 1, nt - 1), 0, 0),
                               memory_space=pltpu.SMEM),
                  pl.BlockSpec(memory_space=pl.ANY),
                  pl.BlockSpec((tmC, D), lambda i: (i, 0)),
                  pl.BlockSpec((tmC, LANES), lambda i: (i, 0)),
                  pl.BlockSpec((1, N_MOD, D), lambda i: (i // per_seq, 0, 0)),
                  pl.BlockSpec((1, D), lambda i: (0, 0)),
                  pl.BlockSpec((1, D), lambda i: (0, 0))],
        out_specs=pl.BlockSpec((tmC, D), lambda i: (i, 0)),
        out_shape=jax.ShapeDtypeStruct((T, D), F32),
        scratch_shapes=[pltpu.VMEM((K, tmC * sub, LANES), F32), pltpu.VMEM((K, tmC * sub, LANES), F32),
                        pltpu.SemaphoreType.DMA((2,))],
        compiler_params=pltpu.CompilerParams(
            dimension_semantics=("arbitrary",), vmem_limit_bytes=48 * 2 ** 20),
    )(pos, pos, y_sorted, x1, tg, mod, ln2g, ln2b)


def _routing_tables(top_e, rank, counts, E, tmE, sub):
    T, K = top_e.shape
    padded = (counts + tmE - 1) // tmE * tmE
    pad_end = jnp.cumsum(padded)
    pad_start = pad_end - padded
    n_blocks = -(-(T * K) // tmE) + E
    block_start = jnp.arange(n_blocks, dtype=I32) * tmE
    block_e = jnp.minimum(jnp.sum((pad_end[None, :] <= block_start[:, None]).astype(I32), axis=1), E - 1)
    n_used = (pad_end[-1] // tmE).astype(I32).reshape(1)
    ids = jnp.arange(E, dtype=I32)

    def lookup(table, idx):
        return jnp.sum(jnp.where(idx[..., None] == ids, table.astype(I32), 0), axis=-1)

    pos = (lookup(pad_start, top_e) + rank).astype(I32) * sub
    nonempty = counts > 0
    at_or_after = jnp.flip(lax.cummin(jnp.flip(jnp.where(nonempty, ids, E))))
    after = jnp.concatenate([at_or_after[1:], jnp.full((1,), E, I32)])
    next_e = lookup(jnp.where(after < E, after, -1), block_e)
    slot = lookup((jnp.cumsum(nonempty.astype(I32)) - 1) % 2, block_e)
    return block_e.astype(I32), n_used, next_e, slot, pad_end.astype(I32), pos, n_blocks


def _tile_major(pos, tile):
    T, K = pos.shape
    return pos.reshape(T // tile, tile, K).transpose(0, 2, 1).reshape(T // tile, 1, K * tile)


def _layer(x, mod, w_in, kv_norm_g, w_uk, w_uv, w_pool_group, pool_scale, w_branch_attn, w_branch_pool,
           w_out, ln1_g, ln1_b, w_router, b_router, w_gate_up, b_gate_up, w_down, b_down, ln2_g, ln2_b,
           *, alpha):
    B, S, D = x.shape
    R, H, dh = w_uk.shape
    AW = H * dh
    PW = pool_scale.shape[0]
    E = w_router.shape[1]
    K = TOP_K_EXPERTS
    n_idx = w_in.shape[1] - (AW + R + PW + 2 * D)
    di = dh
    Hi = (n_idx - di) // (di + 1)
    sizes = (AW, R, Hi * di, di, Hi, PW, 2 * D)
    offs = [0]
    for s in sizes:
        offs.append(offs[-1] + s)
    ws = [w_in[:, offs[k]:offs[k + 1]].astype(BF16) for k in range(len(sizes))]
    ws[4] = jnp.pad(ws[4], ((0, 0), (0, LANES - Hi)))

    wukt = w_uk.transpose(1, 0, 2).astype(BF16)
    wuvt = w_uv.transpose(1, 2, 0).astype(BF16)
    qlt, ckva, ckvt, qit, ki, wit, u, g = _inproj(
        x, mod, ws, kv_norm_g.reshape(1, R), wukt, q_scale=dh ** -0.5 * LOG2E, qi_scale=di ** -0.5,
        wi_scale=Hi ** -0.5, Hi=Hi, tm=min(S, 512))

    topk = min(TOPK_MAX, S // 4)
    y_attn_t = _attention(qit, wit, ki, qlt, ckva, ckvt, wuvt, tq=min(S, 256), topk=topk)

    wr = jnp.pad(w_router, ((0, 0), (0, LANES - E)))
    wr_hi = wr.astype(BF16)
    wr = jnp.stack([wr_hi, (wr - wr_hi.astype(F32)).astype(BF16)])
    br = jnp.pad(b_router, (0, LANES - E)).reshape(1, LANES)
    x1, h2, te, tg = _merge(x, y_attn_t, u, g, mod, w_pool_group.astype(BF16), pool_scale.reshape(1, PW),
                            w_branch_attn.astype(BF16), w_branch_pool.astype(BF16), w_out.astype(BF16),
                            ln1_g.reshape(1, D), ln1_b.reshape(1, D), wr, br,
                            tm=min(S, 256), alpha=alpha, E=E, K=K)

    T = B * S
    tmE = 512
    tmC = min(S, 256)
    tD = min(T, 1024)
    sub = D // LANES
    te = te.reshape(T, LANES)
    rank, counts = _route(te, tR=min(T, 512), K=K)
    block_e, n_used, next_e, slot, pad_end, pos, n_blocks = _routing_tables(
        te[:, :K], rank[:, :K], counts[0, :E], E, tmE, sub)
    xs = _dispatch(pad_end, n_used, _tile_major(pos, tD), h2.reshape(T * sub, LANES), n_blocks,
                   tD=tD, tmE=tmE, K=K, sub=sub)
    y_sorted = _experts(block_e, n_used, next_e, slot, xs, w_gate_up, b_gate_up.reshape(E, 1, -1),
                        w_down, b_down.reshape(E, 1, D), tmE=tmE)
    out = _combine(_tile_major(pos, tmC), y_sorted, x1.reshape(T, D), tg.reshape(T, LANES), mod,
                   ln2_g.reshape(1, D), ln2_b.reshape(1, D), tmC=tmC, K=K, alpha=alpha, S=S)
    return out.reshape(B, S, D)


def kernel(x, c, w_ada, b_ada, w_in, kv_norm_g, w_uk, w_uv, w_pool_group, pool_scale, w_branch_attn,
           w_branch_pool, w_out, ln1_g, ln1_b, w_router, b_router, w_gate_up, b_gate_up, w_down, b_down,
           ln2_g, ln2_b):
    B, S, D = x.shape
    depth = w_ada.shape[0]
    alpha = (2.0 * depth) ** 0.25
    for l in range(depth):
        mod = _ada(c, w_ada[l], b_ada[l]).reshape(B, N_MOD, D)
        x = _layer(x, mod, w_in[l], kv_norm_g[l], w_uk[l], w_uv[l], w_pool_group[l], pool_scale[l],
                   w_branch_attn[l], w_branch_pool[l], w_out[l], ln1_g[l], ln1_b[l], w_router[l],
                   b_router[l], w_gate_up[l], b_gate_up[l], w_down[l], b_down[l], ln2_g[l], ln2_b[l],
                   alpha=alpha)
    return x
```

```python
import functools

import jax
import jax.numpy as jnp
from jax import lax
from jax.experimental import pallas as pl
from jax.experimental.pallas import tpu as pltpu

F32 = jnp.float32
BF16 = jnp.bfloat16
I32 = jnp.int32
HIGHEST = lax.Precision.HIGHEST

LN_EPS = 1e-5
TOPK_MAX = 256
TOP_K_EXPERTS = 4
POOL_WINDOWS = (2, 4, 8, 16)
SWIGLU_LIMIT = 7.0
SWIGLU_ALPHA = 1.702
N_MOD = 6
INT_MIN = -(2 ** 31)
NEG_BIG = -1e30
LANES = 128
SUBLANES = 8
SORT_GROUP = 16
TIE_CHUNK = 256
COMBINE_GROUPS = 8
POOL_HALO = 16
ONES_ROWS = 16
LOG2E = 1.4426950408889634
ATTN_KEY_CHUNK = 256
ATTN_HEAD_UNROLL = 8


def _store_row_slabs(ref, x):
    n, D = x.shape
    sub = D // LANES
    for s in range(sub):
        ref[pl.ds(s, n, stride=sub), :] = x[:, s * LANES:(s + 1) * LANES]


def _load_row_slabs(ref, n, sub):
    return jnp.concatenate([ref[pl.ds(s, n, stride=sub), :] for s in range(sub)], axis=1)


def _sigmoid(x):
    return 0.5 * jnp.tanh(0.5 * x) + 0.5


def _layer_norm(z, g, b):
    mu = jnp.mean(z, axis=-1, keepdims=True)
    zc = z - mu
    var = jnp.mean(zc * zc, axis=-1, keepdims=True)
    return zc * lax.rsqrt(var + LN_EPS) * g + b


def _ada_kernel(c_ref, w_ref, b_ref, o_ref):
    c = c_ref[...]
    cond = c * jax.nn.sigmoid(c)
    o_ref[...] = jnp.dot(cond, w_ref[...], precision=HIGHEST, preferred_element_type=F32) + b_ref[...]


def _ada(c, w_ada, b_ada):
    B, D = c.shape
    N = w_ada.shape[1]
    tn = D
    return pl.pallas_call(
        _ada_kernel,
        grid=(N // tn,),
        in_specs=[pl.BlockSpec((B, D), lambda j: (0, 0)),
                  pl.BlockSpec((D, tn), lambda j: (0, j)),
                  pl.BlockSpec((1, tn), lambda j: (0, j))],
        out_specs=pl.BlockSpec((B, tn), lambda j: (0, j)),
        out_shape=jax.ShapeDtypeStruct((B, N), F32),
    )(c, w_ada, b_ada.reshape(1, N))


def _inproj_kernel(x_ref, mod_ref, wq, wckv, wqi, wki, wwi, wu, wg, kvg_ref, wukt_ref,
                   qlt_o, ckva_o, ckvt_o, qit_o, ki_o, wit_o, u_o, g_o,
                   *, tm, H, dh, Hi, q_scale, qi_scale, wi_scale):
    i = pl.program_id(1)
    x = x_ref[0]
    mod = mod_ref[0]
    h = (x * (1.0 + mod[1:2]) + mod[0:1]).astype(BF16)

    def proj(w):
        return jnp.dot(h, w[...], preferred_element_type=F32)

    qt = proj(wq).T.astype(BF16)
    for hh in range(H):
        qlt = jnp.dot(wukt_ref[hh], qt[hh * dh:(hh + 1) * dh, :], preferred_element_type=F32) * q_scale
        qlt_o[0, hh] = qlt.astype(BF16)

    ckv = proj(wckv)
    ckv = ckv * lax.rsqrt(jnp.mean(ckv * ckv, axis=-1, keepdims=True) + LN_EPS) * kvg_ref[...]
    ckvt_o[0, 0:ckv.shape[1], :] = ckv.T.astype(BF16)
    ckvt_o[0, ckv.shape[1]:, :] = jnp.ones((ONES_ROWS, tm), BF16)
    R = ckv.shape[1]
    pos = i * tm + lax.broadcasted_iota(I32, (tm, R), 0)
    lane = lax.broadcasted_iota(I32, (tm, R), 1)
    pos_hi = ((pos >> 8) << 8).astype(F32)
    pos_lo = (pos & 255).astype(F32)
    extra = jnp.where((lane == 0) | (lane == 2), pos_hi, jnp.where((lane == 1) | (lane == 3), pos_lo, 0.0))
    ckva_o[0, :, 0:R] = ckv.astype(BF16)
    ckva_o[0, :, R:2 * R] = extra.astype(BF16)

    qit_o[0] = (proj(wqi) * qi_scale).T.astype(BF16)
    ki_o[0] = proj(wki).astype(BF16)
    wit_o[0] = (proj(wwi) * wi_scale).T[0:Hi, :]
    u_o[0] = proj(wu)
    g_o[0] = proj(wg).astype(BF16)


def _inproj(x, mod, ws, kv_norm_g, wukt, *, q_scale, qi_scale, wi_scale, Hi, tm):
    B, S, D = x.shape
    wq, wckv, wqi, wki, wwi, wu, wg = ws
    H, R, dh = wukt.shape
    row = lambda n: pl.BlockSpec((1, tm, n), lambda b, i: (b, i, 0))
    col = lambda n: pl.BlockSpec((1, n, tm), lambda b, i: (b, 0, i))
    full = lambda a: pl.BlockSpec(a.shape, lambda b, i: (0,) * a.ndim)
    out_specs = [pl.BlockSpec((1, H, R, tm), lambda b, i: (b, 0, 0, i)), row(2 * R), col(R + ONES_ROWS),
                 col(wqi.shape[1]),
                 row(wki.shape[1]), col(Hi), row(wu.shape[1]), row(wg.shape[1])]
    out_shape = [jax.ShapeDtypeStruct((B, H, R, S), BF16), jax.ShapeDtypeStruct((B, S, 2 * R), BF16),
                 jax.ShapeDtypeStruct((B, R + ONES_ROWS, S), BF16),
                 jax.ShapeDtypeStruct((B, wqi.shape[1], S), BF16),
                 jax.ShapeDtypeStruct((B, S, wki.shape[1]), BF16), jax.ShapeDtypeStruct((B, Hi, S), F32),
                 jax.ShapeDtypeStruct((B, S, wu.shape[1]), F32), jax.ShapeDtypeStruct((B, S, wg.shape[1]), BF16)]
    return pl.pallas_call(
        functools.partial(_inproj_kernel, tm=tm, H=H, dh=dh, Hi=Hi, q_scale=q_scale, qi_scale=qi_scale,
                          wi_scale=wi_scale),
        grid=(B, S // tm),
        in_specs=[row(D), pl.BlockSpec((1, N_MOD, D), lambda b, i: (b, 0, 0))]
                 + [full(w) for w in ws] + [full(kv_norm_g), full(wukt)],
        out_specs=out_specs,
        out_shape=out_shape,
        compiler_params=pltpu.CompilerParams(
            dimension_semantics=("parallel", "parallel"), vmem_limit_bytes=48 * 2 ** 20),
    )(x, mod, *ws, kv_norm_g, wukt)


def _key_to_f32(t):
    return pltpu.bitcast(jnp.where(t >= 0, t, t ^ 0x7FFFFFFF), F32)


def _sort_network(n):
    pairs, p = [], 1
    while p < n:
        k = p
        while k >= 1:
            for j in range(k % p, n - k, 2 * k):
                for i in range(min(k, n - j - k)):
                    if (i + j) // (2 * p) == (i + j + k) // (2 * p):
                        pairs.append((i + j, i + j + k))
            k //= 2
        p *= 2
    return pairs


def _count_ge_sorted(blocks, cand):
    n = len(blocks)

    def pick(level, bits):
        if len(level) == 1:
            return level[0]
        half = len(level) // 2
        return jnp.where(bits[0], pick(level[half:], bits[1:]), pick(level[:half], bits[1:]))

    total = jnp.where(blocks[n - 1] >= cand, 1.0, 0.0)
    bits = []
    step = n // 2
    while step >= 1:
        level = [blocks[c + step - 1] for c in range(0, n, 2 * step)]
        b = pick(level, bits) >= cand
        bits.append(b)
        total = total + jnp.where(b, float(step), 0.0)
        step //= 2
    return total


def _attn_body(qit_ref, wit_ref, ki_ref, qlt_ref, aug_ref, ckva_ref, ckvt_ref, wuvt_ref, yt_ref,
               score_scr, bias_scr, lg_scr, o_scr, sort_scr, *, W, tq, S, H, dh, Hi, di, topk, cw):
    j = pl.program_id(1)
    qpos = j * tq + lax.broadcasted_iota(I32, (1, tq), 1)
    score = score_scr.at[0:W, :]

    wit = wit_ref[0]
    for c in range(W // cw):
        kic = ki_ref[0, c * cw:(c + 1) * cw, :]
        acc = jnp.zeros((cw, tq), F32)
        for h in range(Hi):
            d = jnp.dot(kic, qit_ref[0, h * di:(h + 1) * di, :], preferred_element_type=F32)
            acc = acc + wit[h:h + 1, :] * jnp.maximum(d, 0.0)
        kpos_c = c * cw + lax.broadcasted_iota(I32, (cw, tq), 0)
        score_scr[c * cw:(c + 1) * cw, :] = jnp.where(kpos_c <= qpos, acc, -jnp.inf)

    def count(hit):
        return jnp.sum(jnp.where(hit, 1.0, 0.0), axis=0, keepdims=True)

    Wb = W // SORT_GROUP
    net = _sort_network(SORT_GROUP)

    def column(r, lg):
        return [(slice(g * Wb + r * SUBLANES, g * Wb + (r + 1) * SUBLANES), slice(lg * LANES, (lg + 1) * LANES))
                for g in range(SORT_GROUP)]

    for r in range(Wb // SUBLANES):
        for lg in range(tq // LANES):
            v = [score_scr[idx] for idx in column(r, lg)]
            for a, b in net:
                v[a], v[b] = jnp.maximum(v[a], v[b]), jnp.minimum(v[a], v[b])
            for idx, val in zip(column(r, lg), v):
                sort_scr[idx] = val

    def count_ge(cand_f):
        parts = []
        for lg in range(tq // LANES):
            c = cand_f[:, lg * LANES:(lg + 1) * LANES]
            acc = None
            for r in range(Wb // SUBLANES):
                n = _count_ge_sorted([sort_scr[idx] for idx in column(r, lg)], c)
                acc = n if acc is None else acc + n
            parts.append(jnp.sum(acc, axis=0, keepdims=True))
        return jnp.concatenate(parts, axis=1)

    def thr_step(i, t):
        cand = t + lax.shift_left(jnp.int32(1), 31 - i)
        cnt = count_ge(_key_to_f32(cand))
        return jnp.where(cnt >= float(topk), cand, t)

    t = lax.fori_loop(0, 32, thr_step, jnp.full((1, tq), INT_MIN, I32))
    few = t == INT_MIN
    thr_f = _key_to_f32(t)
    up_f = _key_to_f32(t + 1)
    need = float(topk) - count(score[...] >= up_f)

    tc = min(TIE_CHUNK, W)
    earlier = (lax.broadcasted_iota(I32, (tc, tc), 1) < lax.broadcasted_iota(I32, (tc, tc), 0)).astype(BF16)
    tied_before = jnp.zeros((1, tq), F32)
    for c in range(W // tc):
        rows = slice(c * tc, (c + 1) * tc)
        s = score_scr[rows, :]
        gt = s >= up_f
        eq = (s >= thr_f) & jnp.logical_not(gt)
        eq_f = jnp.where(eq, 1.0, 0.0)
        rank = jnp.dot(earlier, eq_f.astype(BF16), preferred_element_type=F32) + tied_before
        tied_before = tied_before + jnp.sum(eq_f, axis=0, keepdims=True)
        kpos = c * tc + lax.broadcasted_iota(I32, (tc, tq), 0)
        sel = (few | gt | (eq & (rank < need))) & (kpos <= qpos)
        bias_scr[rows, :] = jnp.where(sel, 0.0, NEG_BIG)

    R = qlt_ref.shape[2]
    ck = min(ATTN_KEY_CHUNK, W)

    def head(h, carry):
        qa = jnp.concatenate([qlt_ref[0, h], aug_ref[h]], axis=0)
        m = None
        for c in range(W // ck):
            keys = slice(c * ck, (c + 1) * ck)
            lg = jnp.dot(ckva_ref[0, keys, :], qa, preferred_element_type=F32) + bias_scr[keys, :]
            lg_scr[keys, :] = lg
            mc = jnp.max(lg, axis=0, keepdims=True)
            m = mc if m is None else jnp.maximum(m, mc)
        acc = jnp.zeros((R + ONES_ROWS, tq), F32)
        for c in range(W // ck):
            keys = slice(c * ck, (c + 1) * ck)
            p = jnp.exp2(lg_scr[keys, :] - m)
            acc = acc + jnp.dot(ckvt_ref[0, :, keys], p.astype(BF16), preferred_element_type=F32)
        o_scr[h] = (acc[0:R] / acc[R:R + 1]).astype(BF16)
        return carry

    lax.fori_loop(0, H, head, 0, unroll=ATTN_HEAD_UNROLL)
    for h in range(H):
        yt = jnp.dot(wuvt_ref[h], o_scr[h], preferred_element_type=F32)
        yt_ref[0, h * dh:(h + 1) * dh, :] = yt.astype(BF16)


def _attn_kernel(*refs, tq, cw, S, **kw):
    j = pl.program_id(1)
    nc = ((j + 1) * tq + cw - 1) // cw
    for w in range(1, S // cw + 1):
        @pl.when(nc == w)
        def _():
            _attn_body(*refs, W=w * cw, tq=tq, S=S, cw=cw, **kw)


def _attention(qit, wit, ki, qlt, ckva, ckvt, wuvt, *, tq, topk):
    B, H, R, S = qlt.shape
    dh = wuvt.shape[1]
    di = ki.shape[2]
    Hi = wit.shape[1]
    slopes = jnp.asarray([LOG2E * 2.0 ** (-8.0 * (h + 1) / H) for h in range(H)], F32)
    s_hi = slopes.astype(BF16)
    s_lo = (slopes - s_hi.astype(F32)).astype(BF16)
    aug = jnp.zeros((H, R, tq), BF16)
    aug = aug.at[:, 0:2, :].set(s_hi[:, None, None]).at[:, 2:4, :].set(s_lo[:, None, None])
    kern = functools.partial(_attn_kernel, tq=tq, S=S, H=H, dh=dh, Hi=Hi, di=di, topk=topk, cw=min(S, 512))
    col = lambda n: pl.BlockSpec((1, n, tq), lambda b, j: (b, 0, j))
    full = lambda a: pl.BlockSpec(a.shape, lambda b, j: (0,) * a.ndim)
    return pl.pallas_call(
        kern,
        grid=(B, S // tq),
        in_specs=[col(qit.shape[1]), col(Hi), pl.BlockSpec((1, S, di), lambda b, j: (b, 0, 0)),
                  pl.BlockSpec((1, H, R, tq), lambda b, j: (b, 0, 0, j)), full(aug),
                  pl.BlockSpec((1, S, 2 * R), lambda b, j: (b, 0, 0)),
                  pl.BlockSpec((1, R + ONES_ROWS, S), lambda b, j: (b, 0, 0)), full(wuvt)],
        out_specs=col(H * dh),
        out_shape=jax.ShapeDtypeStruct((B, H * dh, S), BF16),
        scratch_shapes=[pltpu.VMEM((S, tq), F32), pltpu.VMEM((S, tq), F32), pltpu.VMEM((S, tq), F32),
                        pltpu.VMEM((H, R, tq), BF16), pltpu.VMEM((S, tq), F32)],
        compiler_params=pltpu.CompilerParams(
            dimension_semantics=("parallel", "parallel"), vmem_limit_bytes=48 * 2 ** 20),
    )(qit, wit, ki, qlt, aug, ckva, ckvt, wuvt)


def _merge_kernel(x_ref, yat_ref, u_ref, halo_ref, g_ref, mod_ref, wpg_ref, psc_ref, wba_ref, wbp_ref,
                  wout_ref, ln1g_ref, ln1b_ref, wr_ref, br_ref,
                  x1_o, h2_o, te_o, tg_o, ext_scr, *, tm, D, G, alpha, E, K):
    i = pl.program_id(1)

    @pl.when(i == 0)
    def _():
        ext_scr[0:POOL_HALO, :] = jnp.zeros((POOL_HALO, ext_scr.shape[1]), F32)

    @pl.when(i > 0)
    def _():
        ext_scr[0:POOL_HALO, :] = halo_ref[0]

    ext_scr[POOL_HALO:POOL_HALO + tm, :] = u_ref[0]

    pos = i * tm + lax.broadcasted_iota(I32, (tm, 1), 0)
    ys = []
    for g, w in enumerate(POOL_WINDOWS):
        cols = slice(g * G, (g + 1) * G)
        cur = ext_scr[POOL_HALO:POOL_HALO + tm, cols]
        acc = cur
        for back in range(1, w):
            acc = acc + ext_scr[POOL_HALO - back:POOL_HALO - back + tm, cols]
        cnt = jnp.minimum(pos + 1, w).astype(F32)
        pooled = acc / cnt - cur
        ys.append(jnp.dot(pooled.astype(BF16), wpg_ref[g], preferred_element_type=F32))
    y_pool = jnp.concatenate(ys, axis=1) * psc_ref[...]

    gates = g_ref[0]
    ya = yat_ref[0].astype(F32).T.astype(BF16)
    a = jnp.dot(ya, wba_ref[...], preferred_element_type=F32)
    p = jnp.dot(y_pool.astype(BF16), wbp_ref[...], preferred_element_type=F32)
    mix = _sigmoid(gates[:, :D].astype(F32)) * a + _sigmoid(gates[:, D:].astype(F32)) * p
    o = jnp.dot(mix.astype(BF16), wout_ref[...], preferred_element_type=F32)
    mod = mod_ref[0]
    x1 = _layer_norm(alpha * x_ref[0] + mod[2:3] * o, ln1g_ref[...], ln1b_ref[...])
    x1_o[0] = x1
    h2 = x1 * (1.0 + mod[4:5]) + mod[3:4]
    _store_row_slabs(h2_o.at[0], h2)

    h2_hi = h2.astype(BF16)
    h2_lo = (h2 - h2_hi.astype(F32)).astype(BF16)
    logits = (jnp.dot(h2_hi, wr_ref[0], preferred_element_type=F32)
              + jnp.dot(h2_hi, wr_ref[1], preferred_element_type=F32)
              + jnp.dot(h2_lo, wr_ref[0], preferred_element_type=F32)) + br_ref[...]
    lane = lax.broadcasted_iota(I32, logits.shape, 1)
    logits = jnp.where(lane < E, logits, -jnp.inf)
    lane_f = lane.astype(F32)
    te_f = jnp.zeros(logits.shape, F32)
    vals = []
    for k in range(K):
        m = jnp.max(logits, axis=1, keepdims=True)
        idx = jnp.min(jnp.where(logits == m, lane_f, float(LANES)), axis=1, keepdims=True)
        te_f = jnp.where(lane == k, idx, te_f)
        vals.append(m)
        logits = jnp.where(lane_f == idx, -jnp.inf, logits)
    te = te_f.astype(I32)
    ex = [jnp.exp(v - vals[0]) for v in vals]
    den = ex[0]
    for e in ex[1:]:
        den = den + e
    tg = jnp.zeros(logits.shape, F32)
    for k in range(K):
        tg = jnp.where(lane == k, ex[k] / den, tg)
    te_o[0] = te
    tg_o[0] = tg


def _merge(x, yat, u, g, mod, wpg, psc, wba, wbp, wout, ln1g, ln1b, wr, br, *, tm, alpha, E, K):
    B, S, D = x.shape
    PW = u.shape[2]
    G = wpg.shape[1]
    kern = functools.partial(_merge_kernel, tm=tm, D=D, G=G, alpha=alpha, E=E, K=K)
    row = lambda n: pl.BlockSpec((1, tm, n), lambda b, i: (b, i, 0))
    full = lambda a: pl.BlockSpec(a.shape, lambda b, i: (0,) * a.ndim)
    hb = tm // POOL_HALO
    halo = pl.BlockSpec((1, POOL_HALO, PW), lambda b, i: (b, jnp.maximum(i * hb - 1, 0), 0))
    return pl.pallas_call(
        kern,
        grid=(B, S // tm),
        in_specs=[row(D), pl.BlockSpec((1, yat.shape[1], tm), lambda b, i: (b, 0, i)), row(PW), halo,
                  row(g.shape[2]), pl.BlockSpec((1, N_MOD, D), lambda b, i: (b, 0, 0)),
                  full(wpg), full(psc), full(wba), full(wbp), full(wout), full(ln1g), full(ln1b),
                  full(wr), full(br)],
        out_specs=[row(D), pl.BlockSpec((1, tm * (D // LANES), LANES), lambda b, i: (b, i, 0)),
                   row(LANES), row(LANES)],
        out_shape=[jax.ShapeDtypeStruct((B, S, D), F32),
                   jax.ShapeDtypeStruct((B, S * (D // LANES), LANES), F32),
                   jax.ShapeDtypeStruct((B, S, LANES), I32), jax.ShapeDtypeStruct((B, S, LANES), F32)],
        scratch_shapes=[pltpu.VMEM((POOL_HALO + tm, PW), F32)],
        compiler_params=pltpu.CompilerParams(
            dimension_semantics=("parallel", "arbitrary"), vmem_limit_bytes=48 * 2 ** 20),
    )(x, yat, u, u, g, mod, wpg, psc, wba, wbp, wout, ln1g, ln1b, wr, br)


def _route_kernel(te_ref, rank_o, cnt_o, base_scr, *, tR, K):
    i = pl.program_id(0)

    @pl.when(i == 0)
    def _():
        base_scr[...] = jnp.zeros(base_scr.shape, F32)

    te = te_ref[...]
    lane = lax.broadcasted_iota(I32, te.shape, 1)
    hits = [lane == te[:, k:k + 1] for k in range(K)]
    onehot = hits[0].astype(F32)
    for k in range(1, K):
        onehot = onehot + hits[k].astype(F32)
    earlier = (lax.broadcasted_iota(I32, (tR, tR), 0) > lax.broadcasted_iota(I32, (tR, tR), 1)).astype(BF16)
    before = jnp.dot(earlier, onehot.astype(BF16), preferred_element_type=F32) + base_scr[...]
    rank = jnp.zeros(te.shape, I32)
    for k in range(K):
        rk = jnp.sum(jnp.where(hits[k], before, 0.0), axis=1, keepdims=True)
        rank = jnp.where(lane == k, rk.astype(I32), rank)
    rank_o[...] = rank
    base_scr[...] = base_scr[...] + jnp.sum(onehot, axis=0, keepdims=True)
    cnt_o[...] = base_scr[...].astype(I32)


def _route(te, *, tR, K):
    T = te.shape[0]
    return pl.pallas_call(
        functools.partial(_route_kernel, tR=tR, K=K),
        grid=(T // tR,),
        in_specs=[pl.BlockSpec((tR, LANES), lambda i: (i, 0))],
        out_specs=[pl.BlockSpec((tR, LANES), lambda i: (i, 0)), pl.BlockSpec((1, LANES), lambda i: (0, 0))],
        out_shape=[jax.ShapeDtypeStruct((T, LANES), I32), jax.ShapeDtypeStruct((1, LANES), I32)],
        scratch_shapes=[pltpu.VMEM((1, LANES), F32)],
        compiler_params=pltpu.CompilerParams(dimension_semantics=("arbitrary",)),
    )(te)


def _dispatch_kernel(pe_ref, nu_ref, pos_ref, h2_ref, xs_out, zero_scr, sem, zsem,
                     *, tD, K, sub, E, tmE, n_blocks):
    blk = tmE * sub

    @pl.when(pl.program_id(0) == 0)
    def _():
        zero_scr[...] = jnp.zeros(zero_scr.shape, zero_scr.dtype)

        def zero_block(b):
            return pltpu.make_async_copy(zero_scr, xs_out.at[pl.ds(pl.multiple_of(b * blk, blk), blk), :], zsem)

        def last_block(e):
            return jnp.maximum(pe_ref[e] // tmE - 1, 0)

        def start_last(e, carry):
            zero_block(last_block(e)).start()
            return carry

        def start_tail(b, carry):
            zero_block(b).start()
            return carry

        def wait_last(e, carry):
            zero_block(last_block(e)).wait()
            return carry

        def wait_tail(b, carry):
            zero_block(b).wait()
            return carry

        lax.fori_loop(0, E, start_last, 0)
        lax.fori_loop(nu_ref[0], n_blocks, start_tail, 0)
        lax.fori_loop(0, E, wait_last, 0)
        lax.fori_loop(nu_ref[0], n_blocks, wait_tail, 0)

    def issue(r, carry):
        src = h2_ref.at[pl.ds(pl.multiple_of(r * sub, sub), sub), :]
        for k in range(K):
            p = pl.multiple_of(pos_ref[0, 0, k * tD + r], sub)
            pltpu.async_copy(src, xs_out.at[pl.ds(p, sub), :], sem, priority=k % 2)
        return carry

    lax.fori_loop(0, tD, issue, 0, unroll=8)
    for k in range(K):
        pltpu.make_async_copy(h2_ref, xs_out.at[pl.ds(0, tD * sub), :], sem).wait()


def _dispatch(pad_end, n_used, pos, h2, n_blocks, *, tD, tmE, K, sub):
    T = h2.shape[0] // sub
    E = pad_end.shape[0]
    grid_spec = pltpu.PrefetchScalarGridSpec(
        num_scalar_prefetch=2,
        grid=(T // tD,),
        in_specs=[pl.BlockSpec((1, 1, K * tD), lambda i, pe, nu: (i, 0, 0), memory_space=pltpu.SMEM),
                  pl.BlockSpec((tD * sub, LANES), lambda i, pe, nu: (i, 0))],
        out_specs=pl.BlockSpec(memory_space=pl.ANY),
        scratch_shapes=[pltpu.VMEM((tmE * sub, LANES), h2.dtype), pltpu.SemaphoreType.DMA(()),
                        pltpu.SemaphoreType.DMA(())],
    )
    return pl.pallas_call(
        functools.partial(_dispatch_kernel, tD=tD, K=K, sub=sub, E=E, tmE=tmE, n_blocks=n_blocks),
        grid_spec=grid_spec,
        out_shape=jax.ShapeDtypeStruct((n_blocks * tmE * sub, LANES), h2.dtype),
        compiler_params=pltpu.CompilerParams(
            dimension_semantics=("arbitrary",), has_side_effects=True),
    )(pad_end, n_used, pos, h2)


def _expert_kernel(be_ref, nu_ref, nx_ref, sl_ref, x_ref, wgu_hbm, bgu_ref, wd_hbm, bd_ref, y_ref,
                   wgu_f32, wd_f32, wgu_bf, wd_bf, sem, *, F, tmE, sub):
    i = pl.program_id(0)
    used = i < nu_ref[0]
    e = be_ref[i]
    slot = sl_ref[i]
    new_expert = used & ((i == 0) | (e != be_ref[jnp.maximum(i - 1, 0)]))

    def fetch(expert, s):
        return (pltpu.make_async_copy(wgu_hbm.at[expert], wgu_f32.at[s], sem.at[s, 0]),
                pltpu.make_async_copy(wd_hbm.at[expert], wd_f32.at[s], sem.at[s, 1]))

    @pl.when(used & (i == 0))
    def _():
        for cp in fetch(e, slot):
            cp.start()

    @pl.when(new_expert)
    def _():
        for cp in fetch(e, slot):
            cp.wait()

        @pl.when(nx_ref[i] >= 0)
        def _():
            for cp in fetch(nx_ref[i], 1 - slot):
                cp.start()

        wgu_bf[...] = wgu_f32[slot].astype(BF16)
        wd_bf[...] = wd_f32[slot].astype(BF16)

    @pl.when(used)
    def _():
        x = _load_row_slabs(x_ref, tmE, sub).astype(BF16)
        gu = jnp.dot(x, wgu_bf[...], preferred_element_type=F32) + bgu_ref[0]
        glu = jnp.minimum(gu[:, :F], SWIGLU_LIMIT)
        lin = jnp.clip(gu[:, F:], -SWIGLU_LIMIT, SWIGLU_LIMIT)
        act = glu * jax.nn.sigmoid(SWIGLU_ALPHA * glu) * (lin + 1.0)
        y = jnp.dot(act.astype(BF16), wd_bf[...], preferred_element_type=F32) + bd_ref[0]
        _store_row_slabs(y_ref, y)

    @pl.when(jnp.logical_not(used))
    def _():
        y_ref[...] = jnp.zeros(y_ref.shape, F32)


def _experts(block_e, n_used, next_e, slot, xs, wgu, bgu, wd, bd, *, tmE):
    E, D, F2 = wgu.shape
    F = F2 // 2
    sub = D // LANES
    n_blocks = xs.shape[0] // (tmE * sub)
    slab_block = pl.BlockSpec((tmE * sub, LANES), lambda i, be, nu, nx, sl: (i, 0))
    grid_spec = pltpu.PrefetchScalarGridSpec(
        num_scalar_prefetch=4,
        grid=(n_blocks,),
        in_specs=[slab_block,
                  pl.BlockSpec(memory_space=pl.ANY),
                  pl.BlockSpec((1, 1, F2), lambda i, be, nu, nx, sl: (be[i], 0, 0)),
                  pl.BlockSpec(memory_space=pl.ANY),
                  pl.BlockSpec((1, 1, D), lambda i, be, nu, nx, sl: (be[i], 0, 0))],
        out_specs=slab_block,
        scratch_shapes=[pltpu.VMEM((2, D, F2), F32), pltpu.VMEM((2, F, D), F32),
                        pltpu.VMEM((D, F2), BF16), pltpu.VMEM((F, D), BF16),
                        pltpu.SemaphoreType.DMA((2, 2))],
    )
    return pl.pallas_call(
        functools.partial(_expert_kernel, F=F, tmE=tmE, sub=sub),
        grid_spec=grid_spec,
        out_shape=jax.ShapeDtypeStruct(xs.shape, F32),
        compiler_params=pltpu.CompilerParams(
            dimension_semantics=("arbitrary",), vmem_limit_bytes=56 * 2 ** 20),
    )(block_e, n_used, next_e, slot, xs, wgu, bgu, wd, bd)


def _combine_kernel(pos_ref, nxt_ref, y_hbm, x1_ref, tg_ref, mod_ref, g_ref, b_ref, o_ref, buf0, buf1, sem,
                    *, tmC, K, alpha, nt, sub):
    i = pl.program_id(0)
    rows_per_group = tmC // COMBINE_GROUPS

    def start_row(p_ref, dst, s, r):
        row0 = r * sub if isinstance(r, int) else pl.multiple_of(r * sub, sub)
        for k in range(K):
            p = pl.multiple_of(p_ref[0, 0, k * tmC + r], sub)
            pltpu.async_copy(y_hbm.at[pl.ds(p, sub), :], dst.at[k, pl.ds(row0, sub), :], sem.at[s],
                             priority=k % 2)

    def reduce_rows(cur, g):
        rows = slice(g * rows_per_group, (g + 1) * rows_per_group)
        tg = tg_ref[rows, :]
        y = None
        for k in range(K):
            part = _load_row_slabs(cur.at[k, pl.ds(g * rows_per_group * sub, rows_per_group * sub), :],
                                   rows_per_group, sub)
            y = tg[:, k:k + 1] * part if y is None else y + tg[:, k:k + 1] * part
        mod = mod_ref[0]
        o_ref[rows, :] = _layer_norm(alpha * x1_ref[rows, :] + mod[5:6] * y, g_ref[...], b_ref[...])

    def step(cur, cur_s, nxt, nxt_s, prefetch):
        for k in range(K):
            pltpu.make_async_copy(y_hbm.at[pl.ds(0, tmC * sub), :], cur.at[k], sem.at[cur_s]).wait()
        for g in range(COMBINE_GROUPS):
            if prefetch:
                for r in range(g * rows_per_group, (g + 1) * rows_per_group):
                    start_row(nxt_ref, nxt, nxt_s, r)
            reduce_rows(cur, g)

    @pl.when(i == 0)
    def _():
        def issue(r, carry):
            start_row(pos_ref, buf0, 0, r)
            return carry
        lax.fori_loop(0, tmC, issue, 0, unroll=8)

    even = lax.rem(i, 2) == 0
    last = i + 1 == nt
    for is_even, (cur, cur_s, nxt, nxt_s) in ((True, (buf0, 0, buf1, 1)), (False, (buf1, 1, buf0, 0))):
        for is_last in (False, True):
            @pl.when((even == is_even) & (last == is_last))
            def _():
                step(cur, cur_s, nxt, nxt_s, prefetch=not is_last)


def _combine(pos, y_sorted, x1, tg, mod, ln2g, ln2b, *, tmC, K, alpha, S):
    T, D = x1.shape
    nt = T // tmC
    per_seq = S // tmC
    sub = D // LANES
    return pl.pallas_call(
        functools.partial(_combine_kernel, tmC=tmC, K=K, alpha=alpha, nt=nt, sub=sub),
        grid=(nt,),
        in_specs=[pl.BlockSpec((1, 1, K * tmC), lambda i: (i, 0, 0), memory_space=pltpu.SMEM),
                  pl.BlockSpec((1, 1, K * tmC), lambda i: (jnp.minimum(i + 1, nt - 1), 0, 0),
                               memory_space=pltpu.SMEM),
                  pl.BlockSpec(memory_space=pl.ANY),
                  pl.BlockSpec((tmC, D), lambda i: (i, 0)),
                  pl.BlockSpec((tmC, LANES), lambda i: (i, 0)),
                  pl.BlockSpec((1, N_MOD, D), lambda i: (i // per_seq, 0, 0)),
                  pl.BlockSpec((1, D), lambda i: (0, 0)),
                  pl.BlockSpec((1, D), lambda i: (0, 0))],
        out_specs=pl.BlockSpec((tmC, D), lambda i: (i, 0)),
        out_shape=jax.ShapeDtypeStruct((T, D), F32),
        scratch_shapes=[pltpu.VMEM((K, tmC * sub, LANES), F32), pltpu.VMEM((K, tmC * sub, LANES), F32),
                        pltpu.SemaphoreType.DMA((2,))],
        compiler_params=pltpu.CompilerParams(
            dimension_semantics=("arbitrary",), vmem_limit_bytes=48 * 2 ** 20),
    )(pos, pos, y_sorted, x1, tg, mod, ln2g, ln2b)


def _routing_tables(top_e, rank, counts, E, tmE, sub):
    T, K = top_e.shape
    padded = (counts + tmE - 1) // tmE * tmE
    pad_end = jnp.cumsum(padded)
    pad_start = pad_end - padded
    n_blocks = -(-(T * K) // tmE) + E
    block_start = jnp.arange(n_blocks, dtype=I32) * tmE
    block_e = jnp.minimum(jnp.sum((pad_end[None, :] <= block_start[:, None]).astype(I32), axis=1), E - 1)
    n_used = (pad_end[-1] // tmE).astype(I32).reshape(1)
    ids = jnp.arange(E, dtype=I32)

    def lookup(table, idx):
        return jnp.sum(jnp.where(idx[..., None] == ids, table.astype(I32), 0), axis=-1)

    pos = (lookup(pad_start, top_e) + rank).astype(I32) * sub
    nonempty = counts > 0
    at_or_after = jnp.flip(lax.cummin(jnp.flip(jnp.where(nonempty, ids, E))))
    after = jnp.concatenate([at_or_after[1:], jnp.full((1,), E, I32)])
    next_e = lookup(jnp.where(after < E, after, -1), block_e)
    slot = lookup((jnp.cumsum(nonempty.astype(I32)) - 1) % 2, block_e)
    return block_e.astype(I32), n_used, next_e, slot, pad_end.astype(I32), pos, n_blocks


def _tile_major(pos, tile):
    T, K = pos.shape
    return pos.reshape(T // tile, tile, K).transpose(0, 2, 1).reshape(T // tile, 1, K * tile)


def _layer(x, mod, w_in, kv_norm_g, w_uk, w_uv, w_pool_group, pool_scale, w_branch_attn, w_branch_pool,
           w_out, ln1_g, ln1_b, w_router, b_router, w_gate_up, b_gate_up, w_down, b_down, ln2_g, ln2_b,
           *, alpha):
    B, S, D = x.shape
    R, H, dh = w_uk.shape
    AW = H * dh
    PW = pool_scale.shape[0]
    E = w_router.shape[1]
    K = TOP_K_EXPERTS
    n_idx = w_in.shape[1] - (AW + R + PW + 2 * D)
    di = dh
    Hi = (n_idx - di) // (di + 1)
    sizes = (AW, R, Hi * di, di, Hi, PW, 2 * D)
    offs = [0]
    for s in sizes:
        offs.append(offs[-1] + s)
    ws = [w_in[:, offs[k]:offs[k + 1]].astype(BF16) for k in range(len(sizes))]
    ws[4] = jnp.pad(ws[4], ((0, 0), (0, LANES - Hi)))

    wukt = w_uk.transpose(1, 0, 2).astype(BF16)
    wuvt = w_uv.transpose(1, 2, 0).astype(BF16)
    qlt, ckva, ckvt, qit, ki, wit, u, g = _inproj(
        x, mod, ws, kv_norm_g.reshape(1, R), wukt, q_scale=dh ** -0.5 * LOG2E, qi_scale=di ** -0.5,
        wi_scale=Hi ** -0.5, Hi=Hi, tm=min(S, 512))

    topk = min(TOPK_MAX, S // 4)
    y_attn_t = _attention(qit, wit, ki, qlt, ckva, ckvt, wuvt, tq=min(S, 256), topk=topk)

    wr = jnp.pad(w_router, ((0, 0), (0, LANES - E)))
    wr_hi = wr.astype(BF16)
    wr = jnp.stack([wr_hi, (wr - wr_hi.astype(F32)).astype(BF16)])
    br = jnp.pad(b_router, (0, LANES - E)).reshape(1, LANES)
    x1, h2, te, tg = _merge(x, y_attn_t, u, g, mod, w_pool_group.astype(BF16), pool_scale.reshape(1, PW),
                            w_branch_attn.astype(BF16), w_branch_pool.astype(BF16), w_out.astype(BF16),
                            ln1_g.reshape(1, D), ln1_b.reshape(1, D), wr, br,
                            tm=min(S, 256), alpha=alpha, E=E, K=K)

    T = B * S
    tmE = 512
    tmC = min(S, 256)
    tD = min(T, 1024)
    sub = D // LANES
    te = te.reshape(T, LANES)
    rank, counts = _route(te, tR=min(T, 512), K=K)
    block_e, n_used, next_e, slot, pad_end, pos, n_blocks = _routing_tables(
        te[:, :K], rank[:, :K], counts[0, :E], E, tmE, sub)
    xs = _dispatch(pad_end, n_used, _tile_major(pos, tD), h2.reshape(T * sub, LANES), n_blocks,
                   tD=tD, tmE=tmE, K=K, sub=sub)
    y_sorted = _experts(block_e, n_used, next_e, slot, xs, w_gate_up, b_gate_up.reshape(E, 1, -1),
                        w_down, b_down.reshape(E, 1, D), tmE=tmE)
    out = _combine(_tile_major(pos, tmC), y_sorted, x1.reshape(T, D), tg.reshape(T, LANES), mod,
                   ln2_g.reshape(1, D), ln2_b.reshape(1, D), tmC=tmC, K=K, alpha=alpha, S=S)
    return out.reshape(B, S, D)


def kernel(x, c, w_ada, b_ada, w_in, kv_norm_g, w_uk, w_uv, w_pool_group, pool_scale, w_branch_attn,
           w_branch_pool, w_out, ln1_g, ln1_b, w_router, b_router, w_gate_up, b_gate_up, w_down, b_down,
           ln2_g, ln2_b):
    B, S, D = x.shape
    depth = w_ada.shape[0]
    alpha = (2.0 * depth) ** 0.25
    for l in range(depth):
        mod = _ada(c, w_ada[l], b_ada[l]).reshape(B, N_MOD, D)
        x = _layer(x, mod, w_in[l], kv_norm_g[l], w_uk[l], w_uv[l], w_pool_group[l], pool_scale[l],
                   w_branch_attn[l], w_branch_pool[l], w_out[l], ln1_g[l], ln1_b[l], w_router[l],
                   b_router[l], w_gate_up[l], b_gate_up[l], w_down[l], b_down[l], ln2_g[l], ln2_b[l],
                   alpha=alpha)
    return x
```

```python
import functools

import jax
import jax.numpy as jnp
from jax import lax
from jax.experimental import pallas as pl
from jax.experimental.pallas import tpu as pltpu

F32 = jnp.float32
BF16 = jnp.bfloat16
I32 = jnp.int32
HIGHEST = lax.Precision.HIGHEST

LN_EPS = 1e-5
TOPK_MAX = 256
TOP_K_EXPERTS = 4
POOL_WINDOWS = (2, 4, 8, 16)
SWIGLU_LIMIT = 7.0
SWIGLU_ALPHA = 1.702
N_MOD = 6
INT_MIN = -(2 ** 31)
NEG_BIG = -1e30
LANES = 128
SUBLANES = 8
SORT_GROUP = 16
TIE_CHUNK = 256
COMBINE_GROUPS = 8
POOL_HALO = 16
ONES_ROWS = 16
LOG2E = 1.4426950408889634
ATTN_KEY_CHUNK = 256
ATTN_HEAD_UNROLL = 8


def _store_row_slabs(ref, x):
    n, D = x.shape
    sub = D // LANES
    for s in range(sub):
        ref[pl.ds(s, n, stride=sub), :] = x[:, s * LANES:(s + 1) * LANES]


def _load_row_slabs(ref, n, sub):
    return jnp.concatenate([ref[pl.ds(s, n, stride=sub), :] for s in range(sub)], axis=1)


def _sigmoid(x):
    return 0.5 * jnp.tanh(0.5 * x) + 0.5


def _layer_norm(z, g, b):
    mu = jnp.mean(z, axis=-1, keepdims=True)
    zc = z - mu
    var = jnp.mean(zc * zc, axis=-1, keepdims=True)
    return zc * lax.rsqrt(var + LN_EPS) * g + b


def _ada_kernel(c_ref, w_ref, b_ref, o_ref):
    c = c_ref[...]
    cond = c * jax.nn.sigmoid(c)
    o_ref[...] = jnp.dot(cond, w_ref[...], precision=HIGHEST, preferred_element_type=F32) + b_ref[...]


def _ada(c, w_ada, b_ada):
    B, D = c.shape
    N = w_ada.shape[1]
    tn = D
    return pl.pallas_call(
        _ada_kernel,
        grid=(N // tn,),
        in_specs=[pl.BlockSpec((B, D), lambda j: (0, 0)),
                  pl.BlockSpec((D, tn), lambda j: (0, j)),
                  pl.BlockSpec((1, tn), lambda j: (0, j))],
        out_specs=pl.BlockSpec((B, tn), lambda j: (0, j)),
        out_shape=jax.ShapeDtypeStruct((B, N), F32),
    )(c, w_ada, b_ada.reshape(1, N))


def _inproj_kernel(x_ref, mod_ref, wq, wckv, wqi, wki, wwi, wu, wg, kvg_ref, wukt_ref,
                   qlt_o, ckva_o, ckvt_o, qit_o, ki_o, wit_o, u_o, g_o,
                   *, tm, H, dh, Hi, q_scale, qi_scale, wi_scale):
    i = pl.program_id(1)
    x = x_ref[0]
    mod = mod_ref[0]
    h = (x * (1.0 + mod[1:2]) + mod[0:1]).astype(BF16)

    def proj(w):
        return jnp.dot(h, w[...], preferred_element_type=F32)

    qt = proj(wq).T.astype(BF16)
    for hh in range(H):
        qlt = jnp.dot(wukt_ref[hh], qt[hh * dh:(hh + 1) * dh, :], preferred_element_type=F32) * q_scale
        qlt_o[0, hh] = qlt.astype(BF16)

    ckv = proj(wckv)
    ckv = ckv * lax.rsqrt(jnp.mean(ckv * ckv, axis=-1, keepdims=True) + LN_EPS) * kvg_ref[...]
    ckvt_o[0, 0:ckv.shape[1], :] = ckv.T.astype(BF16)
    ckvt_o[0, ckv.shape[1]:, :] = jnp.ones((ONES_ROWS, tm), BF16)
    R = ckv.shape[1]
    pos = i * tm + lax.broadcasted_iota(I32, (tm, R), 0)
    lane = lax.broadcasted_iota(I32, (tm, R), 1)
    pos_hi = ((pos >> 8) << 8).astype(F32)
    pos_lo = (pos & 255).astype(F32)
    extra = jnp.where((lane == 0) | (lane == 2), pos_hi, jnp.where((lane == 1) | (lane == 3), pos_lo, 0.0))
    ckva_o[0, :, 0:R] = ckv.astype(BF16)
    ckva_o[0, :, R:2 * R] = extra.astype(BF16)

    qit_o[0] = (proj(wqi) * qi_scale).T.astype(BF16)
    ki_o[0] = proj(wki).astype(BF16)
    wit_o[0] = (proj(wwi) * wi_scale).T[0:Hi, :]
    u_o[0] = proj(wu)
    g_o[0] = proj(wg).astype(BF16)


def _inproj(x, mod, ws, kv_norm_g, wukt, *, q_scale, qi_scale, wi_scale, Hi, tm):
    B, S, D = x.shape
    wq, wckv, wqi, wki, wwi, wu, wg = ws
    H, R, dh = wukt.shape
    row = lambda n: pl.BlockSpec((1, tm, n), lambda b, i: (b, i, 0))
    col = lambda n: pl.BlockSpec((1, n, tm), lambda b, i: (b, 0, i))
    full = lambda a: pl.BlockSpec(a.shape, lambda b, i: (0,) * a.ndim)
    out_specs = [pl.BlockSpec((1, H, R, tm), lambda b, i: (b, 0, 0, i)), row(2 * R), col(R + ONES_ROWS),
                 col(wqi.shape[1]),
                 row(wki.shape[1]), col(Hi), row(wu.shape[1]), row(wg.shape[1])]
    out_shape = [jax.ShapeDtypeStruct((B, H, R, S), BF16), jax.ShapeDtypeStruct((B, S, 2 * R), BF16),
                 jax.ShapeDtypeStruct((B, R + ONES_ROWS, S), BF16),
                 jax.ShapeDtypeStruct((B, wqi.shape[1], S), BF16),
                 jax.ShapeDtypeStruct((B, S, wki.shape[1]), BF16), jax.ShapeDtypeStruct((B, Hi, S), F32),
                 jax.ShapeDtypeStruct((B, S, wu.shape[1]), F32), jax.ShapeDtypeStruct((B, S, wg.shape[1]), BF16)]
    return pl.pallas_call(
        functools.partial(_inproj_kernel, tm=tm, H=H, dh=dh, Hi=Hi, q_scale=q_scale, qi_scale=qi_scale,
                          wi_scale=wi_scale),
        grid=(B, S // tm),
        in_specs=[row(D), pl.BlockSpec((1, N_MOD, D), lambda b, i: (b, 0, 0))]
                 + [full(w) for w in ws] + [full(kv_norm_g), full(wukt)],
        out_specs=out_specs,
        out_shape=out_shape,
        compiler_params=pltpu.CompilerParams(
            dimension_semantics=("parallel", "parallel"), vmem_limit_bytes=48 * 2 ** 20),
    )(x, mod, *ws, kv_norm_g, wukt)


def _key_to_f32(t):
    return pltpu.bitcast(jnp.where(t >= 0, t, t ^ 0x7FFFFFFF), F32)


def _sort_network(n):
    pairs, p = [], 1
    while p < n:
        k = p
        while k >= 1:
            for j in range(k % p, n - k, 2 * k):
                for i in range(min(k, n - j - k)):
                    if (i + j) // (2 * p) == (i + j + k) // (2 * p):
                        pairs.append((i + j, i + j + k))
            k //= 2
        p *= 2
    return pairs


def _count_ge_sorted(blocks, cand):
    n = len(blocks)

    def pick(level, bits):
        if len(level) == 1:
            return level[0]
        half = len(level) // 2
        return jnp.where(bits[0], pick(level[half:], bits[1:]), pick(level[:half], bits[1:]))

    total = jnp.where(blocks[n - 1] >= cand, 1.0, 0.0)
    bits = []
    step = n // 2
    while step >= 1:
        level = [blocks[c + step - 1] for c in range(0, n, 2 * step)]
        b = pick(level, bits) >= cand
        bits.append(b)
        total = total + jnp.where(b, float(step), 0.0)
        step //= 2
    return total


def _attn_body(qit_ref, wit_ref, ki_ref, qlt_ref, aug_ref, ckva_ref, ckvt_ref, wuvt_ref, yt_ref,
               score_scr, bias_scr, lg_scr, o_scr, sort_scr, *, W, tq, S, H, dh, Hi, di, topk, cw):
    j = pl.program_id(1)
    qpos = j * tq + lax.broadcasted_iota(I32, (1, tq), 1)
    score = score_scr.at[0:W, :]

    wit = wit_ref[0]
    for c in range(W // cw):
        kic = ki_ref[0, c * cw:(c + 1) * cw, :]
        acc = jnp.zeros((cw, tq), F32)
        for h in range(Hi):
            d = jnp.dot(kic, qit_ref[0, h * di:(h + 1) * di, :], preferred_element_type=F32)
            acc = acc + wit[h:h + 1, :] * jnp.maximum(d, 0.0)
        kpos_c = c * cw + lax.broadcasted_iota(I32, (cw, tq), 0)
        score_scr[c * cw:(c + 1) * cw, :] = jnp.where(kpos_c <= qpos, acc, -jnp.inf)

    def count(hit):
        return jnp.sum(jnp.where(hit, 1.0, 0.0), axis=0, keepdims=True)

    Wb = W // SORT_GROUP
    net = _sort_network(SORT_GROUP)

    def column(r, lg):
        return [(slice(g * Wb + r * SUBLANES, g * Wb + (r + 1) * SUBLANES), slice(lg * LANES, (lg + 1) * LANES))
                for g in range(SORT_GROUP)]

    for r in range(Wb // SUBLANES):
        for lg in range(tq // LANES):
            v = [score_scr[idx] for idx in column(r, lg)]
            for a, b in net:
                v[a], v[b] = jnp.maximum(v[a], v[b]), jnp.minimum(v[a], v[b])
            for idx, val in zip(column(r, lg), v):
                sort_scr[idx] = val

    def count_ge(cand_f):
        parts = []
        for lg in range(tq // LANES):
            c = cand_f[:, lg * LANES:(lg + 1) * LANES]
            acc = None
            for r in range(Wb // SUBLANES):
                n = _count_ge_sorted([sort_scr[idx] for idx in column(r, lg)], c)
                acc = n if acc is None else acc + n
            parts.append(jnp.sum(acc, axis=0, keepdims=True))
        return jnp.concatenate(parts, axis=1)

    def thr_step(i, t):
        cand = t + lax.shift_left(jnp.int32(1), 31 - i)
        cnt = count_ge(_key_to_f32(cand))
        return jnp.where(cnt >= float(topk), cand, t)

    t = lax.fori_loop(0, 32, thr_step, jnp.full((1, tq), INT_MIN, I32))
    few = t == INT_MIN
    thr_f = _key_to_f32(t)
    up_f = _key_to_f32(t + 1)
    need = float(topk) - count(score[...] >= up_f)

    tc = min(TIE_CHUNK, W)
    earlier = (lax.broadcasted_iota(I32, (tc, tc), 1) < lax.broadcasted_iota(I32, (tc, tc), 0)).astype(BF16)
    tied_before = jnp.zeros((1, tq), F32)
    for c in range(W // tc):
        rows = slice(c * tc, (c + 1) * tc)
        s = score_scr[rows, :]
        gt = s >= up_f
        eq = (s >= thr_f) & jnp.logical_not(gt)
        eq_f = jnp.where(eq, 1.0, 0.0)
        rank = jnp.dot(earlier, eq_f.astype(BF16), preferred_element_type=F32) + tied_before
        tied_before = tied_before + jnp.sum(eq_f, axis=0, keepdims=True)
        kpos = c * tc + lax.broadcasted_iota(I32, (tc, tq), 0)
        sel = (few | gt | (eq & (rank < need))) & (kpos <= qpos)
        bias_scr[rows, :] = jnp.where(sel, 0.0, NEG_BIG)

    R = qlt_ref.shape[2]
    ck = min(ATTN_KEY_CHUNK, W)

    def head(h, carry):
        qa = jnp.concatenate([qlt_ref[0, h], aug_ref[h]], axis=0)
        m = None
        for c in range(W // ck):
            keys = slice(c * ck, (c + 1) * ck)
            lg = jnp.dot(ckva_ref[0, keys, :], qa, preferred_element_type=F32) + bias_scr[keys, :]
            lg_scr[keys, :] = lg
            mc = jnp.max(lg, axis=0, keepdims=True)
            m = mc if m is None else jnp.maximum(m, mc)
        acc = jnp.zeros((R + ONES_ROWS, tq), F32)
        for c in range(W // ck):
            keys = slice(c * ck, (c + 1) * ck)
            p = jnp.exp2(lg_scr[keys, :] - m)
            acc = acc + jnp.dot(ckvt_ref[0, :, keys], p.astype(BF16), preferred_element_type=F32)
        o_scr[h] = (acc[0:R] / acc[R:R + 1]).astype(BF16)
        return carry

    lax.fori_loop(0, H, head, 0, unroll=ATTN_HEAD_UNROLL)
    for h in range(H):
        yt = jnp.dot(wuvt_ref[h], o_scr[h], preferred_element_type=F32)
        yt_ref[0, h * dh:(h + 1) * dh, :] = yt.astype(BF16)


def _attn_kernel(*refs, tq, cw, S, **kw):
    j = pl.program_id(1)
    nc = ((j + 1) * tq + cw - 1) // cw
    for w in range(1, S // cw + 1):
        @pl.when(nc == w)
        def _():
            _attn_body(*refs, W=w * cw, tq=tq, S=S, cw=cw, **kw)


def _attention(qit, wit, ki, qlt, ckva, ckvt, wuvt, *, tq, topk):
    B, H, R, S = qlt.shape
    dh = wuvt.shape[1]
    di = ki.shape[2]
    Hi = wit.shape[1]
    slopes = jnp.asarray([LOG2E * 2.0 ** (-8.0 * (h + 1) / H) for h in range(H)], F32)
    s_hi = slopes.astype(BF16)
    s_lo = (slopes - s_hi.astype(F32)).astype(BF16)
    aug = jnp.zeros((H, R, tq), BF16)
    aug = aug.at[:, 0:2, :].set(s_hi[:, None, None]).at[:, 2:4, :].set(s_lo[:, None, None])
    kern = functools.partial(_attn_kernel, tq=tq, S=S, H=H, dh=dh, Hi=Hi, di=di, topk=topk, cw=min(S, 512))
    col = lambda n: pl.BlockSpec((1, n, tq), lambda b, j: (b, 0, j))
    full = lambda a: pl.BlockSpec(a.shape, lambda b, j: (0,) * a.ndim)
    return pl.pallas_call(
        kern,
        grid=(B, S // tq),
        in_specs=[col(qit.shape[1]), col(Hi), pl.BlockSpec((1, S, di), lambda b, j: (b, 0, 0)),
                  pl.BlockSpec((1, H, R, tq), lambda b, j: (b, 0, 0, j)), full(aug),
                  pl.BlockSpec((1, S, 2 * R), lambda b, j: (b, 0, 0)),
                  pl.BlockSpec((1, R + ONES_ROWS, S), lambda b, j: (b, 0, 0)), full(wuvt)],
        out_specs=col(H * dh),
        out_shape=jax.ShapeDtypeStruct((B, H * dh, S), BF16),
        scratch_shapes=[pltpu.VMEM((S, tq), F32), pltpu.VMEM((S, tq), F32), pltpu.VMEM((S, tq), F32),
                        pltpu.VMEM((H, R, tq), BF16), pltpu.VMEM((S, tq), F32)],
        compiler_params=pltpu.CompilerParams(
            dimension_semantics=("parallel", "parallel"), vmem_limit_bytes=48 * 2 ** 20),
    )(qit, wit, ki, qlt, aug, ckva, ckvt, wuvt)


def _merge_kernel(x_ref, yat_ref, u_ref, halo_ref, g_ref, mod_ref, wpg_ref, psc_ref, wba_ref, wbp_ref,
                  wout_ref, ln1g_ref, ln1b_ref, wr_ref, br_ref,
                  x1_o, h2_o, te_o, tg_o, ext_scr, *, tm, D, G, alpha, E, K):
    i = pl.program_id(1)

    @pl.when(i == 0)
    def _():
        ext_scr[0:POOL_HALO, :] = jnp.zeros((POOL_HALO, ext_scr.shape[1]), F32)

    @pl.when(i > 0)
    def _():
        ext_scr[0:POOL_HALO, :] = halo_ref[0]

    ext_scr[POOL_HALO:POOL_HALO + tm, :] = u_ref[0]

    pos = i * tm + lax.broadcasted_iota(I32, (tm, 1), 0)
    ys = []
    for g, w in enumerate(POOL_WINDOWS):
        cols = slice(g * G, (g + 1) * G)
        cur = ext_scr[POOL_HALO:POOL_HALO + tm, cols]
        acc = cur
        for back in range(1, w):
            acc = acc + ext_scr[POOL_HALO - back:POOL_HALO - back + tm, cols]
        cnt = jnp.minimum(pos + 1, w).astype(F32)
        pooled = acc / cnt - cur
        ys.append(jnp.dot(pooled.astype(BF16), wpg_ref[g], preferred_element_type=F32))
    y_pool = jnp.concatenate(ys, axis=1) * psc_ref[...]

    gates = g_ref[0]
    ya = yat_ref[0].astype(F32).T.astype(BF16)
    a = jnp.dot(ya, wba_ref[...], preferred_element_type=F32)
    p = jnp.dot(y_pool.astype(BF16), wbp_ref[...], preferred_element_type=F32)
    mix = _sigmoid(gates[:, :D].astype(F32)) * a + _sigmoid(gates[:, D:].astype(F32)) * p
    o = jnp.dot(mix.astype(BF16), wout_ref[...], preferred_element_type=F32)
    mod = mod_ref[0]
    x1 = _layer_norm(alpha * x_ref[0] + mod[2:3] * o, ln1g_ref[...], ln1b_ref[...])
    x1_o[0] = x1
    h2 = x1 * (1.0 + mod[4:5]) + mod[3:4]
    _store_row_slabs(h2_o.at[0], h2)

    h2_hi = h2.astype(BF16)
    h2_lo = (h2 - h2_hi.astype(F32)).astype(BF16)
    logits = (jnp.dot(h2_hi, wr_ref[0], preferred_element_type=F32)
              + jnp.dot(h2_hi, wr_ref[1], preferred_element_type=F32)
              + jnp.dot(h2_lo, wr_ref[0], preferred_element_type=F32)) + br_ref[...]
    lane = lax.broadcasted_iota(I32, logits.shape, 1)
    logits = jnp.where(lane < E, logits, -jnp.inf)
    lane_f = lane.astype(F32)
    te_f = jnp.zeros(logits.shape, F32)
    vals = []
    for k in range(K):
        m = jnp.max(logits, axis=1, keepdims=True)
        idx = jnp.min(jnp.where(logits == m, lane_f, float(LANES)), axis=1, keepdims=True)
        te_f = jnp.where(lane == k, idx, te_f)
        vals.append(m)
        logits = jnp.where(lane_f == idx, -jnp.inf, logits)
    te = te_f.astype(I32)
    ex = [jnp.exp(v - vals[0]) for v in vals]
    den = ex[0]
    for e in ex[1:]:
        den = den + e
    tg = jnp.zeros(logits.shape, F32)
    for k in range(K):
        tg = jnp.where(lane == k, ex[k] / den, tg)
    te_o[0] = te
    tg_o[0] = tg


def _merge(x, yat, u, g, mod, wpg, psc, wba, wbp, wout, ln1g, ln1b, wr, br, *, tm, alpha, E, K):
    B, S, D = x.shape
    PW = u.shape[2]
    G = wpg.shape[1]
    kern = functools.partial(_merge_kernel, tm=tm, D=D, G=G, alpha=alpha, E=E, K=K)
    row = lambda n: pl.BlockSpec((1, tm, n), lambda b, i: (b, i, 0))
    full = lambda a: pl.BlockSpec(a.shape, lambda b, i: (0,) * a.ndim)
    hb = tm // POOL_HALO
    halo = pl.BlockSpec((1, POOL_HALO, PW), lambda b, i: (b, jnp.maximum(i * hb - 1, 0), 0))
    return pl.pallas_call(
        kern,
        grid=(B, S // tm),
        in_specs=[row(D), pl.BlockSpec((1, yat.shape[1], tm), lambda b, i: (b, 0, i)), row(PW), halo,
                  row(g.shape[2]), pl.BlockSpec((1, N_MOD, D), lambda b, i: (b, 0, 0)),
                  full(wpg), full(psc), full(wba), full(wbp), full(wout), full(ln1g), full(ln1b),
                  full(wr), full(br)],
        out_specs=[row(D), pl.BlockSpec((1, tm * (D // LANES), LANES), lambda b, i: (b, i, 0)),
                   row(LANES), row(LANES)],
        out_shape=[jax.ShapeDtypeStruct((B, S, D), F32),
                   jax.ShapeDtypeStruct((B, S * (D // LANES), LANES), F32),
                   jax.ShapeDtypeStruct((B, S, LANES), I32), jax.ShapeDtypeStruct((B, S, LANES), F32)],
        scratch_shapes=[pltpu.VMEM((POOL_HALO + tm, PW), F32)],
        compiler_params=pltpu.CompilerParams(
            dimension_semantics=("parallel", "arbitrary"), vmem_limit_bytes=48 * 2 ** 20),
    )(x, yat, u, u, g, mod, wpg, psc, wba, wbp, wout, ln1g, ln1b, wr, br)


def _route_kernel(te_ref, rank_o, cnt_o, base_scr, *, tR, K):
    i = pl.program_id(0)

    @pl.when(i == 0)
    def _():
        base_scr[...] = jnp.zeros(base_scr.shape, F32)

    te = te_ref[...]
    lane = lax.broadcasted_iota(I32, te.shape, 1)
    hits = [lane == te[:, k:k + 1] for k in range(K)]
    onehot = hits[0].astype(F32)
    for k in range(1, K):
        onehot = onehot + hits[k].astype(F32)
    earlier = (lax.broadcasted_iota(I32, (tR, tR), 0) > lax.broadcasted_iota(I32, (tR, tR), 1)).astype(BF16)
    before = jnp.dot(earlier, onehot.astype(BF16), preferred_element_type=F32) + base_scr[...]
    rank = jnp.zeros(te.shape, I32)
    for k in range(K):
        rk = jnp.sum(jnp.where(hits[k], before, 0.0), axis=1, keepdims=True)
        rank = jnp.where(lane == k, rk.astype(I32), rank)
    rank_o[...] = rank
    base_scr[...] = base_scr[...] + jnp.sum(onehot, axis=0, keepdims=True)
    cnt_o[...] = base_scr[...].astype(I32)


def _route(te, *, tR, K):
    T = te.shape[0]
    return pl.pallas_call(
        functools.partial(_route_kernel, tR=tR, K=K),
        grid=(T // tR,),
        in_specs=[pl.BlockSpec((tR, LANES), lambda i: (i, 0))],
        out_specs=[pl.BlockSpec((tR, LANES), lambda i: (i, 0)), pl.BlockSpec((1, LANES), lambda i: (0, 0))],
        out_shape=[jax.ShapeDtypeStruct((T, LANES), I32), jax.ShapeDtypeStruct((1, LANES), I32)],
        scratch_shapes=[pltpu.VMEM((1, LANES), F32)],
        compiler_params=pltpu.CompilerParams(dimension_semantics=("arbitrary",)),
    )(te)


def _dispatch_kernel(pe_ref, nu_ref, pos_ref, h2_ref, xs_out, zero_scr, sem, zsem,
                     *, tD, K, sub, E, tmE, n_blocks):
    blk = tmE * sub

    @pl.when(pl.program_id(0) == 0)
    def _():
        zero_scr[...] = jnp.zeros(zero_scr.shape, zero_scr.dtype)

        def zero_block(b):
            return pltpu.make_async_copy(zero_scr, xs_out.at[pl.ds(pl.multiple_of(b * blk, blk), blk), :], zsem)

        def last_block(e):
            return jnp.maximum(pe_ref[e] // tmE - 1, 0)

        def start_last(e, carry):
            zero_block(last_block(e)).start()
            return carry

        def start_tail(b, carry):
            zero_block(b).start()
            return carry

        def wait_last(e, carry):
            zero_block(last_block(e)).wait()
            return carry

        def wait_tail(b, carry):
            zero_block(b).wait()
            return carry

        lax.fori_loop(0, E, start_last, 0)
        lax.fori_loop(nu_ref[0], n_blocks, start_tail, 0)
        lax.fori_loop(0, E, wait_last, 0)
        lax.fori_loop(nu_ref[0], n_blocks, wait_tail, 0)

    def issue(r, carry):
        src = h2_ref.at[pl.ds(pl.multiple_of(r * sub, sub), sub), :]
        for k in range(K):
            p = pl.multiple_of(pos_ref[0, 0, k * tD + r], sub)
            pltpu.async_copy(src, xs_out.at[pl.ds(p, sub), :], sem, priority=k % 2)
        return carry

    lax.fori_loop(0, tD, issue, 0, unroll=8)
    for k in range(K):
        pltpu.make_async_copy(h2_ref, xs_out.at[pl.ds(0, tD * sub), :], sem).wait()


def _dispatch(pad_end, n_used, pos, h2, n_blocks, *, tD, tmE, K, sub):
    T = h2.shape[0] // sub
    E = pad_end.shape[0]
    grid_spec = pltpu.PrefetchScalarGridSpec(
        num_scalar_prefetch=2,
        grid=(T // tD,),
        in_specs=[pl.BlockSpec((1, 1, K * tD), lambda i, pe, nu: (i, 0, 0), memory_space=pltpu.SMEM),
                  pl.BlockSpec((tD * sub, LANES), lambda i, pe, nu: (i, 0))],
        out_specs=pl.BlockSpec(memory_space=pl.ANY),
        scratch_shapes=[pltpu.VMEM((tmE * sub, LANES), h2.dtype), pltpu.SemaphoreType.DMA(()),
                        pltpu.SemaphoreType.DMA(())],
    )
    return pl.pallas_call(
        functools.partial(_dispatch_kernel, tD=tD, K=K, sub=sub, E=E, tmE=tmE, n_blocks=n_blocks),
        grid_spec=grid_spec,
        out_shape=jax.ShapeDtypeStruct((n_blocks * tmE * sub, LANES), h2.dtype),
        compiler_params=pltpu.CompilerParams(
            dimension_semantics=("arbitrary",), has_side_effects=True),
    )(pad_end, n_used, pos, h2)


def _expert_kernel(be_ref, nu_ref, nx_ref, sl_ref, x_ref, wgu_hbm, bgu_ref, wd_hbm, bd_ref, y_ref,
                   wgu_f32, wd_f32, wgu_bf, wd_bf, sem, *, F, tmE, sub):
    i = pl.program_id(0)
    used = i < nu_ref[0]
    e = be_ref[i]
    slot = sl_ref[i]
    new_expert = used & ((i == 0) | (e != be_ref[jnp.maximum(i - 1, 0)]))

    def fetch(expert, s):
        return (pltpu.make_async_copy(wgu_hbm.at[expert], wgu_f32.at[s], sem.at[s, 0]),
                pltpu.make_async_copy(wd_hbm.at[expert], wd_f32.at[s], sem.at[s, 1]))

    @pl.when(used & (i == 0))
    def _():
        for cp in fetch(e, slot):
            cp.start()

    @pl.when(new_expert)
    def _():
        for cp in fetch(e, slot):
            cp.wait()

        @pl.when(nx_ref[i] >= 0)
        def _():
            for cp in fetch(nx_ref[i], 1 - slot):
                cp.start()

        wgu_bf[...] = wgu_f32[slot].astype(BF16)
        wd_bf[...] = wd_f32[slot].astype(BF16)

    @pl.when(used)
    def _():
        x = _load_row_slabs(x_ref, tmE, sub).astype(BF16)
        gu = jnp.dot(x, wgu_bf[...], preferred_element_type=F32) + bgu_ref[0]
        glu = jnp.minimum(gu[:, :F], SWIGLU_LIMIT)
        lin = jnp.clip(gu[:, F:], -SWIGLU_LIMIT, SWIGLU_LIMIT)
        act = glu * jax.nn.sigmoid(SWIGLU_ALPHA * glu) * (lin + 1.0)
        y = jnp.dot(act.astype(BF16), wd_bf[...], preferred_element_type=F32) + bd_ref[0]
        _store_row_slabs(y_ref, y)

    @pl.when(jnp.logical_not(used))
    def _():
        y_ref[...] = jnp.zeros(y_ref.shape, F32)


def _experts(block_e, n_used, next_e, slot, xs, wgu, bgu, wd, bd, *, tmE):
    E, D, F2 = wgu.shape
    F = F2 // 2
    sub = D // LANES
    n_blocks = xs.shape[0] // (tmE * sub)
    slab_block = pl.BlockSpec((tmE * sub, LANES), lambda i, be, nu, nx, sl: (i, 0))
    grid_spec = pltpu.PrefetchScalarGridSpec(
        num_scalar_prefetch=4,
        grid=(n_blocks,),
        in_specs=[slab_block,
                  pl.BlockSpec(memory_space=pl.ANY),
                  pl.BlockSpec((1, 1, F2), lambda i, be, nu, nx, sl: (be[i], 0, 0)),
                  pl.BlockSpec(memory_space=pl.ANY),
                  pl.BlockSpec((1, 1, D), lambda i, be, nu, nx, sl: (be[i], 0, 0))],
        out_specs=slab_block,
        scratch_shapes=[pltpu.VMEM((2, D, F2), F32), pltpu.VMEM((2, F, D), F32),
                        pltpu.VMEM((D, F2), BF16), pltpu.VMEM((F, D), BF16),
                        pltpu.SemaphoreType.DMA((2, 2))],
    )
    return pl.pallas_call(
        functools.partial(_expert_kernel, F=F, tmE=tmE, sub=sub),
        grid_spec=grid_spec,
        out_shape=jax.ShapeDtypeStruct(xs.shape, F32),
        compiler_params=pltpu.CompilerParams(
            dimension_semantics=("arbitrary",), vmem_limit_bytes=56 * 2 ** 20),
    )(block_e, n_used, next_e, slot, xs, wgu, bgu, wd, bd)


def _combine_kernel(pos_ref, nxt_ref, y_hbm, x1_ref, tg_ref, mod_ref, g_ref, b_ref, o_ref, buf0, buf1, sem,
                    *, tmC, K, alpha, nt, sub):
    i = pl.program_id(0)
    rows_per_group = tmC // COMBINE_GROUPS

    def start_row(p_ref, dst, s, r):
        row0 = r * sub if isinstance(r, int) else pl.multiple_of(r * sub, sub)
        for k in range(K):
            p = pl.multiple_of(p_ref[0, 0, k * tmC + r], sub)
            pltpu.async_copy(y_hbm.at[pl.ds(p, sub), :], dst.at[k, pl.ds(row0, sub), :], sem.at[s],
                             priority=k % 2)

    def reduce_rows(cur, g):
        rows = slice(g * rows_per_group, (g + 1) * rows_per_group)
        tg = tg_ref[rows, :]
        y = None
        for k in range(K):
            part = _load_row_slabs(cur.at[k, pl.ds(g * rows_per_group * sub, rows_per_group * sub), :],
                                   rows_per_group, sub)
            y = tg[:, k:k + 1] * part if y is None else y + tg[:, k:k + 1] * part
        mod = mod_ref[0]
        o_ref[rows, :] = _layer_norm(alpha * x1_ref[rows, :] + mod[5:6] * y, g_ref[...], b_ref[...])

    def step(cur, cur_s, nxt, nxt_s, prefetch):
        for k in range(K):
            pltpu.make_async_copy(y_hbm.at[pl.ds(0, tmC * sub), :], cur.at[k], sem.at[cur_s]).wait()
        for g in range(COMBINE_GROUPS):
            if prefetch:
                for r in range(g * rows_per_group, (g + 1) * rows_per_group):
                    start_row(nxt_ref, nxt, nxt_s, r)
            reduce_rows(cur, g)

    @pl.when(i == 0)
    def _():
        def issue(r, carry):
            start_row(pos_ref, buf0, 0, r)
            return carry
        lax.fori_loop(0, tmC, issue, 0, unroll=8)

    even = lax.rem(i, 2) == 0
    last = i + 1 == nt
    for is_even, (cur, cur_s, nxt, nxt_s) in ((True, (buf0, 0, buf1, 1)), (False, (buf1, 1, buf0, 0))):
        for is_last in (False, True):
            @pl.when((even == is_even) & (last == is_last))
            def _():
                step(cur, cur_s, nxt, nxt_s, prefetch=not is_last)


def _combine(pos, y_sorted, x1, tg, mod, ln2g, ln2b, *, tmC, K, alpha, S):
    T, D = x1.shape
    nt = T // tmC
    per_seq = S // tmC
    sub = D // LANES
    return pl.pallas_call(
        functools.partial(_combine_kernel, tmC=tmC, K=K, alpha=alpha, nt=nt, sub=sub),
        grid=(nt,),
        in_specs=[pl.BlockSpec((1, 1, K * tmC), lambda i: (i, 0, 0), memory_space=pltpu.SMEM),
                  pl.BlockSpec((1, 1, K * tmC), lambda i: (jnp.minimum(i + 1, nt - 1), 0, 0),
                               memory_space=pltpu.SMEM),
                  pl.BlockSpec(memory_space=pl.ANY),
                  pl.BlockSpec((tmC, D), lambda i: (i, 0)),
                  pl.BlockSpec((tmC, LANES), lambda i: (i, 0)),
                  pl.BlockSpec((1, N_MOD, D), lambda i: (i // per_seq, 0, 0)),
                  pl.BlockSpec((1, D), lambda i: (0, 0)),
                  pl.BlockSpec((1, D), lambda i: (0, 0))],
        out_specs=pl.BlockSpec((tmC, D), lambda i: (i, 0)),
        out_shape=jax.ShapeDtypeStruct((T, D), F32),
        scratch_shapes=[pltpu.VMEM((K, tmC * sub, LANES), F32), pltpu.VMEM((K, tmC * sub, LANES), F32),
                        pltpu.SemaphoreType.DMA((2,))],
        compiler_params=pltpu.CompilerParams(
            dimension_semantics=("arbitrary",), vmem_limit_bytes=48 * 2 ** 20),
    )(pos, pos, y_sorted, x1, tg, mod, ln2g, ln2b)


def _routing_tables(top_e, rank, counts, E, tmE, sub):
    T, K = top_e.shape
    padded = (counts + tmE - 1) // tmE * tmE
    pad_end = jnp.cumsum(padded)
    pad_start = pad_end - padded
    n_blocks = -(-(T * K) // tmE) + E
    block_start = jnp.arange(n_blocks, dtype=I32) * tmE
    block_e = jnp.minimum(jnp.sum((pad_end[None, :] <= block_start[:, None]).astype(I32), axis=1), E - 1)
    n_used = (pad_end[-1] // tmE).astype(I32).reshape(1)
    ids = jnp.arange(E, dtype=I32)

    def lookup(table, idx):
        return jnp.sum(jnp.where(idx[..., None] == ids, table.astype(I32), 0), axis=-1)

    pos = (lookup(pad_start, top_e) + rank).astype(I32) * sub
    nonempty = counts > 0
    at_or_after = jnp.flip(lax.cummin(jnp.flip(jnp.where(nonempty, ids, E))))
    after = jnp.concatenate([at_or_after[1:], jnp.full((1,), E, I32)])
    next_e = lookup(jnp.where(after < E, after, -1), block_e)
    slot = lookup((jnp.cumsum(nonempty.astype(I32)) - 1) % 2, block_e)
    return block_e.astype(I32), n_used, next_e, slot, pad_end.astype(I32), pos, n_blocks


def _tile_major(pos, tile):
    T, K = pos.shape
    return pos.reshape(T // tile, tile, K).transpose(0, 2, 1).reshape(T // tile, 1, K * tile)


def _layer(x, mod, w_in, kv_norm_g, w_uk, w_uv, w_pool_group, pool_scale, w_branch_attn, w_branch_pool,
           w_out, ln1_g, ln1_b, w_router, b_router, w_gate_up, b_gate_up, w_down, b_down, ln2_g, ln2_b,
           *, alpha):
    B, S, D = x.shape
    R, H, dh = w_uk.shape
    AW = H * dh
    PW = pool_scale.shape[0]
    E = w_router.shape[1]
    K = TOP_K_EXPERTS
    n_idx = w_in.shape[1] - (AW + R + PW + 2 * D)
    di = dh
    Hi = (n_idx - di) // (di + 1)
    sizes = (AW, R, Hi * di, di, Hi, PW, 2 * D)
    offs = [0]
    for s in sizes:
        offs.append(offs[-1] + s)
    ws = [w_in[:, offs[k]:offs[k + 1]].astype(BF16) for k in range(len(sizes))]
    ws[4] = jnp.pad(ws[4], ((0, 0), (0, LANES - Hi)))

    wukt = w_uk.transpose(1, 0, 2).astype(BF16)
    wuvt = w_uv.transpose(1, 2, 0).astype(BF16)
    qlt, ckva, ckvt, qit, ki, wit, u, g = _inproj(
        x, mod, ws, kv_norm_g.reshape(1, R), wukt, q_scale=dh ** -0.5 * LOG2E, qi_scale=di ** -0.5,
        wi_scale=Hi ** -0.5, Hi=Hi, tm=min(S, 512))

    topk = min(TOPK_MAX, S // 4)
    y_attn_t = _attention(qit, wit, ki, qlt, ckva, ckvt, wuvt, tq=min(S, 256), topk=topk)

    wr = jnp.pad(w_router, ((0, 0), (0, LANES - E)))
    wr_hi = wr.astype(BF16)
    wr = jnp.stack([wr_hi, (wr - wr_hi.astype(F32)).astype(BF16)])
    br = jnp.pad(b_router, (0, LANES - E)).reshape(1, LANES)
    x1, h2, te, tg = _merge(x, y_attn_t, u, g, mod, w_pool_group.astype(BF16), pool_scale.reshape(1, PW),
                            w_branch_attn.astype(BF16), w_branch_pool.astype(BF16), w_out.astype(BF16),
                            ln1_g.reshape(1, D), ln1_b.reshape(1, D), wr, br,
                            tm=min(S, 512), alpha=alpha, E=E, K=K)

    T = B * S
    tmE = 512
    tmC = min(S, 256)
    tD = min(T, 1024)
    sub = D // LANES
    te = te.reshape(T, LANES)
    rank, counts = _route(te, tR=min(T, 512), K=K)
    block_e, n_used, next_e, slot, pad_end, pos, n_blocks = _routing_tables(
        te[:, :K], rank[:, :K], counts[0, :E], E, tmE, sub)
    xs = _dispatch(pad_end, n_used, _tile_major(pos, tD), h2.reshape(T * sub, LANES), n_blocks,
                   tD=tD, tmE=tmE, K=K, sub=sub)
    y_sorted = _experts(block_e, n_used, next_e, slot, xs, w_gate_up, b_gate_up.reshape(E, 1, -1),
                        w_down, b_down.reshape(E, 1, D), tmE=tmE)
    out = _combine(_tile_major(pos, tmC), y_sorted, x1.reshape(T, D), tg.reshape(T, LANES), mod,
                   ln2_g.reshape(1, D), ln2_b.reshape(1, D), tmC=tmC, K=K, alpha=alpha, S=S)
    return out.reshape(B, S, D)


def kernel(x, c, w_ada, b_ada, w_in, kv_norm_g, w_uk, w_uv, w_pool_group, pool_scale, w_branch_attn,
           w_branch_pool, w_out, ln1_g, ln1_b, w_router, b_router, w_gate_up, b_gate_up, w_down, b_down,
           ln2_g, ln2_b):
    B, S, D = x.shape
    depth = w_ada.shape[0]
    alpha = (2.0 * depth) ** 0.25
    for l in range(depth):
        mod = _ada(c, w_ada[l], b_ada[l]).reshape(B, N_MOD, D)
        x = _layer(x, mod, w_in[l], kv_norm_g[l], w_uk[l], w_uv[l], w_pool_group[l], pool_scale[l],
                   w_branch_attn[l], w_branch_pool[l], w_out[l], ln1_g[l], ln1_b[l], w_router[l],
                   b_router[l], w_gate_up[l], b_gate_up[l], w_down[l], b_down[l], ln2_g[l], ln2_b[l],
                   alpha=alpha)
    return x
```

```python
import functools

import jax
import jax.numpy as jnp
from jax import lax
from jax.experimental import pallas as pl
from jax.experimental.pallas import tpu as pltpu

F32 = jnp.float32
BF16 = jnp.bfloat16
I32 = jnp.int32
HIGHEST = lax.Precision.HIGHEST

LN_EPS = 1e-5
TOPK_MAX = 256
TOP_K_EXPERTS = 4
POOL_WINDOWS = (2, 4, 8, 16)
SWIGLU_LIMIT = 7.0
SWIGLU_ALPHA = 1.702
N_MOD = 6
INT_MIN = -(2 ** 31)
NEG_BIG = -1e30
LANES = 128
SUBLANES = 8
SORT_GROUP = 16
TIE_CHUNK = 256
COMBINE_GROUPS = 8
POOL_HALO = 16
ONES_ROWS = 16
LOG2E = 1.4426950408889634
ATTN_KEY_CHUNK = 256
ATTN_HEAD_UNROLL = 8


def _store_row_slabs(ref, x):
    n, D = x.shape
    sub = D // LANES
    for s in range(sub):
        ref[pl.ds(s, n, stride=sub), :] = x[:, s * LANES:(s + 1) * LANES]


def _load_row_slabs(ref, n, sub):
    return jnp.concatenate([ref[pl.ds(s, n, stride=sub), :] for s in range(sub)], axis=1)


def _sigmoid(x):
    return 0.5 * jnp.tanh(0.5 * x) + 0.5


def _layer_norm(z, g, b):
    mu = jnp.mean(z, axis=-1, keepdims=True)
    zc = z - mu
    var = jnp.mean(zc * zc, axis=-1, keepdims=True)
    return zc * lax.rsqrt(var + LN_EPS) * g + b


def _ada_kernel(c_ref, w_ref, b_ref, o_ref):
    c = c_ref[...]
    cond = c * jax.nn.sigmoid(c)
    o_ref[...] = jnp.dot(cond, w_ref[...], precision=HIGHEST, preferred_element_type=F32) + b_ref[...]


def _ada(c, w_ada, b_ada):
    B, D = c.shape
    N = w_ada.shape[1]
    tn = D
    return pl.pallas_call(
        _ada_kernel,
        grid=(N // tn,),
        in_specs=[pl.BlockSpec((B, D), lambda j: (0, 0)),
                  pl.BlockSpec((D, tn), lambda j: (0, j)),
                  pl.BlockSpec((1, tn), lambda j: (0, j))],
        out_specs=pl.BlockSpec((B, tn), lambda j: (0, j)),
        out_shape=jax.ShapeDtypeStruct((B, N), F32),
    )(c, w_ada, b_ada.reshape(1, N))


def _inproj_kernel(x_ref, mod_ref, wq, wckv, wqi, wki, wwi, wu, wg, kvg_ref, wukt_ref,
                   qlt_o, ckva_o, ckvt_o, qit_o, ki_o, wit_o, u_o, g_o,
                   *, tm, H, dh, Hi, q_scale, qi_scale, wi_scale):
    i = pl.program_id(1)
    x = x_ref[0]
    mod = mod_ref[0]
    h = (x * (1.0 + mod[1:2]) + mod[0:1]).astype(BF16)

    def proj(w):
        return jnp.dot(h, w[...], preferred_element_type=F32)

    qt = proj(wq).T.astype(BF16)
    for hh in range(H):
        qlt = jnp.dot(wukt_ref[hh], qt[hh * dh:(hh + 1) * dh, :], preferred_element_type=F32) * q_scale
        qlt_o[0, hh] = qlt.astype(BF16)

    ckv = proj(wckv)
    ckv = ckv * lax.rsqrt(jnp.mean(ckv * ckv, axis=-1, keepdims=True) + LN_EPS) * kvg_ref[...]
    ckvt_o[0, 0:ckv.shape[1], :] = ckv.T.astype(BF16)
    ckvt_o[0, ckv.shape[1]:, :] = jnp.ones((ONES_ROWS, tm), BF16)
    R = ckv.shape[1]
    pos = i * tm + lax.broadcasted_iota(I32, (tm, R), 0)
    lane = lax.broadcasted_iota(I32, (tm, R), 1)
    pos_hi = ((pos >> 8) << 8).astype(F32)
    pos_lo = (pos & 255).astype(F32)
    extra = jnp.where((lane == 0) | (lane == 2), pos_hi, jnp.where((lane == 1) | (lane == 3), pos_lo, 0.0))
    ckva_o[0, :, 0:R] = ckv.astype(BF16)
    ckva_o[0, :, R:2 * R] = extra.astype(BF16)

    qit_o[0] = (proj(wqi) * qi_scale).T.astype(BF16)
    ki_o[0] = proj(wki).astype(BF16)
    wit_o[0] = (proj(wwi) * wi_scale).T[0:Hi, :]
    u_o[0] = proj(wu)
    g_o[0] = proj(wg).astype(BF16)


def _inproj(x, mod, ws, kv_norm_g, wukt, *, q_scale, qi_scale, wi_scale, Hi, tm):
    B, S, D = x.shape
    wq, wckv, wqi, wki, wwi, wu, wg = ws
    H, R, dh = wukt.shape
    row = lambda n: pl.BlockSpec((1, tm, n), lambda b, i: (b, i, 0))
    col = lambda n: pl.BlockSpec((1, n, tm), lambda b, i: (b, 0, i))
    full = lambda a: pl.BlockSpec(a.shape, lambda b, i: (0,) * a.ndim)
    out_specs = [pl.BlockSpec((1, H, R, tm), lambda b, i: (b, 0, 0, i)), row(2 * R), col(R + ONES_ROWS),
                 col(wqi.shape[1]),
                 row(wki.shape[1]), col(Hi), row(wu.shape[1]), row(wg.shape[1])]
    out_shape = [jax.ShapeDtypeStruct((B, H, R, S), BF16), jax.ShapeDtypeStruct((B, S, 2 * R), BF16),
                 jax.ShapeDtypeStruct((B, R + ONES_ROWS, S), BF16),
                 jax.ShapeDtypeStruct((B, wqi.shape[1], S), BF16),
                 jax.ShapeDtypeStruct((B, S, wki.shape[1]), BF16), jax.ShapeDtypeStruct((B, Hi, S), F32),
                 jax.ShapeDtypeStruct((B, S, wu.shape[1]), F32), jax.ShapeDtypeStruct((B, S, wg.shape[1]), BF16)]
    return pl.pallas_call(
        functools.partial(_inproj_kernel, tm=tm, H=H, dh=dh, Hi=Hi, q_scale=q_scale, qi_scale=qi_scale,
                          wi_scale=wi_scale),
        grid=(B, S // tm),
        in_specs=[row(D), pl.BlockSpec((1, N_MOD, D), lambda b, i: (b, 0, 0))]
                 + [full(w) for w in ws] + [full(kv_norm_g), full(wukt)],
        out_specs=out_specs,
        out_shape=out_shape,
        compiler_params=pltpu.CompilerParams(
            dimension_semantics=("parallel", "parallel"), vmem_limit_bytes=48 * 2 ** 20),
    )(x, mod, *ws, kv_norm_g, wukt)


def _key_to_f32(t):
    return pltpu.bitcast(jnp.where(t >= 0, t, t ^ 0x7FFFFFFF), F32)


def _sort_network(n):
    pairs, p = [], 1
    while p < n:
        k = p
        while k >= 1:
            for j in range(k % p, n - k, 2 * k):
                for i in range(min(k, n - j - k)):
                    if (i + j) // (2 * p) == (i + j + k) // (2 * p):
                        pairs.append((i + j, i + j + k))
            k //= 2
        p *= 2
    return pairs


def _count_ge_sorted(blocks, cand):
    n = len(blocks)

    def pick(level, bits):
        if len(level) == 1:
            return level[0]
        half = len(level) // 2
        return jnp.where(bits[0], pick(level[half:], bits[1:]), pick(level[:half], bits[1:]))

    total = jnp.where(blocks[n - 1] >= cand, 1.0, 0.0)
    bits = []
    step = n // 2
    while step >= 1:
        level = [blocks[c + step - 1] for c in range(0, n, 2 * step)]
        b = pick(level, bits) >= cand
        bits.append(b)
        total = total + jnp.where(b, float(step), 0.0)
        step //= 2
    return total


def _attn_body(qit_ref, wit_ref, ki_ref, qlt_ref, aug_ref, ckva_ref, ckvt_ref, wuvt_ref, yt_ref,
               score_scr, bias_scr, lg_scr, o_scr, sort_scr, *, W, tq, S, H, dh, Hi, di, topk, cw):
    j = pl.program_id(1)
    qpos = j * tq + lax.broadcasted_iota(I32, (1, tq), 1)
    score = score_scr.at[0:W, :]

    wit = wit_ref[0]
    for c in range(W // cw):
        kic = ki_ref[0, c * cw:(c + 1) * cw, :]
        acc = jnp.zeros((cw, tq), F32)
        for h in range(Hi):
            d = jnp.dot(kic, qit_ref[0, h * di:(h + 1) * di, :], preferred_element_type=F32)
            acc = acc + wit[h:h + 1, :] * jnp.maximum(d, 0.0)
        kpos_c = c * cw + lax.broadcasted_iota(I32, (cw, tq), 0)
        score_scr[c * cw:(c + 1) * cw, :] = jnp.where(kpos_c <= qpos, acc, -jnp.inf)

    def count(hit):
        return jnp.sum(jnp.where(hit, 1.0, 0.0), axis=0, keepdims=True)

    Wb = W // SORT_GROUP
    net = _sort_network(SORT_GROUP)

    def column(r, lg):
        return [(slice(g * Wb + r * SUBLANES, g * Wb + (r + 1) * SUBLANES), slice(lg * LANES, (lg + 1) * LANES))
                for g in range(SORT_GROUP)]

    for r in range(Wb // SUBLANES):
        for lg in range(tq // LANES):
            v = [score_scr[idx] for idx in column(r, lg)]
            for a, b in net:
                v[a], v[b] = jnp.maximum(v[a], v[b]), jnp.minimum(v[a], v[b])
            for idx, val in zip(column(r, lg), v):
                sort_scr[idx] = val

    def count_ge(cand_f):
        parts = []
        for lg in range(tq // LANES):
            c = cand_f[:, lg * LANES:(lg + 1) * LANES]
            acc = None
            for r in range(Wb // SUBLANES):
                n = _count_ge_sorted([sort_scr[idx] for idx in column(r, lg)], c)
                acc = n if acc is None else acc + n
            parts.append(jnp.sum(acc, axis=0, keepdims=True))
        return jnp.concatenate(parts, axis=1)

    def thr_step(i, t):
        cand = t + lax.shift_left(jnp.int32(1), 31 - i)
        cnt = count_ge(_key_to_f32(cand))
        return jnp.where(cnt >= float(topk), cand, t)

    t = lax.fori_loop(0, 32, thr_step, jnp.full((1, tq), INT_MIN, I32))
    few = t == INT_MIN
    thr_f = _key_to_f32(t)
    up_f = _key_to_f32(t + 1)
    need = float(topk) - count(score[...] >= up_f)

    tc = min(TIE_CHUNK, W)
    earlier = (lax.broadcasted_iota(I32, (tc, tc), 1) < lax.broadcasted_iota(I32, (tc, tc), 0)).astype(BF16)
    tied_before = jnp.zeros((1, tq), F32)
    for c in range(W // tc):
        rows = slice(c * tc, (c + 1) * tc)
        s = score_scr[rows, :]
        gt = s >= up_f
        eq = (s >= thr_f) & jnp.logical_not(gt)
        eq_f = jnp.where(eq, 1.0, 0.0)
        rank = jnp.dot(earlier, eq_f.astype(BF16), preferred_element_type=F32) + tied_before
        tied_before = tied_before + jnp.sum(eq_f, axis=0, keepdims=True)
        kpos = c * tc + lax.broadcasted_iota(I32, (tc, tq), 0)
        sel = (few | gt | (eq & (rank < need))) & (kpos <= qpos)
        bias_scr[rows, :] = jnp.where(sel, 0.0, NEG_BIG)

    R = qlt_ref.shape[2]
    ck = min(ATTN_KEY_CHUNK, W)

    def head(h, carry):
        qa = jnp.concatenate([qlt_ref[0, h], aug_ref[h]], axis=0)
        m = None
        for c in range(W // ck):
            keys = slice(c * ck, (c + 1) * ck)
            lg = jnp.dot(ckva_ref[0, keys, :], qa, preferred_element_type=F32) + bias_scr[keys, :]
            lg_scr[keys, :] = lg
            mc = jnp.max(lg, axis=0, keepdims=True)
            m = mc if m is None else jnp.maximum(m, mc)
        acc = jnp.zeros((R + ONES_ROWS, tq), F32)
        for c in range(W // ck):
            keys = slice(c * ck, (c + 1) * ck)
            p = jnp.exp2(lg_scr[keys, :] - m)
            acc = acc + jnp.dot(ckvt_ref[0, :, keys], p.astype(BF16), preferred_element_type=F32)
        o_scr[h] = (acc[0:R] / acc[R:R + 1]).astype(BF16)
        return carry

    lax.fori_loop(0, H, head, 0, unroll=ATTN_HEAD_UNROLL)
    for h in range(H):
        yt = jnp.dot(wuvt_ref[h], o_scr[h], preferred_element_type=F32)
        yt_ref[0, h * dh:(h + 1) * dh, :] = yt.astype(BF16)


def _attn_kernel(*refs, tq, cw, S, **kw):
    j = pl.program_id(1)
    nc = ((j + 1) * tq + cw - 1) // cw
    for w in range(1, S // cw + 1):
        @pl.when(nc == w)
        def _():
            _attn_body(*refs, W=w * cw, tq=tq, S=S, cw=cw, **kw)


def _attention(qit, wit, ki, qlt, ckva, ckvt, wuvt, *, tq, topk):
    B, H, R, S = qlt.shape
    dh = wuvt.shape[1]
    di = ki.shape[2]
    Hi = wit.shape[1]
    slopes = jnp.asarray([LOG2E * 2.0 ** (-8.0 * (h + 1) / H) for h in range(H)], F32)
    s_hi = slopes.astype(BF16)
    s_lo = (slopes - s_hi.astype(F32)).astype(BF16)
    aug = jnp.zeros((H, R, tq), BF16)
    aug = aug.at[:, 0:2, :].set(s_hi[:, None, None]).at[:, 2:4, :].set(s_lo[:, None, None])
    kern = functools.partial(_attn_kernel, tq=tq, S=S, H=H, dh=dh, Hi=Hi, di=di, topk=topk, cw=min(S, 512))
    col = lambda n: pl.BlockSpec((1, n, tq), lambda b, j: (b, 0, j))
    full = lambda a: pl.BlockSpec(a.shape, lambda b, j: (0,) * a.ndim)
    return pl.pallas_call(
        kern,
        grid=(B, S // tq),
        in_specs=[col(qit.shape[1]), col(Hi), pl.BlockSpec((1, S, di), lambda b, j: (b, 0, 0)),
                  pl.BlockSpec((1, H, R, tq), lambda b, j: (b, 0, 0, j)), full(aug),
                  pl.BlockSpec((1, S, 2 * R), lambda b, j: (b, 0, 0)),
                  pl.BlockSpec((1, R + ONES_ROWS, S), lambda b, j: (b, 0, 0)), full(wuvt)],
        out_specs=col(H * dh),
        out_shape=jax.ShapeDtypeStruct((B, H * dh, S), BF16),
        scratch_shapes=[pltpu.VMEM((S, tq), F32), pltpu.VMEM((S, tq), F32), pltpu.VMEM((S, tq), F32),
                        pltpu.VMEM((H, R, tq), BF16), pltpu.VMEM((S, tq), F32)],
        compiler_params=pltpu.CompilerParams(
            dimension_semantics=("parallel", "parallel"), vmem_limit_bytes=48 * 2 ** 20),
    )(qit, wit, ki, qlt, aug, ckva, ckvt, wuvt)


def _merge_kernel(x_ref, yat_ref, u_ref, halo_ref, g_ref, mod_ref, wpg_ref, psc_ref, wba_ref, wbp_ref,
                  wout_ref, ln1g_ref, ln1b_ref, wr_ref, br_ref,
                  x1_o, h2_o, te_o, tg_o, ext_scr, *, tm, D, G, alpha, E, K):
    i = pl.program_id(1)

    @pl.when(i == 0)
    def _():
        ext_scr[0:POOL_HALO, :] = jnp.zeros((POOL_HALO, ext_scr.shape[1]), F32)

    @pl.when(i > 0)
    def _():
        ext_scr[0:POOL_HALO, :] = halo_ref[0]

    ext_scr[POOL_HALO:POOL_HALO + tm, :] = u_ref[0]

    pos = i * tm + lax.broadcasted_iota(I32, (tm, 1), 0)
    ys = []
    for g, w in enumerate(POOL_WINDOWS):
        cols = slice(g * G, (g + 1) * G)
        cur = ext_scr[POOL_HALO:POOL_HALO + tm, cols]
        acc = cur
        for back in range(1, w):
            acc = acc + ext_scr[POOL_HALO - back:POOL_HALO - back + tm, cols]
        cnt = jnp.minimum(pos + 1, w).astype(F32)
        pooled = acc / cnt - cur
        ys.append(jnp.dot(pooled.astype(BF16), wpg_ref[g], preferred_element_type=F32))
    y_pool = jnp.concatenate(ys, axis=1) * psc_ref[...]

    gates = g_ref[0]
    ya = yat_ref[0].astype(F32).T.astype(BF16)
    a = jnp.dot(ya, wba_ref[...], preferred_element_type=F32)
    p = jnp.dot(y_pool.astype(BF16), wbp_ref[...], preferred_element_type=F32)
    mix = _sigmoid(gates[:, :D].astype(F32)) * a + _sigmoid(gates[:, D:].astype(F32)) * p
    o = jnp.dot(mix.astype(BF16), wout_ref[...], preferred_element_type=F32)
    mod = mod_ref[0]
    x1 = _layer_norm(alpha * x_ref[0] + mod[2:3] * o, ln1g_ref[...], ln1b_ref[...])
    x1_o[0] = x1
    h2 = x1 * (1.0 + mod[4:5]) + mod[3:4]
    _store_row_slabs(h2_o.at[0], h2)

    h2_hi = h2.astype(BF16)
    h2_lo = (h2 - h2_hi.astype(F32)).astype(BF16)
    logits = (jnp.dot(h2_hi, wr_ref[0], preferred_element_type=F32)
              + jnp.dot(h2_hi, wr_ref[1], preferred_element_type=F32)
              + jnp.dot(h2_lo, wr_ref[0], preferred_element_type=F32)) + br_ref[...]
    lane = lax.broadcasted_iota(I32, logits.shape, 1)
    logits = jnp.where(lane < E, logits, -jnp.inf)
    lane_f = lane.astype(F32)
    te_f = jnp.zeros(logits.shape, F32)
    vals = []
    for k in range(K):
        m = jnp.max(logits, axis=1, keepdims=True)
        idx = jnp.min(jnp.where(logits == m, lane_f, float(LANES)), axis=1, keepdims=True)
        te_f = jnp.where(lane == k, idx, te_f)
        vals.append(m)
        logits = jnp.where(lane_f == idx, -jnp.inf, logits)
    te = te_f.astype(I32)
    ex = [jnp.exp(v - vals[0]) for v in vals]
    den = ex[0]
    for e in ex[1:]:
        den = den + e
    tg = jnp.zeros(logits.shape, F32)
    for k in range(K):
        tg = jnp.where(lane == k, ex[k] / den, tg)
    te_o[0] = te
    tg_o[0] = tg


def _merge(x, yat, u, g, mod, wpg, psc, wba, wbp, wout, ln1g, ln1b, wr, br, *, tm, alpha, E, K):
    B, S, D = x.shape
    PW = u.shape[2]
    G = wpg.shape[1]
    kern = functools.partial(_merge_kernel, tm=tm, D=D, G=G, alpha=alpha, E=E, K=K)
    row = lambda n: pl.BlockSpec((1, tm, n), lambda b, i: (b, i, 0))
    full = lambda a: pl.BlockSpec(a.shape, lambda b, i: (0,) * a.ndim)
    hb = tm // POOL_HALO
    halo = pl.BlockSpec((1, POOL_HALO, PW), lambda b, i: (b, jnp.maximum(i * hb - 1, 0), 0))
    return pl.pallas_call(
        kern,
        grid=(B, S // tm),
        in_specs=[row(D), pl.BlockSpec((1, yat.shape[1], tm), lambda b, i: (b, 0, i)), row(PW), halo,
                  row(g.shape[2]), pl.BlockSpec((1, N_MOD, D), lambda b, i: (b, 0, 0)),
                  full(wpg), full(psc), full(wba), full(wbp), full(wout), full(ln1g), full(ln1b),
                  full(wr), full(br)],
        out_specs=[row(D), pl.BlockSpec((1, tm * (D // LANES), LANES), lambda b, i: (b, i, 0)),
                   row(LANES), row(LANES)],
        out_shape=[jax.ShapeDtypeStruct((B, S, D), F32),
                   jax.ShapeDtypeStruct((B, S * (D // LANES), LANES), F32),
                   jax.ShapeDtypeStruct((B, S, LANES), I32), jax.ShapeDtypeStruct((B, S, LANES), F32)],
        scratch_shapes=[pltpu.VMEM((POOL_HALO + tm, PW), F32)],
        compiler_params=pltpu.CompilerParams(
            dimension_semantics=("parallel", "arbitrary"), vmem_limit_bytes=48 * 2 ** 20),
    )(x, yat, u, u, g, mod, wpg, psc, wba, wbp, wout, ln1g, ln1b, wr, br)


def _route_kernel(te_ref, rank_o, cnt_o, base_scr, *, tR, K):
    i = pl.program_id(0)

    @pl.when(i == 0)
    def _():
        base_scr[...] = jnp.zeros(base_scr.shape, F32)

    te = te_ref[...]
    lane = lax.broadcasted_iota(I32, te.shape, 1)
    hits = [lane == te[:, k:k + 1] for k in range(K)]
    onehot = hits[0].astype(F32)
    for k in range(1, K):
        onehot = onehot + hits[k].astype(F32)
    earlier = (lax.broadcasted_iota(I32, (tR, tR), 0) > lax.broadcasted_iota(I32, (tR, tR), 1)).astype(BF16)
    before = jnp.dot(earlier, onehot.astype(BF16), preferred_element_type=F32) + base_scr[...]
    rank = jnp.zeros(te.shape, I32)
    for k in range(K):
        rk = jnp.sum(jnp.where(hits[k], before, 0.0), axis=1, keepdims=True)
        rank = jnp.where(lane == k, rk.astype(I32), rank)
    rank_o[...] = rank
    base_scr[...] = base_scr[...] + jnp.sum(onehot, axis=0, keepdims=True)
    cnt_o[...] = base_scr[...].astype(I32)


def _route(te, *, tR, K):
    T = te.shape[0]
    return pl.pallas_call(
        functools.partial(_route_kernel, tR=tR, K=K),
        grid=(T // tR,),
        in_specs=[pl.BlockSpec((tR, LANES), lambda i: (i, 0))],
        out_specs=[pl.BlockSpec((tR, LANES), lambda i: (i, 0)), pl.BlockSpec((1, LANES), lambda i: (0, 0))],
        out_shape=[jax.ShapeDtypeStruct((T, LANES), I32), jax.ShapeDtypeStruct((1, LANES), I32)],
        scratch_shapes=[pltpu.VMEM((1, LANES), F32)],
        compiler_params=pltpu.CompilerParams(dimension_semantics=("arbitrary",)),
    )(te)


def _dispatch_kernel(pe_ref, nu_ref, pos_ref, h2_ref, xs_out, zero_scr, sem, zsem,
                     *, tD, K, sub, E, tmE, n_blocks):
    blk = tmE * sub

    @pl.when(pl.program_id(0) == 0)
    def _():
        zero_scr[...] = jnp.zeros(zero_scr.shape, zero_scr.dtype)

        def zero_block(b):
            return pltpu.make_async_copy(zero_scr, xs_out.at[pl.ds(pl.multiple_of(b * blk, blk), blk), :], zsem)

        def last_block(e):
            return jnp.maximum(pe_ref[e] // tmE - 1, 0)

        def start_last(e, carry):
            zero_block(last_block(e)).start()
            return carry

        def start_tail(b, carry):
            zero_block(b).start()
            return carry

        def wait_last(e, carry):
            zero_block(last_block(e)).wait()
            return carry

        def wait_tail(b, carry):
            zero_block(b).wait()
            return carry

        lax.fori_loop(0, E, start_last, 0)
        lax.fori_loop(nu_ref[0], n_blocks, start_tail, 0)
        lax.fori_loop(0, E, wait_last, 0)
        lax.fori_loop(nu_ref[0], n_blocks, wait_tail, 0)

    def issue(r, carry):
        src = h2_ref.at[pl.ds(pl.multiple_of(r * sub, sub), sub), :]
        for k in range(K):
            p = pl.multiple_of(pos_ref[0, 0, k * tD + r], sub)
            pltpu.async_copy(src, xs_out.at[pl.ds(p, sub), :], sem, priority=k % 2)
        return carry

    lax.fori_loop(0, tD, issue, 0, unroll=8)
    for k in range(K):
        pltpu.make_async_copy(h2_ref, xs_out.at[pl.ds(0, tD * sub), :], sem).wait()


def _dispatch(pad_end, n_used, pos, h2, n_blocks, *, tD, tmE, K, sub):
    T = h2.shape[0] // sub
    E = pad_end.shape[0]
    grid_spec = pltpu.PrefetchScalarGridSpec(
        num_scalar_prefetch=2,
        grid=(T // tD,),
        in_specs=[pl.BlockSpec((1, 1, K * tD), lambda i, pe, nu: (i, 0, 0), memory_space=pltpu.SMEM),
                  pl.BlockSpec((tD * sub, LANES), lambda i, pe, nu: (i, 0))],
        out_specs=pl.BlockSpec(memory_space=pl.ANY),
        scratch_shapes=[pltpu.VMEM((tmE * sub, LANES), h2.dtype), pltpu.SemaphoreType.DMA(()),
                        pltpu.SemaphoreType.DMA(())],
    )
    return pl.pallas_call(
        functools.partial(_dispatch_kernel, tD=tD, K=K, sub=sub, E=E, tmE=tmE, n_blocks=n_blocks),
        grid_spec=grid_spec,
        out_shape=jax.ShapeDtypeStruct((n_blocks * tmE * sub, LANES), h2.dtype),
        compiler_params=pltpu.CompilerParams(
            dimension_semantics=("arbitrary",), has_side_effects=True),
    )(pad_end, n_used, pos, h2)


def _expert_kernel(be_ref, nu_ref, nx_ref, sl_ref, x_ref, wgu_hbm, bgu_ref, wd_hbm, bd_ref, y_ref,
                   wgu_f32, wd_f32, wgu_bf, wd_bf, sem, *, F, tmE, sub):
    i = pl.program_id(0)
    used = i < nu_ref[0]
    e = be_ref[i]
    slot = sl_ref[i]
    new_expert = used & ((i == 0) | (e != be_ref[jnp.maximum(i - 1, 0)]))

    def fetch(expert, s):
        return (pltpu.make_async_copy(wgu_hbm.at[expert], wgu_f32.at[s], sem.at[s, 0]),
                pltpu.make_async_copy(wd_hbm.at[expert], wd_f32.at[s], sem.at[s, 1]))

    @pl.when(used & (i == 0))
    def _():
        for cp in fetch(e, slot):
            cp.start()

    @pl.when(new_expert)
    def _():
        for cp in fetch(e, slot):
            cp.wait()

        @pl.when(nx_ref[i] >= 0)
        def _():
            for cp in fetch(nx_ref[i], 1 - slot):
                cp.start()

        wgu_bf[...] = wgu_f32[slot].astype(BF16)
        wd_bf[...] = wd_f32[slot].astype(BF16)

    @pl.when(used)
    def _():
        x = _load_row_slabs(x_ref, tmE, sub).astype(BF16)
        gu = jnp.dot(x, wgu_bf[...], preferred_element_type=F32) + bgu_ref[0]
        glu = jnp.minimum(gu[:, :F], SWIGLU_LIMIT)
        lin = jnp.clip(gu[:, F:], -SWIGLU_LIMIT, SWIGLU_LIMIT)
        act = glu * jax.nn.sigmoid(SWIGLU_ALPHA * glu) * (lin + 1.0)
        y = jnp.dot(act.astype(BF16), wd_bf[...], preferred_element_type=F32) + bd_ref[0]
        _store_row_slabs(y_ref, y)

    @pl.when(jnp.logical_not(used))
    def _():
        y_ref[...] = jnp.zeros(y_ref.shape, F32)


def _experts(block_e, n_used, next_e, slot, xs, wgu, bgu, wd, bd, *, tmE):
    E, D, F2 = wgu.shape
    F = F2 // 2
    sub = D // LANES
    n_blocks = xs.shape[0] // (tmE * sub)
    slab_block = pl.BlockSpec((tmE * sub, LANES), lambda i, be, nu, nx, sl: (i, 0))
    grid_spec = pltpu.PrefetchScalarGridSpec(
        num_scalar_prefetch=4,
        grid=(n_blocks,),
        in_specs=[slab_block,
                  pl.BlockSpec(memory_space=pl.ANY),
                  pl.BlockSpec((1, 1, F2), lambda i, be, nu, nx, sl: (be[i], 0, 0)),
                  pl.BlockSpec(memory_space=pl.ANY),
                  pl.BlockSpec((1, 1, D), lambda i, be, nu, nx, sl: (be[i], 0, 0))],
        out_specs=slab_block,
        scratch_shapes=[pltpu.VMEM((2, D, F2), F32), pltpu.VMEM((2, F, D), F32),
                        pltpu.VMEM((D, F2), BF16), pltpu.VMEM((F, D), BF16),
                        pltpu.SemaphoreType.DMA((2, 2))],
    )
    return pl.pallas_call(
        functools.partial(_expert_kernel, F=F, tmE=tmE, sub=sub),
        grid_spec=grid_spec,
        out_shape=jax.ShapeDtypeStruct(xs.shape, F32),
        compiler_params=pltpu.CompilerParams(
            dimension_semantics=("arbitrary",), vmem_limit_bytes=56 * 2 ** 20),
    )(block_e, n_used, next_e, slot, xs, wgu, bgu, wd, bd)


def _combine_kernel(pos_ref, nxt_ref, y_hbm, x1_ref, tg_ref, mod_ref, g_ref, b_ref, o_ref, buf0, buf1, sem,
                    *, tmC, K, alpha, nt, sub):
    i = pl.program_id(0)
    rows_per_group = tmC // COMBINE_GROUPS

    def start_row(p_ref, dst, s, r):
        row0 = r * sub if isinstance(r, int) else pl.multiple_of(r * sub, sub)
        for k in range(K):
            p = pl.multiple_of(p_ref[0, 0, k * tmC + r], sub)
            pltpu.async_copy(y_hbm.at[pl.ds(p, sub), :], dst.at[k, pl.ds(row0, sub), :], sem.at[s],
                             priority=k % 2)

    def reduce_rows(cur, g):
        rows = slice(g * rows_per_group, (g + 1) * rows_per_group)
        tg = tg_ref[rows, :]
        y = None
        for k in range(K):
            part = _load_row_slabs(cur.at[k, pl.ds(g * rows_per_group * sub, rows_per_group * sub), :],
                                   rows_per_group, sub)
            y = tg[:, k:k + 1] * part if y is None else y + tg[:, k:k + 1] * part
        mod = mod_ref[0]
        o_ref[rows, :] = _layer_norm(alpha * x1_ref[rows, :] + mod[5:6] * y, g_ref[...], b_ref[...])

    def step(cur, cur_s, nxt, nxt_s, prefetch):
        for k in range(K):
            pltpu.make_async_copy(y_hbm.at[pl.ds(0, tmC * sub), :], cur.at[k], sem.at[cur_s]).wait()
        for g in range(COMBINE_GROUPS):
            if prefetch:
                for r in range(g * rows_per_group, (g + 1) * rows_per_group):
                    start_row(nxt_ref, nxt, nxt_s, r)
            reduce_rows(cur, g)

    @pl.when(i == 0)
    def _():
        def issue(r, carry):
            start_row(pos_ref, buf0, 0, r)
            return carry
        lax.fori_loop(0, tmC, issue, 0, unroll=8)

    even = lax.rem(i, 2) == 0
    last = i + 1 == nt
    for is_even, (cur, cur_s, nxt, nxt_s) in ((True, (buf0, 0, buf1, 1)), (False, (buf1, 1, buf0, 0))):
        for is_last in (False, True):
            @pl.when((even == is_even) & (last == is_last))
            def _():
                step(cur, cur_s, nxt, nxt_s, prefetch=not is_last)


def _combine(pos, y_sorted, x1, tg, mod, ln2g, ln2b, *, tmC, K, alpha, S):
    T, D = x1.shape
    nt = T // tmC
    per_seq = S // tmC
    sub = D // LANES
    return pl.pallas_call(
        functools.partial(_combine_kernel, tmC=tmC, K=K, alpha=alpha, nt=nt, sub=sub),
        grid=(nt,),
        in_specs=[pl.BlockSpec((1, 1, K * tmC), lambda i: (i, 0, 0), memory_space=pltpu.SMEM),
                  pl.BlockSpec((1, 1, K * tmC), lambda i: (jnp.minimum(i + 1, nt - 1), 0, 0),
                               memory_space=pltpu.SMEM),
                  pl.BlockSpec(memory_space=pl.ANY),
                  pl.BlockSpec((tmC, D), lambda i: (i, 0)),
                  pl.BlockSpec((tmC, LANES), lambda i: (i, 0)),
                  pl.BlockSpec((1, N_MOD, D), lambda i: (i // per_seq, 0, 0)),
                  pl.BlockSpec((1, D), lambda i: (0, 0)),
                  pl.BlockSpec((1, D), lambda i: (0, 0))],
        out_specs=pl.BlockSpec((tmC, D), lambda i: (i, 0)),
        out_shape=jax.ShapeDtypeStruct((T, D), F32),
        scratch_shapes=[pltpu.VMEM((K, tmC * sub, LANES), F32), pltpu.VMEM((K, tmC * sub, LANES), F32),
                        pltpu.SemaphoreType.DMA((2,))],
        compiler_params=pltpu.CompilerParams(
            dimension_semantics=("arbitrary",), vmem_limit_bytes=48 * 2 ** 20),
    )(pos, pos, y_sorted, x1, tg, mod, ln2g, ln2b)


def _routing_tables(top_e, rank, counts, E, tmE, sub):
    T, K = top_e.shape
    padded = (counts + tmE - 1) // tmE * tmE
    pad_end = jnp.cumsum(padded)
    pad_start = pad_end - padded
    n_blocks = -(-(T * K) // tmE) + E
    block_start = jnp.arange(n_blocks, dtype=I32) * tmE
    block_e = jnp.minimum(jnp.sum((pad_end[None, :] <= block_start[:, None]).astype(I32), axis=1), E - 1)
    n_used = (pad_end[-1] // tmE).astype(I32).reshape(1)
    ids = jnp.arange(E, dtype=I32)

    def lookup(table, idx):
        return jnp.sum(jnp.where(idx[..., None] == ids, table.astype(I32), 0), axis=-1)

    pos = (lookup(pad_start, top_e) + rank).astype(I32) * sub
    nonempty = counts > 0
    at_or_after = jnp.flip(lax.cummin(jnp.flip(jnp.where(nonempty, ids, E))))
    after = jnp.concatenate([at_or_after[1:], jnp.full((1,), E, I32)])
    next_e = lookup(jnp.where(after < E, after, -1), block_e)
    slot = lookup((jnp.cumsum(nonempty.astype(I32)) - 1) % 2, block_e)
    return block_e.astype(I32), n_used, next_e, slot, pad_end.astype(I32), pos, n_blocks


def _tile_major(pos, tile):
    T, K = pos.shape
    return pos.reshape(T // tile, tile, K).transpose(0, 2, 1).reshape(T // tile, 1, K * tile)


def _layer(x, mod, w_in, kv_norm_g, w_uk, w_uv, w_pool_group, pool_scale, w_branch_attn, w_branch_pool,
           w_out, ln1_g, ln1_b, w_router, b_router, w_gate_up, b_gate_up, w_down, b_down, ln2_g, ln2_b,
           *, alpha):
    B, S, D = x.shape
    R, H, dh = w_uk.shape
    AW = H * dh
    PW = pool_scale.shape[0]
    E = w_router.shape[1]
    K = TOP_K_EXPERTS
    n_idx = w_in.shape[1] - (AW + R + PW + 2 * D)
    di = dh
    Hi = (n_idx - di) // (di + 1)
    sizes = (AW, R, Hi * di, di, Hi, PW, 2 * D)
    offs = [0]
    for s in sizes:
        offs.append(offs[-1] + s)
    ws = [w_in[:, offs[k]:offs[k + 1]].astype(BF16) for k in range(len(sizes))]
    ws[4] = jnp.pad(ws[4], ((0, 0), (0, LANES - Hi)))

    wukt = w_uk.transpose(1, 0, 2).astype(BF16)
    wuvt = w_uv.transpose(1, 2, 0).astype(BF16)
    qlt, ckva, ckvt, qit, ki, wit, u, g = _inproj(
        x, mod, ws, kv_norm_g.reshape(1, R), wukt, q_scale=dh ** -0.5 * LOG2E, qi_scale=di ** -0.5,
        wi_scale=Hi ** -0.5, Hi=Hi, tm=min(S, 512))

    topk = min(TOPK_MAX, S // 4)
    y_attn_t = _attention(qit, wit, ki, qlt, ckva, ckvt, wuvt, tq=min(S, 256), topk=topk)

    wr = jnp.pad(w_router, ((0, 0), (0, LANES - E)))
    wr_hi = wr.astype(BF16)
    wr = jnp.stack([wr_hi, (wr - wr_hi.astype(F32)).astype(BF16)])
    br = jnp.pad(b_router, (0, LANES - E)).reshape(1, LANES)
    x1, h2, te, tg = _merge(x, y_attn_t, u, g, mod, w_pool_group.astype(BF16), pool_scale.reshape(1, PW),
                            w_branch_attn.astype(BF16), w_branch_pool.astype(BF16), w_out.astype(BF16),
                            ln1_g.reshape(1, D), ln1_b.reshape(1, D), wr, br,
                            tm=min(S, 512), alpha=alpha, E=E, K=K)

    T = B * S
    tmE = 512
    tmC = min(S, 256)
    tD = min(T, 1024)
    sub = D // LANES
    te = te.reshape(T, LANES)
    rank, counts = _route(te, tR=min(T, 256), K=K)
    block_e, n_used, next_e, slot, pad_end, pos, n_blocks = _routing_tables(
        te[:, :K], rank[:, :K], counts[0, :E], E, tmE, sub)
    xs = _dispatch(pad_end, n_used, _tile_major(pos, tD), h2.reshape(T * sub, LANES), n_blocks,
                   tD=tD, tmE=tmE, K=K, sub=sub)
    y_sorted = _experts(block_e, n_used, next_e, slot, xs, w_gate_up, b_gate_up.reshape(E, 1, -1),
                        w_down, b_down.reshape(E, 1, D), tmE=tmE)
    out = _combine(_tile_major(pos, tmC), y_sorted, x1.reshape(T, D), tg.reshape(T, LANES), mod,
                   ln2_g.reshape(1, D), ln2_b.reshape(1, D), tmC=tmC, K=K, alpha=alpha, S=S)
    return out.reshape(B, S, D)


def kernel(x, c, w_ada, b_ada, w_in, kv_norm_g, w_uk, w_uv, w_pool_group, pool_scale, w_branch_attn,
           w_branch_pool, w_out, ln1_g, ln1_b, w_router, b_router, w_gate_up, b_gate_up, w_down, b_down,
           ln2_g, ln2_b):
    B, S, D = x.shape
    depth = w_ada.shape[0]
    alpha = (2.0 * depth) ** 0.25
    for l in range(depth):
        mod = _ada(c, w_ada[l], b_ada[l]).reshape(B, N_MOD, D)
        x = _layer(x, mod, w_in[l], kv_norm_g[l], w_uk[l], w_uv[l], w_pool_group[l], pool_scale[l],
                   w_branch_attn[l], w_branch_pool[l], w_out[l], ln1_g[l], ln1_b[l], w_router[l],
                   b_router[l], w_gate_up[l], b_gate_up[l], w_down[l], b_down[l], ln2_g[l], ln2_b[l],
                   alpha=alpha)
    return x
```
